```python
import math
import jax, jax.numpy as jnp
from jax import lax
import numpy as np

D_MODEL = 1024
BATCH = 8
SEQ = 4096
DEPTH = 4

ATTN_WIDTH = D_MODEL // 2
SSM_WIDTH = D_MODEL - ATTN_WIDTH
HEAD_DIM = 64
N_HEADS = ATTN_WIDTH // HEAD_DIM
SSM_GROUP = 16
N_SSM_GROUPS = SSM_WIDTH // SSM_GROUP
SSM_STATE = 64
IN_WIDTH = 3 * ATTN_WIDTH + SSM_WIDTH
D_FF = ((8 * D_MODEL // 3) + 127) // 128 * 128
CONV_WIDTH = 3
BLOCK_Q = 128
N_MOD = 6
EPS = 1e-6

kernel_name = "hybrid_stickbreak_s5_convffn_adaln"


def _rmsnorm(x, g):
    xf = x.astype(jnp.float32)
    inv = lax.rsqrt(jnp.mean(xf * xf, axis=-1, keepdims=True) + EPS)
    return (xf * inv * g.astype(jnp.float32)).astype(x.dtype)


def _modulate(h, shift, scale):
    return h * (1.0 + scale[:, None, :]) + shift[:, None, :]


def _stick_breaking(q, k, v):
    B, H, L, Dh = q.shape
    nb = L // BLOCK_Q
    qf = q.astype(jnp.float32)
    kf = k.astype(jnp.float32)
    vf = v.astype(jnp.float32)
    q_blocks = qf.reshape(B, H, nb, BLOCK_Q, Dh).transpose(2, 0, 1, 3, 4)
    k_pos = jnp.arange(L, dtype=jnp.int32)
    inv_sqrt = 1.0 / math.sqrt(Dh)

    def block_fn(args):
        qb, bi = args
        z = jnp.einsum('bhqd,bhkd->bhqk', qb, kf) * inv_sqrt
        q_pos = bi * BLOCK_Q + jnp.arange(BLOCK_Q, dtype=jnp.int32)
        mask = k_pos[None, :] < q_pos[:, None]
        log_1m = jnp.where(mask, jax.nn.log_sigmoid(-z), 0.0)
        tail = lax.cumsum(log_1m, axis=3, reverse=True) - log_1m
        w = jnp.where(mask, jnp.exp(jax.nn.log_sigmoid(z) + tail), 0.0)
        return jnp.einsum('bhqk,bhkd->bhqd', w, vf)

    o = lax.map(block_fn, (q_blocks, jnp.arange(nb, dtype=jnp.int32)))
    o = o.transpose(1, 0, 3, 2, 4).reshape(B, L, H * Dh)
    return o.astype(q.dtype)


def _s5(u, a_re, a_im, log_dt, b_re, b_im, c_re, c_im, d_skip, glu_w, glu_b):
    Bsz, L, _ = u.shape
    f32 = jnp.float32
    ug = u.astype(f32).reshape(Bsz, L, N_SSM_GROUPS, SSM_GROUP)
    dt = jnp.exp(log_dt.astype(f32))[:, None]
    ar = a_re.astype(f32)
    ai = a_im.astype(f32)
    mag = jnp.exp(dt * ar)
    abar_re = mag * jnp.cos(dt * ai)
    abar_im = mag * jnp.sin(dt * ai)
    em_re = abar_re - 1.0
    em_im = abar_im
    den = ar * ar + ai * ai
    f_re = (em_re * ar + em_im * ai) / den
    f_im = (em_im * ar - em_re * ai) / den
    br = b_re.astype(f32)
    bim = b_im.astype(f32)
    bb_re = f_re[..., None] * br - f_im[..., None] * bim
    bb_im = f_re[..., None] * bim + f_im[..., None] * br
    bu_re = jnp.einsum('blgh,gph->blgp', ug, bb_re)
    bu_im = jnp.einsum('blgh,gph->blgp', ug, bb_im)
    a_r = jnp.broadcast_to(abar_re, bu_re.shape)
    a_i = jnp.broadcast_to(abar_im, bu_re.shape)

    def combine(e1, e2):
        a1r, a1i, b1r, b1i = e1
        a2r, a2i, b2r, b2i = e2
        return (a2r * a1r - a2i * a1i,
                a2r * a1i + a2i * a1r,
                a2r * b1r - a2i * b1i + b2r,
                a2r * b1i + a2i * b1r + b2i)

    _, _, s_re, s_im = lax.associative_scan(combine, (a_r, a_i, bu_re, bu_im), axis=1)
    y = (jnp.einsum('ghp,blgp->blgh', c_re.astype(f32), s_re)
         - jnp.einsum('ghp,blgp->blgh', c_im.astype(f32), s_im)
         + d_skip.astype(f32) * ug)
    y = jax.nn.gelu(y)
    gate = jax.nn.sigmoid(jnp.einsum('blgh,ghk->blgk', y, glu_w.astype(f32)) + glu_b.astype(f32))
    return (y * gate).reshape(Bsz, L, SSM_WIDTH).astype(u.dtype)


def _causal_dwconv(h, w, b):
    L = h.shape[1]
    hp = jnp.pad(h, ((0, 0), (CONV_WIDTH - 1, 0), (0, 0)))
    out = b
    for i in range(CONV_WIDTH):
        out = out + hp[:, i:i + L, :] * w[i]
    return out


def setup_inputs(seed: int = 0) -> dict:
    key = jax.random.key(seed)
    ks = jax.random.split(key, 32)
    f32 = jnp.float32
    D, G, P, H = D_MODEL, N_SSM_GROUPS, SSM_STATE, SSM_GROUP
    nrm = lambda k, shape, s: jax.random.normal(k, shape, f32) * s
    n_idx = jnp.arange(P, dtype=f32)
    return {
        "x": jax.random.normal(ks[0], (BATCH, SEQ, D), f32),
        "c": jax.random.normal(ks[1], (BATCH, D), f32),
        "ada_w": nrm(ks[2], (DEPTH, D, N_MOD * D), 0.5 * D ** -0.5),
        "ada_b": nrm(ks[3], (DEPTH, N_MOD * D), 0.01),
        "norm1_g": 1.0 + nrm(ks[4], (DEPTH, D), 0.02),
        "w_in": nrm(ks[5], (DEPTH, D, IN_WIDTH), D ** -0.5),
        "q_norm_g": 1.0 + nrm(ks[6], (DEPTH, HEAD_DIM), 0.02),
        "k_norm_g": 1.0 + nrm(ks[7], (DEPTH, HEAD_DIM), 0.02),
        "ssm_a_re": -0.5 + nrm(ks[8], (DEPTH, G, P), 0.01),
        "ssm_a_im": math.pi * n_idx + nrm(ks[9], (DEPTH, G, P), 0.01),
        "ssm_log_dt": jax.random.uniform(ks[10], (DEPTH, G), f32, math.log(1e-3), math.log(1e-1)),
        "ssm_b_re": nrm(ks[11], (DEPTH, G, P, H), (2.0 * H) ** -0.5),
        "ssm_b_im": nrm(ks[12], (DEPTH, G, P, H), (2.0 * H) ** -0.5),
        "ssm_c_re": nrm(ks[13], (DEPTH, G, H, P), (2.0 * P) ** -0.5),
        "ssm_c_im": nrm(ks[14], (DEPTH, G, H, P), (2.0 * P) ** -0.5),
        "ssm_d": nrm(ks[15], (DEPTH, G, H), 1.0),
        "glu_w": nrm(ks[16], (DEPTH, G, H, H), H ** -0.5),
        "glu_b": nrm(ks[17], (DEPTH, G, H), 0.01),
        "attn_out_g": 1.0 + nrm(ks[18], (DEPTH, ATTN_WIDTH), 0.02),
        "ssm_out_g": 1.0 + nrm(ks[19], (DEPTH, SSM_WIDTH), 0.02),
        "w_out": nrm(ks[20], (DEPTH, D, D), D ** -0.5),
        "norm2_g": 1.0 + nrm(ks[21], (DEPTH, D), 0.02),
        "ffn_w_up": nrm(ks[22], (DEPTH, D, 2 * D_FF), D ** -0.5),
        "ffn_conv_w": nrm(ks[23], (DEPTH, CONV_WIDTH, 2 * D_FF), CONV_WIDTH ** -0.5),
        "ffn_conv_b": nrm(ks[24], (DEPTH, 2 * D_FF), 0.01),
        "ffn_w_down": nrm(ks[25], (DEPTH, D_FF, D), D_FF ** -0.5),
    }


def reference(x, c, ada_w, ada_b, norm1_g, w_in, q_norm_g, k_norm_g,
              ssm_a_re, ssm_a_im, ssm_log_dt, ssm_b_re, ssm_b_im, ssm_c_re, ssm_c_im,
              ssm_d, glu_w, glu_b, attn_out_g, ssm_out_g, w_out, norm2_g,
              ffn_w_up, ffn_conv_w, ffn_conv_b, ffn_w_down):
    B, L, D = x.shape
    c_act = jax.nn.silu(c)
    for l in range(DEPTH):
        mod = c_act @ ada_w[l] + ada_b[l]
        sh1, sc1, g1, sh2, sc2, g2 = jnp.split(mod, N_MOD, axis=-1)

        h = _modulate(_rmsnorm(x, norm1_g[l]), sh1, sc1)
        p = h @ w_in[l]
        q = p[..., :ATTN_WIDTH].reshape(B, L, N_HEADS, HEAD_DIM)
        k = p[..., ATTN_WIDTH:2 * ATTN_WIDTH].reshape(B, L, N_HEADS, HEAD_DIM)
        v = p[..., 2 * ATTN_WIDTH:3 * ATTN_WIDTH].reshape(B, L, N_HEADS, HEAD_DIM)
        u = p[..., 3 * ATTN_WIDTH:]
        q = _rmsnorm(q, q_norm_g[l]).transpose(0, 2, 1, 3)
        k = _rmsnorm(k, k_norm_g[l]).transpose(0, 2, 1, 3)
        v = v.transpose(0, 2, 1, 3)
        o_attn = _stick_breaking(q, k, v)
        o_ssm = _s5(u, ssm_a_re[l], ssm_a_im[l], ssm_log_dt[l], ssm_b_re[l], ssm_b_im[l],
                    ssm_c_re[l], ssm_c_im[l], ssm_d[l], glu_w[l], glu_b[l])
        o = jnp.concatenate([_rmsnorm(o_attn, attn_out_g[l]),
                             _rmsnorm(o_ssm, ssm_out_g[l])], axis=-1)
        x = x + g1[:, None, :] * (o @ w_out[l])

        h = _modulate(_rmsnorm(x, norm2_g[l]), sh2, sc2)
        up = _causal_dwconv(h @ ffn_w_up[l], ffn_conv_w[l], ffn_conv_b[l])
        val, gate = jnp.split(up, 2, axis=-1)
        x = x + g2[:, None, :] * ((jax.nn.gelu(gate) * val) @ ffn_w_down[l])
    return x
```

```python
import functools
import math

import jax
import jax.numpy as jnp
from jax import lax
from jax.experimental import pallas as pl
from jax.experimental.pallas import tpu as pltpu

F32 = jnp.float32
BF16 = jnp.bfloat16

EPS = 1e-6
HEAD_DIM = 64
SSM_GROUP = 16
SSM_STATE = 64
CONV_WIDTH = 3
N_MOD = 6

LANES = 128
SUBLANES = 8
VMEM_LIMIT_BYTES = 56 * 1024 * 1024

ADA_TN = 1536
PROJ_TM = 512
ATT_T = 128
SSM_T = 64
FFN_TM = 512
FFN_CK = 256
FFN_HALO = 16

UNDERFLOW_BOUND = 106.0


def _cparams(sem):
    return pltpu.CompilerParams(dimension_semantics=sem,
                                vmem_limit_bytes=VMEM_LIMIT_BYTES)


def _dot(a, b):
    return jnp.dot(a, b, preferred_element_type=F32)


def _ada_kernel(c_ref, w_ref, b_ref, o_ref):
    c = c_ref[...]
    c_act = c * jax.nn.sigmoid(c)
    c_hi = c_act.astype(BF16)
    c_lo = (c_act - c_hi.astype(F32)).astype(BF16)
    w = w_ref[0]
    w_hi = w.astype(BF16)
    w_lo = (w - w_hi.astype(F32)).astype(BF16)
    acc = _dot(c_hi, w_hi) + (_dot(c_hi, w_lo) + _dot(c_lo, w_hi))
    o_ref[0] = acc + b_ref[0]


def _ada_mod(c, ada_w, ada_b):
    depth, d, n = ada_w.shape
    b = c.shape[0]
    return pl.pallas_call(
        _ada_kernel,
        grid=(depth, n // ADA_TN),
        in_specs=[
            pl.BlockSpec((b, d), lambda l, j: (0, 0)),
            pl.BlockSpec((1, d, ADA_TN), lambda l, j: (l, 0, j)),
            pl.BlockSpec((1, 1, ADA_TN), lambda l, j: (l, 0, j)),
        ],
        out_specs=pl.BlockSpec((1, b, ADA_TN), lambda l, j: (l, 0, j)),
        out_shape=jax.ShapeDtypeStruct((depth, b, n), F32),
        compiler_params=_cparams(("parallel", "parallel")),
        name="ada_mod",
    )(c, ada_w, ada_b.reshape(depth, 1, n))


def _norm_mod(xf, g, sh, sc):
    inv = lax.rsqrt(jnp.mean(xf * xf, axis=-1, keepdims=True) + EPS)
    return xf * inv * (g * (1.0 + sc)) + sh


def _in_proj_kernel(x_ref, sh_ref, sc_ref, g_ref, w_ref, gq_ref, gk_ref, ones_ref,
                    q_ref, k_ref, v_ref, u_ref, *, aw):
    hb = _norm_mod(x_ref[0], g_ref[...], sh_ref[0], sc_ref[0]).astype(BF16)
    n_pair = aw // LANES

    def head_norm(p, g):
        ms = _dot((p * p).astype(BF16), ones_ref[...]) * (1.0 / HEAD_DIM)
        return p * lax.rsqrt(ms + EPS) * g

    q = head_norm(_dot(hb, w_ref[:, 0:aw]), gq_ref[...]).astype(BF16)
    for hp in range(n_pair):
        q_ref[0, hp] = q[:, hp * LANES:(hp + 1) * LANES]
    k = head_norm(_dot(hb, w_ref[:, aw:2 * aw]), gk_ref[...]).astype(BF16)
    for hp in range(n_pair):
        k_ref[0, hp] = k[:, hp * LANES:(hp + 1) * LANES]
    v = _dot(hb, w_ref[:, 2 * aw:3 * aw]).astype(BF16)
    for hp in range(n_pair):
        v_ref[0, hp] = v[:, hp * LANES:(hp + 1) * LANES]
    u_ref[...] = _dot(hb, w_ref[:, 3 * aw:]).astype(BF16)


def _in_proj(x, sh, sc, g, w_in_bf, gq, gk, ones_blk, aw, sw):
    b, l, d = x.shape
    tm = min(PROJ_TM, l)
    n_pair = aw // LANES
    qkv_shape = jax.ShapeDtypeStruct((b, n_pair, l, LANES), BF16)
    qkv_spec = pl.BlockSpec((1, n_pair, tm, LANES), lambda bi, i: (bi, 0, i, 0))
    vec = lambda n: pl.BlockSpec((1, n), lambda bi, i: (0, 0))
    bvec = pl.BlockSpec((1, 1, d), lambda bi, i: (bi, 0, 0))
    return pl.pallas_call(
        functools.partial(_in_proj_kernel, aw=aw),
        grid=(b, l // tm),
        in_specs=[
            pl.BlockSpec((1, tm, d), lambda bi, i: (bi, i, 0)),
            bvec, bvec, vec(d),
            pl.BlockSpec(w_in_bf.shape, lambda bi, i: (0, 0)),
            vec(aw), vec(aw),
            pl.BlockSpec((aw, aw), lambda bi, i: (0, 0)),
        ],
        out_specs=[qkv_spec, qkv_spec, qkv_spec,
                   pl.BlockSpec((tm, sw), lambda bi, i: (i, bi))],
        out_shape=[qkv_shape, qkv_shape, qkv_shape,
                   jax.ShapeDtypeStruct((l, b * sw), BF16)],
        compiler_params=_cparams(("parallel", "parallel")),
        name="in_proj",
    )(x, sh, sc, g, w_in_bf, gq, gk, ones_blk)


def _attn_kernel(q_ref, k_ref, v_ref, o_ref, acc_ref, carry_ref):
    t = ATT_T
    i = pl.program_id(2)
    lane = lax.broadcasted_iota(jnp.int32, (1, LANES), 1)
    first = lane < HEAD_DIM
    q2 = q_ref[0, 0]
    zero = jnp.zeros_like(q2)
    qh = (jnp.where(first, q2, zero), jnp.where(first, zero, q2))
    row = lax.broadcasted_iota(jnp.int32, (t, t), 0)
    col = lax.broadcasted_iota(jnp.int32, (t, t), 1)
    later = jnp.where(row > col, 1.0, 0.0).astype(BF16)
    strict = col < row

    acc_ref[...] = jnp.zeros_like(acc_ref)
    carry_ref[...] = jnp.zeros_like(carry_ref)

    def tile(j, masked):
        start = pl.multiple_of(j * t, t)
        kb = k_ref[0, 0, pl.ds(start, t), :]
        vb = v_ref[0, 0, pl.ds(start, t), :]
        cmin = None
        for h in range(2):
            z = lax.dot_general(qh[h], kb, (((1,), (1,)), ((), ())),
                                preferred_element_type=F32)
            s = jnp.maximum(z, 0.0) + jnp.log(1.0 + jnp.exp(-jnp.abs(z)))
            sm = jnp.where(strict, s, 0.0) if masked else s
            hi = sm.astype(BF16)
            lo = (sm - hi.astype(F32)).astype(BF16)
            within = _dot(hi, later) + _dot(lo, later)
            carry = carry_ref[h]
            w = jnp.exp(z - s - within - carry)
            if masked:
                w = jnp.where(strict, w, 0.0)
            acc_ref[h] += _dot(w.astype(BF16), vb)
            carry = carry + jnp.sum(sm, axis=-1, keepdims=True)
            carry_ref[h] = carry
            m = jnp.min(carry)
            cmin = m if cmin is None else jnp.minimum(cmin, m)
        return cmin

    cmin0 = tile(i, True)

    def cond(state):
        j, cmin = state
        return jnp.logical_and(j >= 0, cmin < UNDERFLOW_BOUND)

    def body(state):
        j, _ = state
        return j - 1, tile(j, False)

    lax.while_loop(cond, body, (i - 1, cmin0))
    o_ref[0] = jnp.where(first, acc_ref[0], acc_ref[1]).astype(o_ref.dtype)


def _attn(q, k, v):
    b, n_pair, l, _ = q.shape
    t = ATT_T
    kv_spec = pl.BlockSpec((1, 1, l, LANES), lambda bi, hp, i: (bi, hp, 0, 0))
    return pl.pallas_call(
        _attn_kernel,
        grid=(b, n_pair, l // t),
        in_specs=[pl.BlockSpec((1, 1, t, LANES), lambda bi, hp, i: (bi, hp, i, 0)),
                  kv_spec, kv_spec],
        out_specs=pl.BlockSpec((1, t, LANES), lambda bi, hp, i: (bi, i, hp)),
        out_shape=jax.ShapeDtypeStruct((b, l, n_pair * LANES), BF16),
        scratch_shapes=[pltpu.VMEM((2, t, LANES), F32),
                        pltpu.VMEM((2, t, 1), F32)],
        compiler_params=_cparams(("parallel", "parallel", "parallel")),
        name="attn",
    )(q, k, v)


def _ssm_kernel(u_ref, wb_ref, are_ref, aim_ref, wcre_ref, wcim_ref, d_ref,
                wglu_ref, bglu_ref, g_ref, o_ref, sre_ref, sim_ref, st_ref, y_ref,
                *, n_oct, batch):
    rows = u_ref.shape[0]
    n_tok = rows // batch
    oct_states = sre_ref.shape[1] // n_oct
    half = sre_ref.shape[1] // 2

    @pl.when(pl.program_id(0) == 0)
    def _():
        st_ref[...] = jnp.zeros_like(st_ref)

    u = u_ref[...]
    for o in range(n_oct):
        bu = _dot(u[:, o * LANES:(o + 1) * LANES], wb_ref[o])
        sre_ref[:, o * oct_states:(o + 1) * oct_states] = bu[:, :oct_states]
        sim_ref[:, o * oct_states:(o + 1) * oct_states] = bu[:, oct_states:]

    for part in range(2):
        cols = slice(part * half, (part + 1) * half)
        are = jnp.broadcast_to(are_ref[:, cols], (batch, half))
        aim = jnp.broadcast_to(aim_ref[:, cols], (batch, half))

        def step(t, state, cols=cols, are=are, aim=aim):
            s_re, s_im = state
            r = pl.ds(pl.multiple_of(t * batch, batch), batch)
            n_re = are * s_re - aim * s_im + sre_ref[r, cols]
            n_im = are * s_im + aim * s_re + sim_ref[r, cols]
            sre_ref[r, cols] = n_re
            sim_ref[r, cols] = n_im
            return n_re, n_im

        s_re, s_im = lax.fori_loop(0, n_tok, step,
                                   (st_ref[0, :, cols], st_ref[1, :, cols]),
                                   unroll=4)
        st_ref[0, :, cols] = s_re
        st_ref[1, :, cols] = s_im

    for o in range(n_oct):
        cs = slice(o * oct_states, (o + 1) * oct_states)
        y_ref[:, o * LANES:(o + 1) * LANES] = (
            _dot(sre_ref[:, cs].astype(BF16), wcre_ref[o])
            + _dot(sim_ref[:, cs].astype(BF16), wcim_ref[o]))
    y = y_ref[...] + d_ref[...] * u.astype(F32)
    y = jax.nn.gelu(y)
    gate = jax.nn.sigmoid(_dot(y.astype(BF16), wglu_ref[...]) + bglu_ref[...])
    out = y * gate
    inv = lax.rsqrt(jnp.mean(out * out, axis=-1, keepdims=True) + EPS)
    o_ref[...] = (out * inv * g_ref[...]).astype(o_ref.dtype)


def _ssm(u2, prm, batch):
    rows_total, sw = u2.shape
    n_oct = sw // LANES
    n_state = prm["are"].shape[1]
    rows = SSM_T * batch
    full = lambda a: pl.BlockSpec(a.shape, lambda i: (0,) * a.ndim)
    names = ["wb", "are", "aim", "wcre", "wcim", "d", "wglu", "bglu", "g"]
    return pl.pallas_call(
        functools.partial(_ssm_kernel, n_oct=n_oct, batch=batch),
        grid=(rows_total // rows,),
        in_specs=[pl.BlockSpec((rows, sw), lambda i: (i, 0))]
                 + [full(prm[n]) for n in names],
        out_specs=pl.BlockSpec((rows, sw), lambda i: (i, 0)),
        out_shape=jax.ShapeDtypeStruct((rows_total, sw), BF16),
        scratch_shapes=[pltpu.VMEM((rows, n_state), F32),
                        pltpu.VMEM((rows, n_state), F32),
                        pltpu.VMEM((2, batch, n_state), F32),
                        pltpu.VMEM((rows, sw), F32)],
        compiler_params=_cparams(("arbitrary",)),
        name="ssm",
    )(u2, *[prm[n] for n in names])


def _ssm_params(a_re, a_im, log_dt, b_re, b_im, c_re, c_im, d_skip, glu_w, glu_b, out_g):
    g, p, h = b_re.shape
    per_oct = LANES // h
    n_oct = g // per_oct
    dt = jnp.exp(log_dt)[:, None]
    mag = jnp.exp(dt * a_re)
    abar_re = mag * jnp.cos(dt * a_im)
    abar_im = mag * jnp.sin(dt * a_im)
    em_re = abar_re - 1.0
    em_im = abar_im
    den = a_re * a_re + a_im * a_im
    f_re = (em_re * a_re + em_im * a_im) / den
    f_im = (em_im * a_re - em_re * a_im) / den
    bb_re = f_re[..., None] * b_re - f_im[..., None] * b_im
    bb_im = f_re[..., None] * b_im + f_im[..., None] * b_re
    eye = jnp.eye(per_oct, dtype=F32)

    def in_blockdiag(bb):
        bb = bb.reshape(n_oct, per_oct, p, h)
        m = jnp.einsum("ogph,gk->oghkp", bb, eye)
        return m.reshape(n_oct, per_oct * h, per_oct * p)

    def out_blockdiag(c):
        c = c.reshape(n_oct, per_oct, h, p)
        m = jnp.einsum("oghp,gk->ogpkh", c, eye)
        return m.reshape(n_oct, per_oct * p, per_oct * h)

    wb = jnp.concatenate([in_blockdiag(bb_re), in_blockdiag(bb_im)], axis=-1).astype(BF16)
    eye_g = jnp.eye(g, dtype=F32)
    wglu = jnp.einsum("ghk,gf->ghfk", glu_w, eye_g).reshape(g * h, g * h).astype(BF16)
    return {
        "wb": wb,
        "are": abar_re.reshape(1, g * p), "aim": abar_im.reshape(1, g * p),
        "wcre": out_blockdiag(c_re).astype(BF16),
        "wcim": out_blockdiag(-c_im).astype(BF16),
        "d": d_skip.reshape(1, g * h), "wglu": wglu,
        "bglu": glu_b.reshape(1, g * h), "g": out_g.reshape(1, g * h),
    }


def _out_proj_kernel(x_ref, oa_ref, os_ref, ga_ref, wa_ref, ws_ref, g1_ref, o_ref):
    oa = oa_ref[0].astype(F32)
    inv = lax.rsqrt(jnp.mean(oa * oa, axis=-1, keepdims=True) + EPS)
    on = (oa * inv * ga_ref[...]).astype(BF16)
    y = _dot(on, wa_ref[...]) + _dot(os_ref[...], ws_ref[...])
    o_ref[0] = x_ref[0] + g1_ref[0] * y


def _out_proj(x, o_attn, o_ssm2, ga, wa, ws, g1):
    b, l, d = x.shape
    aw = o_attn.shape[-1]
    sw = o_ssm2.shape[-1] // b
    tm = min(PROJ_TM, l)
    return pl.pallas_call(
        _out_proj_kernel,
        grid=(b, l // tm),
        in_specs=[
            pl.BlockSpec((1, tm, d), lambda bi, i: (bi, i, 0)),
            pl.BlockSpec((1, tm, aw), lambda bi, i: (bi, i, 0)),
            pl.BlockSpec((tm, sw), lambda bi, i: (i, bi)),
            pl.BlockSpec((1, aw), lambda bi, i: (0, 0)),
            pl.BlockSpec(wa.shape, lambda bi, i: (0, 0)),
            pl.BlockSpec(ws.shape, lambda bi, i: (0, 0)),
            pl.BlockSpec((1, 1, d), lambda bi, i: (bi, 0, 0)),
        ],
        out_specs=pl.BlockSpec((1, tm, d), lambda bi, i: (bi, i, 0)),
        out_shape=jax.ShapeDtypeStruct(x.shape, F32),
        compiler_params=_cparams(("parallel", "parallel")),
        name="out_proj",
    )(x, o_attn, o_ssm2, ga, wa, ws, g1)


def _ffn_kernel(x_ref, xh_ref, sh_ref, sc_ref, g2_ref, g_ref, wup_ref, cw_ref, cb_ref,
                wdn_ref, o_ref, acc_ref, *, d_ff):
    tm = x_ref.shape[1]
    g, sh, sc = g_ref[...], sh_ref[0], sc_ref[0]
    h_main = _norm_mod(x_ref[0], g, sh, sc).astype(BF16)
    keep = (pl.program_id(1) > 0).astype(F32)
    h_halo = (_norm_mod(xh_ref[0], g, sh, sc) * keep).astype(BF16)
    h_ext = jnp.concatenate([h_halo, h_main], axis=0)

    def conv(up, cols):
        out = cb_ref[:, cols]
        for i in range(CONV_WIDTH):
            off = FFN_HALO - (CONV_WIDTH - 1) + i
            out = out + up[off:off + tm] * cw_ref[i:i + 1, cols]
        return out

    for c in range(d_ff // FFN_CK):
        vcols = slice(c * FFN_CK, (c + 1) * FFN_CK)
        gcols = slice(d_ff + c * FFN_CK, d_ff + (c + 1) * FFN_CK)
        val = conv(_dot(h_ext, wup_ref[:, vcols]), vcols)
        gate = conv(_dot(h_ext, wup_ref[:, gcols]), gcols)
        a = (jax.nn.gelu(gate) * val).astype(BF16)
        part = _dot(a, wdn_ref[vcols, :])
        if c == 0:
            acc_ref[...] = part
        else:
            acc_ref[...] += part
    o_ref[0] = x_ref[0] + g2_ref[0] * acc_ref[...]


def _ffn(x, sh, sc, g2, g, wup, cw, cb, wdn):
    b, l, d = x.shape
    d_ff = wdn.shape[0]
    tm = min(FFN_TM, l)
    halo_per_tile = tm // FFN_HALO
    bvec = pl.BlockSpec((1, 1, d), lambda bi, i: (bi, 0, 0))
    const = lambda a: pl.BlockSpec(a.shape, lambda bi, i: (0,) * a.ndim)
    return pl.pallas_call(
        functools.partial(_ffn_kernel, d_ff=d_ff),
        grid=(b, l // tm),
        in_specs=[
            pl.BlockSpec((1, tm, d), lambda bi, i: (bi, i, 0)),
            pl.BlockSpec((1, FFN_HALO, d),
                         lambda bi, i: (bi, jnp.maximum(i * halo_per_tile - 1, 0), 0)),
            bvec, bvec, bvec, const(g), const(wup), const(cw), const(cb), const(wdn),
        ],
        out_specs=pl.BlockSpec((1, tm, d), lambda bi, i: (bi, i, 0)),
        out_shape=jax.ShapeDtypeStruct(x.shape, F32),
        scratch_shapes=[pltpu.VMEM((tm, d), F32)],
        compiler_params=_cparams(("parallel", "parallel")),
        name="ffn",
    )(x, x, sh, sc, g2, g, wup, cw, cb, wdn)


def kernel(x, c, ada_w, ada_b, norm1_g, w_in, q_norm_g, k_norm_g, ssm_a_re, ssm_a_im,
           ssm_log_dt, ssm_b_re, ssm_b_im, ssm_c_re, ssm_c_im, ssm_d, glu_w, glu_b,
           attn_out_g, ssm_out_g, w_out, norm2_g, ffn_w_up, ffn_conv_w, ffn_conv_b,
           ffn_w_down):
    b, l, d = x.shape
    depth = ada_w.shape[0]
    aw = attn_out_g.shape[-1]
    sw = ssm_out_g.shape[-1]
    n_heads = aw // HEAD_DIM
    assert b == SUBLANES and aw % LANES == 0 and sw % LANES == 0
    assert l % ATT_T == 0 and l % SSM_T == 0

    mod = _ada_mod(c, ada_w, ada_b)
    head = jnp.arange(aw) // HEAD_DIM
    ones_blk = (head[:, None] == head[None, :]).astype(BF16)
    inv_sqrt = 1.0 / math.sqrt(HEAD_DIM)

    for li in range(depth):
        m = mod[li].reshape(b, N_MOD, 1, d)
        sh1, sc1, g1, sh2, sc2, g2 = (m[:, i] for i in range(N_MOD))
        gq = jnp.tile(q_norm_g[li] * inv_sqrt, n_heads).reshape(1, aw)
        gk = jnp.tile(k_norm_g[li], n_heads).reshape(1, aw)
        q, k, v, u2 = _in_proj(x, sh1, sc1, norm1_g[li].reshape(1, d),
                               w_in[li].astype(BF16), gq, gk, ones_blk, aw, sw)
        o_attn = _attn(q, k, v)
        prm = _ssm_params(ssm_a_re[li], ssm_a_im[li], ssm_log_dt[li], ssm_b_re[li],
                          ssm_b_im[li], ssm_c_re[li], ssm_c_im[li], ssm_d[li],
                          glu_w[li], glu_b[li], ssm_out_g[li])
        o_ssm = _ssm(u2.reshape(l * b, sw), prm, b)
        w_o = w_out[li].astype(BF16)
        x = _out_proj(x, o_attn, o_ssm.reshape(l, b * sw), attn_out_g[li].reshape(1, aw),
                      w_o[:aw], w_o[aw:], g1)
        x = _ffn(x, sh2, sc2, g2, norm2_g[li].reshape(1, d), ffn_w_up[li].astype(BF16),
                 ffn_conv_w[li], ffn_conv_b[li].reshape(1, -1), ffn_w_down[li].astype(BF16))
    return x
```

```python
import functools
import math

import jax
import jax.numpy as jnp
from jax import lax
from jax.experimental import pallas as pl
from jax.experimental.pallas import tpu as pltpu

F32 = jnp.float32
BF16 = jnp.bfloat16

EPS = 1e-6
HEAD_DIM = 64
SSM_GROUP = 16
SSM_STATE = 64
CONV_WIDTH = 3
N_MOD = 6

LANES = 128
SUBLANES = 8
VMEM_LIMIT_BYTES = 56 * 1024 * 1024

ADA_TN = 1536
PROJ_TM = 512
ATT_T = 128
ATT_QS = 512
SSM_T = 64
FFN_TM = 512
FFN_CK = 256
FFN_HALO = 16

UNDERFLOW_BOUND = 152.0
LOG2_E = 1.4426950408889634


def _cparams(sem):
    return pltpu.CompilerParams(dimension_semantics=sem,
                                vmem_limit_bytes=VMEM_LIMIT_BYTES)


def _dot(a, b):
    return jnp.dot(a, b, preferred_element_type=F32)


def _ada_kernel(c_ref, w_ref, b_ref, o_ref):
    c = c_ref[...]
    c_act = c * jax.nn.sigmoid(c)
    c_hi = c_act.astype(BF16)
    c_lo = (c_act - c_hi.astype(F32)).astype(BF16)
    w = w_ref[0]
    w_hi = w.astype(BF16)
    w_lo = (w - w_hi.astype(F32)).astype(BF16)
    acc = _dot(c_hi, w_hi) + (_dot(c_hi, w_lo) + _dot(c_lo, w_hi))
    o_ref[0] = acc + b_ref[0]


def _ada_mod(c, ada_w, ada_b):
    depth, d, n = ada_w.shape
    b = c.shape[0]
    return pl.pallas_call(
        _ada_kernel,
        grid=(depth, n // ADA_TN),
        in_specs=[
            pl.BlockSpec((b, d), lambda l, j: (0, 0)),
            pl.BlockSpec((1, d, ADA_TN), lambda l, j: (l, 0, j)),
            pl.BlockSpec((1, 1, ADA_TN), lambda l, j: (l, 0, j)),
        ],
        out_specs=pl.BlockSpec((1, b, ADA_TN), lambda l, j: (l, 0, j)),
        out_shape=jax.ShapeDtypeStruct((depth, b, n), F32),
        compiler_params=_cparams(("parallel", "parallel")),
        name="ada_mod",
    )(c, ada_w, ada_b.reshape(depth, 1, n))


def _norm_mod(xf, g, sh, sc):
    inv = lax.rsqrt(jnp.mean(xf * xf, axis=-1, keepdims=True) + EPS)
    return xf * inv * (g * (1.0 + sc)) + sh


def _in_proj_kernel(x_ref, sh_ref, sc_ref, g_ref, w_ref, gq_ref, gk_ref, ones_ref,
                    q_ref, k_ref, v_ref, u_ref, *, aw):
    hb = _norm_mod(x_ref[0], g_ref[...], sh_ref[0], sc_ref[0]).astype(BF16)
    n_pair = aw // LANES

    def head_norm(p, g):
        ms = _dot((p * p).astype(BF16), ones_ref[...]) * (1.0 / HEAD_DIM)
        return p * lax.rsqrt(ms + EPS) * g

    q = head_norm(_dot(hb, w_ref[:, 0:aw]), gq_ref[...]).astype(BF16)
    for hp in range(n_pair):
        q_ref[0, hp] = q[:, hp * LANES:(hp + 1) * LANES]
    k = head_norm(_dot(hb, w_ref[:, aw:2 * aw]), gk_ref[...]).astype(BF16)
    for hp in range(n_pair):
        k_ref[0, hp] = k[:, hp * LANES:(hp + 1) * LANES]
    v = _dot(hb, w_ref[:, 2 * aw:3 * aw]).astype(BF16)
    for hp in range(n_pair):
        v_ref[0, hp] = v[:, hp * LANES:(hp + 1) * LANES]
    u_ref[...] = _dot(hb, w_ref[:, 3 * aw:]).astype(BF16)


def _in_proj(x, sh, sc, g, w_in_bf, gq, gk, ones_blk, aw, sw):
    b, l, d = x.shape
    tm = min(PROJ_TM, l)
    n_pair = aw // LANES
    qkv_shape = jax.ShapeDtypeStruct((b, n_pair, l, LANES), BF16)
    qkv_spec = pl.BlockSpec((1, n_pair, tm, LANES), lambda bi, i: (bi, 0, i, 0))
    vec = lambda n: pl.BlockSpec((1, n), lambda bi, i: (0, 0))
    bvec = pl.BlockSpec((1, 1, d), lambda bi, i: (bi, 0, 0))
    return pl.pallas_call(
        functools.partial(_in_proj_kernel, aw=aw),
        grid=(b, l // tm),
        in_specs=[
            pl.BlockSpec((1, tm, d), lambda bi, i: (bi, i, 0)),
            bvec, bvec, vec(d),
            pl.BlockSpec(w_in_bf.shape, lambda bi, i: (0, 0)),
            vec(aw), vec(aw),
            pl.BlockSpec((aw, aw), lambda bi, i: (0, 0)),
        ],
        out_specs=[qkv_spec, qkv_spec, qkv_spec,
                   pl.BlockSpec((tm, sw), lambda bi, i: (i, bi))],
        out_shape=[qkv_shape, qkv_shape, qkv_shape,
                   jax.ShapeDtypeStruct((l, b * sw), BF16)],
        compiler_params=_cparams(("parallel", "parallel")),
        name="in_proj",
    )(x, sh, sc, g, w_in_bf, gq, gk, ones_blk)


def _attn_kernel(q_ref, k_ref, v_ref, o_ref, acc_ref, carry_ref, next_ref, bound_ref):
    t = ATT_T
    n_sub = q_ref.shape[2] // t
    base = pl.program_id(2) * n_sub
    lane = lax.broadcasted_iota(jnp.int32, (1, LANES), 1)
    first = lane < HEAD_DIM
    row = lax.broadcasted_iota(jnp.int32, (t, t), 0)
    col = lax.broadcasted_iota(jnp.int32, (t, t), 1)
    later = jnp.where(row > col, 1.0, 0.0).astype(BF16)
    later2 = jnp.concatenate([later, later], axis=0)
    strict = col < row

    def split_heads(ref, j):
        blk = ref[0, 0, pl.ds(pl.multiple_of(j * t, t), t), :]
        zero = jnp.zeros_like(blk)
        return jnp.concatenate([jnp.where(first, blk, zero), jnp.where(first, zero, blk)], axis=0)

    def block(q2, j, masked, carry):
        zz = lax.dot_general(q2, split_heads(k_ref, j), (((1,), (1,)), ((), ())),
                             preferred_element_type=F32)
        ws, sums = [], []
        for h in range(2):
            z = zz[:, h * t:(h + 1) * t]
            s = jnp.maximum(z, 0.0) + jnp.log2(1.0 + jnp.exp2(-jnp.abs(z)))
            sm = jnp.where(strict, s, 0.0) if masked else s
            hi = sm.astype(BF16)
            lo = (sm - hi.astype(F32)).astype(BF16)
            arg = z - s - _dot(jnp.concatenate([hi, lo], axis=1), later2)
            if carry is not None:
                arg = arg - carry[h]
            w = jnp.exp2(arg)
            if masked:
                w = jnp.where(strict, w, 0.0)
            ws.append(w.astype(BF16))
            sums.append(jnp.sum(sm, axis=-1, keepdims=True))
        pv = _dot(jnp.concatenate(ws, axis=1), split_heads(v_ref, j))
        return pv, sums

    def save(k, pv, carry, next_j):
        acc_ref[k] = pv
        for h in range(2):
            carry_ref[k, h] = carry[h]
        next_ref[k] = next_j
        bound_ref[k] = jnp.min(jnp.minimum(carry[0], carry[1]))

    def window(k):
        i = base + k
        q2 = q_ref[0, 0, k * t:(k + 1) * t, :]
        pv0, r0 = block(q2, i, True, None)
        pv1, r1 = block(q2, i - 1, False, r0)
        c2 = [r0[h] + r1[h] for h in range(2)]
        pv2, r2 = block(q2, i - 2, False, c2)
        save(k, pv0 + pv1 + pv2, [c2[h] + r2[h] for h in range(2)], i - 3)

    def diagonal_only(k):
        i = base + k
        pv0, r0 = block(q_ref[0, 0, k * t:(k + 1) * t, :], i, True, None)
        save(k, pv0, r0, i - 1)

    @pl.when(base >= 2)
    def _():
        for k in range(n_sub):
            window(k)

    @pl.when(base < 2)
    def _():
        for k in range(n_sub):
            if k >= 2:
                window(k)
            else:
                diagonal_only(k)

    def finish(k, _):
        rows = pl.ds(pl.multiple_of(k * t, t), t)
        q2 = q_ref[0, 0, rows, :]

        def cond(state):
            j, bound = state
            return jnp.logical_and(j >= 0, bound < UNDERFLOW_BOUND)

        def body(state):
            j, _ = state
            carry = [carry_ref[k, 0], carry_ref[k, 1]]
            pv, r = block(q2, j, False, carry)
            acc_ref[k] += pv
            carry = [carry[h] + r[h] for h in range(2)]
            for h in range(2):
                carry_ref[k, h] = carry[h]
            return j - 1, jnp.min(jnp.minimum(carry[0], carry[1]))

        lax.while_loop(cond, body, (next_ref[k], bound_ref[k]))
        o_ref[0, rows, :] = acc_ref[k].astype(o_ref.dtype)
        return 0

    lax.fori_loop(0, n_sub, finish, 0)


def _attn(q, k, v):
    b, n_pair, l, _ = q.shape
    t = ATT_T
    qs = min(ATT_QS, l)
    kv_spec = pl.BlockSpec((1, 1, l, LANES), lambda bi, hp, i: (bi, hp, 0, 0))
    return pl.pallas_call(
        _attn_kernel,
        grid=(b, n_pair, l // qs),
        in_specs=[pl.BlockSpec((1, 1, qs, LANES), lambda bi, hp, i: (bi, hp, i, 0)),
                  kv_spec, kv_spec],
        out_specs=pl.BlockSpec((1, qs, LANES), lambda bi, hp, i: (bi, i, hp)),
        out_shape=jax.ShapeDtypeStruct((b, l, n_pair * LANES), BF16),
        scratch_shapes=[pltpu.VMEM((qs // t, t, LANES), F32),
                        pltpu.VMEM((qs // t, 2, t, 1), F32),
                        pltpu.SMEM((qs // t,), jnp.int32),
                        pltpu.SMEM((qs // t,), F32)],
        compiler_params=_cparams(("parallel", "parallel", "parallel")),
        name="attn",
    )(q, k, v)


def _ssm_kernel(u_ref, wb_ref, are_ref, aim_ref, wcre_ref, wcim_ref, d_ref,
                wglu_ref, bglu_ref, g_ref, o_ref, sre_ref, sim_ref, st_ref, y_ref,
                *, n_oct, batch):
    n_tok = u_ref.shape[0]
    sw = u_ref.shape[1] // batch
    oct_states = sre_ref.shape[1] // n_oct
    half = sre_ref.shape[1] // 2

    @pl.when(pl.program_id(0) == 0)
    def _():
        st_ref[...] = jnp.zeros_like(st_ref)

    for bi in range(batch):
        for o in range(n_oct):
            c0 = bi * sw + o * LANES
            y_ref[o, pl.ds(bi, n_tok, stride=batch), :] = u_ref[:, c0:c0 + LANES].astype(F32)
    for o in range(n_oct):
        bu = _dot(y_ref[o].astype(BF16), wb_ref[o])
        sre_ref[:, o * oct_states:(o + 1) * oct_states] = bu[:, :oct_states]
        sim_ref[:, o * oct_states:(o + 1) * oct_states] = bu[:, oct_states:]

    for part in range(2):
        cols = slice(part * half, (part + 1) * half)
        are = jnp.broadcast_to(are_ref[:, cols], (batch, half))
        aim = jnp.broadcast_to(aim_ref[:, cols], (batch, half))

        def step(t, state, cols=cols, are=are, aim=aim):
            s_re, s_im = state
            r = pl.ds(pl.multiple_of(t * batch, batch), batch)
            n_re = are * s_re - aim * s_im + sre_ref[r, cols]
            n_im = are * s_im + aim * s_re + sim_ref[r, cols]
            sre_ref[r, cols] = n_re
            sim_ref[r, cols] = n_im
            return n_re, n_im

        s_re, s_im = lax.fori_loop(0, n_tok, step,
                                   (st_ref[0, :, cols], st_ref[1, :, cols]),
                                   unroll=4)
        st_ref[0, :, cols] = s_re
        st_ref[1, :, cols] = s_im

    sumsq = None
    for o in range(n_oct):
        cs = slice(o * oct_states, (o + 1) * oct_states)
        lanes = slice(o * LANES, (o + 1) * LANES)
        y = (_dot(sre_ref[:, cs].astype(BF16), wcre_ref[o])
             + _dot(sim_ref[:, cs].astype(BF16), wcim_ref[o])
             + d_ref[:, lanes] * y_ref[o])
        y = jax.nn.gelu(y)
        gate = jax.nn.sigmoid(_dot(y.astype(BF16), wglu_ref[o]) + bglu_ref[:, lanes])
        out = y * gate
        y_ref[o] = out
        sq = jnp.sum(out * out, axis=-1, keepdims=True)
        sumsq = sq if sumsq is None else sumsq + sq
    inv = lax.rsqrt(sumsq * (1.0 / sw) + EPS)
    for o in range(n_oct):
        y_ref[o] = y_ref[o] * inv * g_ref[:, o * LANES:(o + 1) * LANES]
    for bi in range(batch):
        for o in range(n_oct):
            c0 = bi * sw + o * LANES
            o_ref[:, c0:c0 + LANES] = (
                y_ref[o, pl.ds(bi, n_tok, stride=batch), :].astype(o_ref.dtype))


def _ssm(u2, prm, batch):
    l, width = u2.shape
    sw = width // batch
    n_oct = sw // LANES
    n_state = prm["are"].shape[1]
    rows = SSM_T * batch
    full = lambda a: pl.BlockSpec(a.shape, lambda i: (0,) * a.ndim)
    names = ["wb", "are", "aim", "wcre", "wcim", "d", "wglu", "bglu", "g"]
    return pl.pallas_call(
        functools.partial(_ssm_kernel, n_oct=n_oct, batch=batch),
        grid=(l // SSM_T,),
        in_specs=[pl.BlockSpec((SSM_T, width), lambda i: (i, 0))]
                 + [full(prm[n]) for n in names],
        out_specs=pl.BlockSpec((SSM_T, width), lambda i: (i, 0)),
        out_shape=jax.ShapeDtypeStruct((l, width), BF16),
        scratch_shapes=[pltpu.VMEM((rows, n_state), F32),
                        pltpu.VMEM((rows, n_state), F32),
                        pltpu.VMEM((2, batch, n_state), F32),
                        pltpu.VMEM((n_oct, rows, LANES), F32)],
        compiler_params=_cparams(("arbitrary",)),
        name="ssm",
    )(u2, *[prm[n] for n in names])


def _ssm_params(a_re, a_im, log_dt, b_re, b_im, c_re, c_im, d_skip, glu_w, glu_b, out_g):
    g, p, h = b_re.shape
    per_oct = LANES // h
    n_oct = g // per_oct
    dt = jnp.exp(log_dt)[:, None]
    mag = jnp.exp(dt * a_re)
    abar_re = mag * jnp.cos(dt * a_im)
    abar_im = mag * jnp.sin(dt * a_im)
    em_re = abar_re - 1.0
    em_im = abar_im
    den = a_re * a_re + a_im * a_im
    f_re = (em_re * a_re + em_im * a_im) / den
    f_im = (em_im * a_re - em_re * a_im) / den
    bb_re = f_re[..., None] * b_re - f_im[..., None] * b_im
    bb_im = f_re[..., None] * b_im + f_im[..., None] * b_re
    eye = jnp.eye(per_oct, dtype=F32)

    def in_blockdiag(bb):
        bb = bb.reshape(n_oct, per_oct, p, h)
        m = jnp.einsum("ogph,gk->oghkp", bb, eye)
        return m.reshape(n_oct, per_oct * h, per_oct * p)

    def out_blockdiag(c):
        c = c.reshape(n_oct, per_oct, h, p)
        m = jnp.einsum("oghp,gk->ogpkh", c, eye)
        return m.reshape(n_oct, per_oct * p, per_oct * h)

    wb = jnp.concatenate([in_blockdiag(bb_re), in_blockdiag(bb_im)], axis=-1).astype(BF16)
    wglu = jnp.einsum("oghk,gf->oghfk", glu_w.reshape(n_oct, per_oct, h, h), eye)
    wglu = wglu.reshape(n_oct, per_oct * h, per_oct * h).astype(BF16)
    return {
        "wb": wb,
        "are": abar_re.reshape(1, g * p), "aim": abar_im.reshape(1, g * p),
        "wcre": out_blockdiag(c_re).astype(BF16),
        "wcim": out_blockdiag(-c_im).astype(BF16),
        "d": d_skip.reshape(1, g * h), "wglu": wglu,
        "bglu": glu_b.reshape(1, g * h), "g": out_g.reshape(1, g * h),
    }


def _out_proj_kernel(x_ref, oa_ref, os_ref, ga_ref, wa_ref, ws_ref, g1_ref, o_ref):
    oa = oa_ref[0].astype(F32)
    inv = lax.rsqrt(jnp.mean(oa * oa, axis=-1, keepdims=True) + EPS)
    on = (oa * inv * ga_ref[...]).astype(BF16)
    y = _dot(on, wa_ref[...]) + _dot(os_ref[...], ws_ref[...])
    o_ref[0] = x_ref[0] + g1_ref[0] * y


def _out_proj(x, o_attn, o_ssm2, ga, wa, ws, g1):
    b, l, d = x.shape
    aw = o_attn.shape[-1]
    sw = o_ssm2.shape[-1] // b
    tm = min(PROJ_TM, l)
    return pl.pallas_call(
        _out_proj_kernel,
        grid=(b, l // tm),
        in_specs=[
            pl.BlockSpec((1, tm, d), lambda bi, i: (bi, i, 0)),
            pl.BlockSpec((1, tm, aw), lambda bi, i: (bi, i, 0)),
            pl.BlockSpec((tm, sw), lambda bi, i: (i, bi)),
            pl.BlockSpec((1, aw), lambda bi, i: (0, 0)),
            pl.BlockSpec(wa.shape, lambda bi, i: (0, 0)),
            pl.BlockSpec(ws.shape, lambda bi, i: (0, 0)),
            pl.BlockSpec((1, 1, d), lambda bi, i: (bi, 0, 0)),
        ],
        out_specs=pl.BlockSpec((1, tm, d), lambda bi, i: (bi, i, 0)),
        out_shape=jax.ShapeDtypeStruct(x.shape, F32),
        compiler_params=_cparams(("parallel", "parallel")),
        name="out_proj",
    )(x, o_attn, o_ssm2, ga, wa, ws, g1)


def _ffn_kernel(x_ref, xh_ref, sh_ref, sc_ref, g2_ref, g_ref, wup_ref, cw_ref, cb_ref,
                wdn_ref, o_ref, acc_ref, *, d_ff):
    tm = x_ref.shape[1]
    g, sh, sc = g_ref[...], sh_ref[0], sc_ref[0]
    h_main = _norm_mod(x_ref[0], g, sh, sc).astype(BF16)
    keep = (pl.program_id(1) > 0).astype(F32)
    h_halo = (_norm_mod(xh_ref[0], g, sh, sc) * keep).astype(BF16)
    h_ext = jnp.concatenate([h_halo, h_main], axis=0)

    def conv(up, cols):
        out = cb_ref[:, cols]
        for i in range(CONV_WIDTH):
            off = FFN_HALO - (CONV_WIDTH - 1) + i
            out = out + up[off:off + tm] * cw_ref[i:i + 1, cols]
        return out

    for c in range(d_ff // FFN_CK):
        vcols = slice(c * FFN_CK, (c + 1) * FFN_CK)
        gcols = slice(d_ff + c * FFN_CK, d_ff + (c + 1) * FFN_CK)
        val = conv(_dot(h_ext, wup_ref[:, vcols]), vcols)
        gate = conv(_dot(h_ext, wup_ref[:, gcols]), gcols)
        a = (jax.nn.gelu(gate) * val).astype(BF16)
        part = _dot(a, wdn_ref[vcols, :])
        if c == 0:
            acc_ref[...] = part
        else:
            acc_ref[...] += part
    o_ref[0] = x_ref[0] + g2_ref[0] * acc_ref[...]


def _ffn(x, sh, sc, g2, g, wup, cw, cb, wdn):
    b, l, d = x.shape
    d_ff = wdn.shape[0]
    tm = min(FFN_TM, l)
    halo_per_tile = tm // FFN_HALO
    bvec = pl.BlockSpec((1, 1, d), lambda bi, i: (bi, 0, 0))
    const = lambda a: pl.BlockSpec(a.shape, lambda bi, i: (0,) * a.ndim)
    return pl.pallas_call(
        functools.partial(_ffn_kernel, d_ff=d_ff),
        grid=(b, l // tm),
        in_specs=[
            pl.BlockSpec((1, tm, d), lambda bi, i: (bi, i, 0)),
            pl.BlockSpec((1, FFN_HALO, d),
                         lambda bi, i: (bi, jnp.maximum(i * halo_per_tile - 1, 0), 0)),
            bvec, bvec, bvec, const(g), const(wup), const(cw), const(cb), const(wdn),
        ],
        out_specs=pl.BlockSpec((1, tm, d), lambda bi, i: (bi, i, 0)),
        out_shape=jax.ShapeDtypeStruct(x.shape, F32),
        scratch_shapes=[pltpu.VMEM((tm, d), F32)],
        compiler_params=_cparams(("parallel", "parallel")),
        name="ffn",
    )(x, x, sh, sc, g2, g, wup, cw, cb, wdn)


def kernel(x, c, ada_w, ada_b, norm1_g, w_in, q_norm_g, k_norm_g, ssm_a_re, ssm_a_im,
           ssm_log_dt, ssm_b_re, ssm_b_im, ssm_c_re, ssm_c_im, ssm_d, glu_w, glu_b,
           attn_out_g, ssm_out_g, w_out, norm2_g, ffn_w_up, ffn_conv_w, ffn_conv_b,
           ffn_w_down):
    b, l, d = x.shape
    depth = ada_w.shape[0]
    aw = attn_out_g.shape[-1]
    sw = ssm_out_g.shape[-1]
    n_heads = aw // HEAD_DIM
    assert b == SUBLANES and aw % LANES == 0 and sw % LANES == 0
    assert l % ATT_T == 0 and l % SSM_T == 0

    mod = _ada_mod(c, ada_w, ada_b)
    head = jnp.arange(aw) // HEAD_DIM
    ones_blk = (head[:, None] == head[None, :]).astype(BF16)
    inv_sqrt = LOG2_E / math.sqrt(HEAD_DIM)

    for li in range(depth):
        m = mod[li].reshape(b, N_MOD, 1, d)
        sh1, sc1, g1, sh2, sc2, g2 = (m[:, i] for i in range(N_MOD))
        gq = jnp.tile(q_norm_g[li] * inv_sqrt, n_heads).reshape(1, aw)
        gk = jnp.tile(k_norm_g[li], n_heads).reshape(1, aw)
        q, k, v, u2 = _in_proj(x, sh1, sc1, norm1_g[li].reshape(1, d),
                               w_in[li].astype(BF16), gq, gk, ones_blk, aw, sw)
        o_attn = _attn(q, k, v)
        prm = _ssm_params(ssm_a_re[li], ssm_a_im[li], ssm_log_dt[li], ssm_b_re[li],
                          ssm_b_im[li], ssm_c_re[li], ssm_c_im[li], ssm_d[li],
                          glu_w[li], glu_b[li], ssm_out_g[li])
        o_ssm = _ssm(u2, prm, b)
        w_o = w_out[li].astype(BF16)
        x = _out_proj(x, o_attn, o_ssm, attn_out_g[li].reshape(1, aw),
                      w_o[:aw], w_o[aw:], g1)
        x = _ffn(x, sh2, sc2, g2, norm2_g[li].reshape(1, d), ffn_w_up[li].astype(BF16),
                 ffn_conv_w[li], ffn_conv_b[li].reshape(1, -1), ffn_w_down[li].astype(BF16))
    return x
```

```python
import functools
import math

import jax
import jax.numpy as jnp
from jax import lax
from jax.experimental import pallas as pl
from jax.experimental.pallas import tpu as pltpu

F32 = jnp.float32
BF16 = jnp.bfloat16

EPS = 1e-6
HEAD_DIM = 64
SSM_GROUP = 16
SSM_STATE = 64
CONV_WIDTH = 3
N_MOD = 6

LANES = 128
SUBLANES = 8
VMEM_LIMIT_BYTES = 56 * 1024 * 1024

ADA_TN = 1536
PROJ_TM = 512
ATT_T = 128
ATT_QS = 512
ATT_WINDOW = 3
SSM_T = 64
FFN_TM = 512
FFN_CK = 256
FFN_HALO = 16

UNDERFLOW_BOUND = 152.0
LOG2_E = 1.4426950408889634


def _cparams(sem):
    return pltpu.CompilerParams(dimension_semantics=sem,
                                vmem_limit_bytes=VMEM_LIMIT_BYTES)


def _dot(a, b):
    return jnp.dot(a, b, preferred_element_type=F32)


def _ada_kernel(c_ref, w_ref, b_ref, o_ref):
    c = c_ref[...]
    c_act = c * jax.nn.sigmoid(c)
    c_hi = c_act.astype(BF16)
    c_lo = (c_act - c_hi.astype(F32)).astype(BF16)
    w = w_ref[0]
    w_hi = w.astype(BF16)
    w_lo = (w - w_hi.astype(F32)).astype(BF16)
    acc = _dot(c_hi, w_hi) + (_dot(c_hi, w_lo) + _dot(c_lo, w_hi))
    o_ref[0] = acc + b_ref[0]


def _ada_mod(c, ada_w, ada_b):
    depth, d, n = ada_w.shape
    b = c.shape[0]
    return pl.pallas_call(
        _ada_kernel,
        grid=(depth, n // ADA_TN),
        in_specs=[
            pl.BlockSpec((b, d), lambda l, j: (0, 0)),
            pl.BlockSpec((1, d, ADA_TN), lambda l, j: (l, 0, j)),
            pl.BlockSpec((1, 1, ADA_TN), lambda l, j: (l, 0, j)),
        ],
        out_specs=pl.BlockSpec((1, b, ADA_TN), lambda l, j: (l, 0, j)),
        out_shape=jax.ShapeDtypeStruct((depth, b, n), F32),
        compiler_params=_cparams(("parallel", "parallel")),
        name="ada_mod",
    )(c, ada_w, ada_b.reshape(depth, 1, n))


def _norm_mod(xf, g, sh, sc):
    inv = lax.rsqrt(jnp.mean(xf * xf, axis=-1, keepdims=True) + EPS)
    return xf * inv * (g * (1.0 + sc)) + sh


def _in_proj_kernel(x_ref, sh_ref, sc_ref, g_ref, w_ref, gq_ref, gk_ref, ones_ref,
                    q_ref, k_ref, v_ref, u_ref, *, aw):
    hb = _norm_mod(x_ref[0], g_ref[...], sh_ref[0], sc_ref[0]).astype(BF16)
    n_pair = aw // LANES

    def head_norm(p, g):
        ms = _dot((p * p).astype(BF16), ones_ref[...]) * (1.0 / HEAD_DIM)
        return p * lax.rsqrt(ms + EPS) * g

    q = head_norm(_dot(hb, w_ref[:, 0:aw]), gq_ref[...]).astype(BF16)
    for hp in range(n_pair):
        q_ref[0, hp] = q[:, hp * LANES:(hp + 1) * LANES]
    k = head_norm(_dot(hb, w_ref[:, aw:2 * aw]), gk_ref[...]).astype(BF16)
    for hp in range(n_pair):
        k_ref[0, hp] = k[:, hp * LANES:(hp + 1) * LANES]
    v = _dot(hb, w_ref[:, 2 * aw:3 * aw]).astype(BF16)
    for hp in range(n_pair):
        v_ref[0, hp] = v[:, hp * LANES:(hp + 1) * LANES]
    u_ref[...] = _dot(hb, w_ref[:, 3 * aw:]).astype(BF16)


def _in_proj(x, sh, sc, g, w_in_bf, gq, gk, ones_blk, aw, sw):
    b, l, d = x.shape
    tm = min(PROJ_TM, l)
    n_pair = aw // LANES
    qkv_shape = jax.ShapeDtypeStruct((b, n_pair, l, LANES), BF16)
    qkv_spec = pl.BlockSpec((1, n_pair, tm, LANES), lambda bi, i: (bi, 0, i, 0))
    vec = lambda n: pl.BlockSpec((1, n), lambda bi, i: (0, 0))
    bvec = pl.BlockSpec((1, 1, d), lambda bi, i: (bi, 0, 0))
    return pl.pallas_call(
        functools.partial(_in_proj_kernel, aw=aw),
        grid=(b, l // tm),
        in_specs=[
            pl.BlockSpec((1, tm, d), lambda bi, i: (bi, i, 0)),
            bvec, bvec, vec(d),
            pl.BlockSpec(w_in_bf.shape, lambda bi, i: (0, 0)),
            vec(aw), vec(aw),
            pl.BlockSpec((aw, aw), lambda bi, i: (0, 0)),
        ],
        out_specs=[qkv_spec, qkv_spec, qkv_spec,
                   pl.BlockSpec((tm, sw), lambda bi, i: (i, bi))],
        out_shape=[qkv_shape, qkv_shape, qkv_shape,
                   jax.ShapeDtypeStruct((l, b * sw), BF16)],
        compiler_params=_cparams(("parallel", "parallel")),
        name="in_proj",
    )(x, sh, sc, g, w_in_bf, gq, gk, ones_blk)


def _attn_kernel(q_ref, k_ref, v_ref, o_ref, acc_ref, carry_ref, next_ref, bound_ref):
    t = ATT_T
    n_sub = q_ref.shape[2] // t
    base = pl.program_id(2) * n_sub
    lane = lax.broadcasted_iota(jnp.int32, (1, LANES), 1)
    first = lane < HEAD_DIM
    row = lax.broadcasted_iota(jnp.int32, (t, t), 0)
    col = lax.broadcasted_iota(jnp.int32, (t, t), 1)
    later = jnp.where(row > col, 1.0, 0.0).astype(BF16)
    later2 = jnp.concatenate([later, later], axis=0)
    strict = col < row

    def split_heads(ref, j):
        blk = ref[0, 0, pl.ds(pl.multiple_of(j * t, t), t), :]
        zero = jnp.zeros_like(blk)
        return jnp.concatenate([jnp.where(first, blk, zero), jnp.where(first, zero, blk)], axis=0)

    def blocks(items, carry_in=None):
        zz = [lax.dot_general(q2, split_heads(k_ref, j), (((1,), (1,)), ((), ())),
                              preferred_element_type=F32) for q2, j, _ in items]
        logb, hilo, sums = [], [], []
        for (_, _, depth), zz_i in zip(items, zz):
            for h in range(2):
                z = zz_i[:, h * t:(h + 1) * t]
                s = jnp.maximum(z, 0.0) + jnp.log2(1.0 + jnp.exp2(-jnp.abs(z)))
                sm = jnp.where(strict, s, 0.0) if depth == 0 else s
                hi = sm.astype(BF16)
                lo = (sm - hi.astype(F32)).astype(BF16)
                logb.append(z - s)
                hilo.append(jnp.concatenate([hi, lo], axis=1))
                sums.append(jnp.sum(sm, axis=-1, keepdims=True))
        within = [_dot(x, later2) for x in hilo]
        ws, carries = [], []
        for n, (_, _, depth) in enumerate(items):
            carry = carries[-1] if depth > 0 else carry_in
            w_heads, after = [], []
            for h in range(2):
                arg = logb[2 * n + h] - within[2 * n + h]
                if carry is not None:
                    arg = arg - carry[h]
                w = jnp.exp2(arg)
                if depth == 0:
                    w = jnp.where(strict, w, 0.0)
                w_heads.append(w.astype(BF16))
                after.append(sums[2 * n + h] if carry is None else carry[h] + sums[2 * n + h])
            ws.append(jnp.concatenate(w_heads, axis=1))
            carries.append(after)
        pvs = [_dot(w, split_heads(v_ref, j)) for w, (_, j, _) in zip(ws, items)]
        return pvs, carries

    def first_blocks(depths):
        items, owner = [], []
        for k in range(n_sub):
            q2 = q_ref[0, 0, k * t:(k + 1) * t, :]
            for d in range(depths[k]):
                items.append((q2, base + k - d, d))
                owner.append(k)
        pvs, carries = blocks(items)
        for k in range(n_sub):
            mine = [n for n, o in enumerate(owner) if o == k]
            acc = pvs[mine[0]]
            for n in mine[1:]:
                acc = acc + pvs[n]
            carry = carries[mine[-1]]
            acc_ref[k] = acc
            for h in range(2):
                carry_ref[k, h] = carry[h]
            next_ref[k] = base + k - depths[k]
            bound_ref[k] = jnp.min(jnp.minimum(carry[0], carry[1]))

    @pl.when(base >= ATT_WINDOW - 1)
    def _():
        first_blocks([ATT_WINDOW] * n_sub)

    @pl.when(base < ATT_WINDOW - 1)
    def _():
        first_blocks([ATT_WINDOW if k >= ATT_WINDOW - 1 else 1 for k in range(n_sub)])

    def finish(k, _):
        rows = pl.ds(pl.multiple_of(k * t, t), t)
        q2 = q_ref[0, 0, rows, :]

        def cond(state):
            j, bound = state
            return jnp.logical_and(j >= 0, bound < UNDERFLOW_BOUND)

        def body(state):
            j, _ = state
            carry = [carry_ref[k, 0], carry_ref[k, 1]]
            (pv,), (carry,) = blocks([(q2, j, -1)], carry)
            acc_ref[k] += pv
            for h in range(2):
                carry_ref[k, h] = carry[h]
            return j - 1, jnp.min(jnp.minimum(carry[0], carry[1]))

        lax.while_loop(cond, body, (next_ref[k], bound_ref[k]))
        o_ref[0, rows, :] = acc_ref[k].astype(o_ref.dtype)
        return 0

    lax.fori_loop(0, n_sub, finish, 0)


def _attn(q, k, v):
    b, n_pair, l, _ = q.shape
    t = ATT_T
    qs = min(ATT_QS, l)
    kv_spec = pl.BlockSpec((1, 1, l, LANES), lambda bi, hp, i: (bi, hp, 0, 0))
    return pl.pallas_call(
        _attn_kernel,
        grid=(b, n_pair, l // qs),
        in_specs=[pl.BlockSpec((1, 1, qs, LANES), lambda bi, hp, i: (bi, hp, i, 0)),
                  kv_spec, kv_spec],
        out_specs=pl.BlockSpec((1, qs, LANES), lambda bi, hp, i: (bi, i, hp)),
        out_shape=jax.ShapeDtypeStruct((b, l, n_pair * LANES), BF16),
        scratch_shapes=[pltpu.VMEM((qs // t, t, LANES), F32),
                        pltpu.VMEM((qs // t, 2, t, 1), F32),
                        pltpu.SMEM((qs // t,), jnp.int32),
                        pltpu.SMEM((qs // t,), F32)],
        compiler_params=_cparams(("parallel", "parallel", "parallel")),
        name="attn",
    )(q, k, v)


def _ssm_kernel(u_ref, wb_ref, are_ref, aim_ref, wcre_ref, wcim_ref, d_ref,
                wglu_ref, bglu_ref, g_ref, o_ref, sre_ref, sim_ref, st_ref, y_ref,
                *, n_oct, batch):
    n_tok = u_ref.shape[0]
    sw = u_ref.shape[1] // batch
    oct_states = sre_ref.shape[1] // n_oct
    half = sre_ref.shape[1] // 2

    @pl.when(pl.program_id(0) == 0)
    def _():
        st_ref[...] = jnp.zeros_like(st_ref)

    for bi in range(batch):
        for o in range(n_oct):
            c0 = bi * sw + o * LANES
            y_ref[o, pl.ds(bi, n_tok, stride=batch), :] = u_ref[:, c0:c0 + LANES].astype(F32)
    for o in range(n_oct):
        bu = _dot(y_ref[o].astype(BF16), wb_ref[o])
        sre_ref[:, o * oct_states:(o + 1) * oct_states] = bu[:, :oct_states]
        sim_ref[:, o * oct_states:(o + 1) * oct_states] = bu[:, oct_states:]

    for part in range(2):
        cols = slice(part * half, (part + 1) * half)
        are = jnp.broadcast_to(are_ref[:, cols], (batch, half))
        aim = jnp.broadcast_to(aim_ref[:, cols], (batch, half))

        def step(t, state, cols=cols, are=are, aim=aim):
            s_re, s_im = state
            r = pl.ds(pl.multiple_of(t * batch, batch), batch)
            n_re = are * s_re - aim * s_im + sre_ref[r, cols]
            n_im = are * s_im + aim * s_re + sim_ref[r, cols]
            sre_ref[r, cols] = n_re
            sim_ref[r, cols] = n_im
            return n_re, n_im

        s_re, s_im = lax.fori_loop(0, n_tok, step,
                                   (st_ref[0, :, cols], st_ref[1, :, cols]),
                                   unroll=4)
        st_ref[0, :, cols] = s_re
        st_ref[1, :, cols] = s_im

    sumsq = None
    for o in range(n_oct):
        cs = slice(o * oct_states, (o + 1) * oct_states)
        lanes = slice(o * LANES, (o + 1) * LANES)
        y = (_dot(sre_ref[:, cs].astype(BF16), wcre_ref[o])
             + _dot(sim_ref[:, cs].astype(BF16), wcim_ref[o])
             + d_ref[:, lanes] * y_ref[o])
        y = jax.nn.gelu(y)
        gate = jax.nn.sigmoid(_dot(y.astype(BF16), wglu_ref[o]) + bglu_ref[:, lanes])
        out = y * gate
        y_ref[o] = out
        sq = jnp.sum(out * out, axis=-1, keepdims=True)
        sumsq = sq if sumsq is None else sumsq + sq
    inv = lax.rsqrt(sumsq * (1.0 / sw) + EPS)
    for o in range(n_oct):
        y_ref[o] = y_ref[o] * inv * g_ref[:, o * LANES:(o + 1) * LANES]
    for bi in range(batch):
        for o in range(n_oct):
            c0 = bi * sw + o * LANES
            o_ref[:, c0:c0 + LANES] = (
                y_ref[o, pl.ds(bi, n_tok, stride=batch), :].astype(o_ref.dtype))


def _ssm(u2, prm, batch):
    l, width = u2.shape
    sw = width // batch
    n_oct = sw // LANES
    n_state = prm["are"].shape[1]
    rows = SSM_T * batch
    full = lambda a: pl.BlockSpec(a.shape, lambda i: (0,) * a.ndim)
    names = ["wb", "are", "aim", "wcre", "wcim", "d", "wglu", "bglu", "g"]
    return pl.pallas_call(
        functools.partial(_ssm_kernel, n_oct=n_oct, batch=batch),
        grid=(l // SSM_T,),
        in_specs=[pl.BlockSpec((SSM_T, width), lambda i: (i, 0))]
                 + [full(prm[n]) for n in names],
        out_specs=pl.BlockSpec((SSM_T, width), lambda i: (i, 0)),
        out_shape=jax.ShapeDtypeStruct((l, width), BF16),
        scratch_shapes=[pltpu.VMEM((rows, n_state), F32),
                        pltpu.VMEM((rows, n_state), F32),
                        pltpu.VMEM((2, batch, n_state), F32),
                        pltpu.VMEM((n_oct, rows, LANES), F32)],
        compiler_params=_cparams(("arbitrary",)),
        name="ssm",
    )(u2, *[prm[n] for n in names])


def _ssm_params(a_re, a_im, log_dt, b_re, b_im, c_re, c_im, d_skip, glu_w, glu_b, out_g):
    g, p, h = b_re.shape
    per_oct = LANES // h
    n_oct = g // per_oct
    dt = jnp.exp(log_dt)[:, None]
    mag = jnp.exp(dt * a_re)
    abar_re = mag * jnp.cos(dt * a_im)
    abar_im = mag * jnp.sin(dt * a_im)
    em_re = abar_re - 1.0
    em_im = abar_im
    den = a_re * a_re + a_im * a_im
    f_re = (em_re * a_re + em_im * a_im) / den
    f_im = (em_im * a_re - em_re * a_im) / den
    bb_re = f_re[..., None] * b_re - f_im[..., None] * b_im
    bb_im = f_re[..., None] * b_im + f_im[..., None] * b_re
    eye = jnp.eye(per_oct, dtype=F32)

    def in_blockdiag(bb):
        bb = bb.reshape(n_oct, per_oct, p, h)
        m = jnp.einsum("ogph,gk->oghkp", bb, eye)
        return m.reshape(n_oct, per_oct * h, per_oct * p)

    def out_blockdiag(c):
        c = c.reshape(n_oct, per_oct, h, p)
        m = jnp.einsum("oghp,gk->ogpkh", c, eye)
        return m.reshape(n_oct, per_oct * p, per_oct * h)

    wb = jnp.concatenate([in_blockdiag(bb_re), in_blockdiag(bb_im)], axis=-1).astype(BF16)
    wglu = jnp.einsum("oghk,gf->oghfk", glu_w.reshape(n_oct, per_oct, h, h), eye)
    wglu = wglu.reshape(n_oct, per_oct * h, per_oct * h).astype(BF16)
    return {
        "wb": wb,
        "are": abar_re.reshape(1, g * p), "aim": abar_im.reshape(1, g * p),
        "wcre": out_blockdiag(c_re).astype(BF16),
        "wcim": out_blockdiag(-c_im).astype(BF16),
        "d": d_skip.reshape(1, g * h), "wglu": wglu,
        "bglu": glu_b.reshape(1, g * h), "g": out_g.reshape(1, g * h),
    }


def _out_proj_kernel(x_ref, oa_ref, os_ref, ga_ref, wa_ref, ws_ref, g1_ref, o_ref):
    oa = oa_ref[0].astype(F32)
    inv = lax.rsqrt(jnp.mean(oa * oa, axis=-1, keepdims=True) + EPS)
    on = (oa * inv * ga_ref[...]).astype(BF16)
    y = _dot(on, wa_ref[...]) + _dot(os_ref[...], ws_ref[...])
    o_ref[0] = x_ref[0] + g1_ref[0] * y


def _out_proj(x, o_attn, o_ssm2, ga, wa, ws, g1):
    b, l, d = x.shape
    aw = o_attn.shape[-1]
    sw = o_ssm2.shape[-1] // b
    tm = min(PROJ_TM, l)
    return pl.pallas_call(
        _out_proj_kernel,
        grid=(b, l // tm),
        in_specs=[
            pl.BlockSpec((1, tm, d), lambda bi, i: (bi, i, 0)),
            pl.BlockSpec((1, tm, aw), lambda bi, i: (bi, i, 0)),
            pl.BlockSpec((tm, sw), lambda bi, i: (i, bi)),
            pl.BlockSpec((1, aw), lambda bi, i: (0, 0)),
            pl.BlockSpec(wa.shape, lambda bi, i: (0, 0)),
            pl.BlockSpec(ws.shape, lambda bi, i: (0, 0)),
            pl.BlockSpec((1, 1, d), lambda bi, i: (bi, 0, 0)),
        ],
        out_specs=pl.BlockSpec((1, tm, d), lambda bi, i: (bi, i, 0)),
        out_shape=jax.ShapeDtypeStruct(x.shape, F32),
        compiler_params=_cparams(("parallel", "parallel")),
        name="out_proj",
    )(x, o_attn, o_ssm2, ga, wa, ws, g1)


def _ffn_kernel(x_ref, xh_ref, sh_ref, sc_ref, g2_ref, g_ref, wup_ref, cw_ref, cb_ref,
                wdn_ref, o_ref, h_ref, up_ref, a_ref, *, d_ff):
    tm = x_ref.shape[1]
    n_chunk = d_ff // FFN_CK
    g, sh, sc = g_ref[...], sh_ref[0], sc_ref[0]
    h_ref[FFN_HALO:, :] = _norm_mod(x_ref[0], g, sh, sc).astype(BF16)
    keep = (pl.program_id(1) > 0).astype(F32)
    h_ref[:FFN_HALO, :] = (_norm_mod(xh_ref[0], g, sh, sc) * keep).astype(BF16)

    def chunk(start):
        if not isinstance(start, int):
            start = pl.multiple_of(start, FFN_CK)
        return pl.ds(start, FFN_CK)

    def cols(c, part):
        return chunk(part * d_ff + c * FFN_CK)

    def up_proj(c, slot):
        for part in range(2):
            up_ref[slot, part] = _dot(h_ref[...], wup_ref[:, cols(c, part)])

    def conv(c, slot, part):
        out = cb_ref[:, cols(c, part)]
        for i in range(CONV_WIDTH):
            off = FFN_HALO - (CONV_WIDTH - 1) + i
            out = out + up_ref[slot, part, off:off + tm, :] * cw_ref[i:i + 1, cols(c, part)]
        return out

    def activate(c, slot):
        a_ref[:, chunk(c * FFN_CK)] = (
            jax.nn.gelu(conv(c, slot, 1)) * conv(c, slot, 0)).astype(BF16)

    for c in range(n_chunk):
        up_proj(c, c % 2)
        activate(c, c % 2)
    o_ref[0] = x_ref[0] + g2_ref[0] * _dot(a_ref[...], wdn_ref[...])


def _ffn(x, sh, sc, g2, g, wup, cw, cb, wdn):
    b, l, d = x.shape
    d_ff = wdn.shape[0]
    tm = min(FFN_TM, l)
    halo_per_tile = tm // FFN_HALO
    bvec = pl.BlockSpec((1, 1, d), lambda bi, i: (bi, 0, 0))
    const = lambda a: pl.BlockSpec(a.shape, lambda bi, i: (0,) * a.ndim,
                                   pipeline_mode=pl.Buffered(1))
    return pl.pallas_call(
        functools.partial(_ffn_kernel, d_ff=d_ff),
        grid=(b, l // tm),
        in_specs=[
            pl.BlockSpec((1, tm, d), lambda bi, i: (bi, i, 0)),
            pl.BlockSpec((1, FFN_HALO, d),
                         lambda bi, i: (bi, jnp.maximum(i * halo_per_tile - 1, 0), 0)),
            bvec, bvec, bvec, const(g), const(wup), const(cw), const(cb), const(wdn),
        ],
        out_specs=pl.BlockSpec((1, tm, d), lambda bi, i: (bi, i, 0)),
        out_shape=jax.ShapeDtypeStruct(x.shape, F32),
        scratch_shapes=[pltpu.VMEM((tm + FFN_HALO, d), BF16),
                        pltpu.VMEM((2, 2, tm + FFN_HALO, FFN_CK), F32),
                        pltpu.VMEM((tm, d_ff), BF16)],
        compiler_params=_cparams(("parallel", "parallel")),
        name="ffn",
    )(x, x, sh, sc, g2, g, wup, cw, cb, wdn)


def kernel(x, c, ada_w, ada_b, norm1_g, w_in, q_norm_g, k_norm_g, ssm_a_re, ssm_a_im,
           ssm_log_dt, ssm_b_re, ssm_b_im, ssm_c_re, ssm_c_im, ssm_d, glu_w, glu_b,
           attn_out_g, ssm_out_g, w_out, norm2_g, ffn_w_up, ffn_conv_w, ffn_conv_b,
           ffn_w_down):
    b, l, d = x.shape
    depth = ada_w.shape[0]
    aw = attn_out_g.shape[-1]
    sw = ssm_out_g.shape[-1]
    n_heads = aw // HEAD_DIM
    assert b == SUBLANES and aw % LANES == 0 and sw % LANES == 0
    assert l % ATT_T == 0 and l % SSM_T == 0

    mod = _ada_mod(c, ada_w, ada_b)
    head = jnp.arange(aw) // HEAD_DIM
    ones_blk = (head[:, None] == head[None, :]).astype(BF16)
    inv_sqrt = LOG2_E / math.sqrt(HEAD_DIM)

    for li in range(depth):
        m = mod[li].reshape(b, N_MOD, 1, d)
        sh1, sc1, g1, sh2, sc2, g2 = (m[:, i] for i in range(N_MOD))
        gq = jnp.tile(q_norm_g[li] * inv_sqrt, n_heads).reshape(1, aw)
        gk = jnp.tile(k_norm_g[li], n_heads).reshape(1, aw)
        q, k, v, u2 = _in_proj(x, sh1, sc1, norm1_g[li].reshape(1, d),
                               w_in[li].astype(BF16), gq, gk, ones_blk, aw, sw)
        o_attn = _attn(q, k, v)
        prm = _ssm_params(ssm_a_re[li], ssm_a_im[li], ssm_log_dt[li], ssm_b_re[li],
                          ssm_b_im[li], ssm_c_re[li], ssm_c_im[li], ssm_d[li],
                          glu_w[li], glu_b[li], ssm_out_g[li])
        o_ssm = _ssm(u2, prm, b)
        w_o = w_out[li].astype(BF16)
        x = _out_proj(x, o_attn, o_ssm, attn_out_g[li].reshape(1, aw),
                      w_o[:aw], w_o[aw:], g1)
        x = _ffn(x, sh2, sc2, g2, norm2_g[li].reshape(1, d), ffn_w_up[li].astype(BF16),
                 ffn_conv_w[li], ffn_conv_b[li].reshape(1, -1), ffn_w_down[li].astype(BF16))
    return x
```

```python
import functools
import math

import jax
import jax.numpy as jnp
from jax import lax
from jax.experimental import pallas as pl
from jax.experimental.pallas import tpu as pltpu

F32 = jnp.float32
BF16 = jnp.bfloat16

EPS = 1e-6
HEAD_DIM = 64
SSM_GROUP = 16
SSM_STATE = 64
CONV_WIDTH = 3
N_MOD = 6

LANES = 128
SUBLANES = 8
VMEM_LIMIT_BYTES = 56 * 1024 * 1024

ADA_TN = 1536
PROJ_TM = 512
ATT_T = 128
ATT_QS = 512
ATT_WINDOW = 3
SSM_T = 64
FFN_TM = 512
FFN_CK = 256
FFN_HALO = 16

UNDERFLOW_BOUND = 152.0
LOG2_E = 1.4426950408889634


def _cparams(sem):
    return pltpu.CompilerParams(dimension_semantics=sem,
                                vmem_limit_bytes=VMEM_LIMIT_BYTES)


def _dot(a, b):
    return jnp.dot(a, b, preferred_element_type=F32)


def _ada_kernel(c_ref, w_ref, b_ref, o_ref):
    c = c_ref[...]
    c_act = c * jax.nn.sigmoid(c)
    c_hi = c_act.astype(BF16)
    c_lo = (c_act - c_hi.astype(F32)).astype(BF16)
    w = w_ref[0]
    w_hi = w.astype(BF16)
    w_lo = (w - w_hi.astype(F32)).astype(BF16)
    acc = _dot(c_hi, w_hi) + (_dot(c_hi, w_lo) + _dot(c_lo, w_hi))
    o_ref[0] = acc + b_ref[0]


def _ada_mod(c, ada_w, ada_b):
    depth, d, n = ada_w.shape
    b = c.shape[0]
    return pl.pallas_call(
        _ada_kernel,
        grid=(depth, n // ADA_TN),
        in_specs=[
            pl.BlockSpec((b, d), lambda l, j: (0, 0)),
            pl.BlockSpec((1, d, ADA_TN), lambda l, j: (l, 0, j)),
            pl.BlockSpec((1, 1, ADA_TN), lambda l, j: (l, 0, j)),
        ],
        out_specs=pl.BlockSpec((1, b, ADA_TN), lambda l, j: (l, 0, j)),
        out_shape=jax.ShapeDtypeStruct((depth, b, n), F32),
        compiler_params=_cparams(("parallel", "parallel")),
        name="ada_mod",
    )(c, ada_w, ada_b.reshape(depth, 1, n))


def _norm_mod(xf, g, sh, sc):
    inv = lax.rsqrt(jnp.mean(xf * xf, axis=-1, keepdims=True) + EPS)
    return xf * inv * (g * (1.0 + sc)) + sh


def _in_proj_kernel(x_ref, sh_ref, sc_ref, g_ref, w_ref, gq_ref, gk_ref, ones_ref,
                    q_ref, k_ref, v_ref, u_ref, *, aw):
    hb = _norm_mod(x_ref[0], g_ref[...], sh_ref[0], sc_ref[0]).astype(BF16)
    n_pair = aw // LANES

    def head_norm(p, g):
        ms = _dot((p * p).astype(BF16), ones_ref[...]) * (1.0 / HEAD_DIM)
        return p * lax.rsqrt(ms + EPS) * g

    q = head_norm(_dot(hb, w_ref[:, 0:aw]), gq_ref[...]).astype(BF16)
    for hp in range(n_pair):
        q_ref[0, hp] = q[:, hp * LANES:(hp + 1) * LANES]
    k = head_norm(_dot(hb, w_ref[:, aw:2 * aw]), gk_ref[...]).astype(BF16)
    for hp in range(n_pair):
        k_ref[0, hp] = k[:, hp * LANES:(hp + 1) * LANES]
    v = _dot(hb, w_ref[:, 2 * aw:3 * aw]).astype(BF16)
    for hp in range(n_pair):
        v_ref[0, hp] = v[:, hp * LANES:(hp + 1) * LANES]
    u_ref[...] = _dot(hb, w_ref[:, 3 * aw:]).astype(BF16)


def _in_proj(x, sh, sc, g, w_in_bf, gq, gk, ones_blk, aw, sw):
    b, l, d = x.shape
    tm = min(PROJ_TM, l)
    n_pair = aw // LANES
    qkv_shape = jax.ShapeDtypeStruct((b, n_pair, l, LANES), BF16)
    qkv_spec = pl.BlockSpec((1, n_pair, tm, LANES), lambda bi, i: (bi, 0, i, 0))
    vec = lambda n: pl.BlockSpec((1, n), lambda bi, i: (0, 0))
    bvec = pl.BlockSpec((1, 1, d), lambda bi, i: (bi, 0, 0))
    return pl.pallas_call(
        functools.partial(_in_proj_kernel, aw=aw),
        grid=(b, l // tm),
        in_specs=[
            pl.BlockSpec((1, tm, d), lambda bi, i: (bi, i, 0)),
            bvec, bvec, vec(d),
            pl.BlockSpec(w_in_bf.shape, lambda bi, i: (0, 0)),
            vec(aw), vec(aw),
            pl.BlockSpec((aw, aw), lambda bi, i: (0, 0)),
        ],
        out_specs=[qkv_spec, qkv_spec, qkv_spec,
                   pl.BlockSpec((tm, sw), lambda bi, i: (i, bi))],
        out_shape=[qkv_shape, qkv_shape, qkv_shape,
                   jax.ShapeDtypeStruct((l, b * sw), BF16)],
        compiler_params=_cparams(("parallel", "parallel")),
        name="in_proj",
    )(x, sh, sc, g, w_in_bf, gq, gk, ones_blk)


def _attn_kernel(q_ref, k_ref, v_ref, o_ref, acc_ref, carry_ref, next_ref, bound_ref):
    t = ATT_T
    n_sub = q_ref.shape[2] // t
    base = pl.program_id(2) * n_sub
    lane = lax.broadcasted_iota(jnp.int32, (1, LANES), 1)
    first = lane < HEAD_DIM
    row = lax.broadcasted_iota(jnp.int32, (t, t), 0)
    col = lax.broadcasted_iota(jnp.int32, (t, t), 1)
    later = jnp.where(row > col, 1.0, 0.0).astype(BF16)
    later2 = jnp.concatenate([later, later], axis=0)
    strict = col < row

    def split_heads(ref, j):
        blk = ref[0, 0, pl.ds(pl.multiple_of(j * t, t), t), :]
        zero = jnp.zeros_like(blk)
        return jnp.concatenate([jnp.where(first, blk, zero), jnp.where(first, zero, blk)], axis=0)

    def blocks(items, carry_in=None):
        zz = [lax.dot_general(q2, split_heads(k_ref, j), (((1,), (1,)), ((), ())),
                              preferred_element_type=F32) for q2, j, _ in items]
        logb, hilo, sums = [], [], []
        for (_, _, depth), zz_i in zip(items, zz):
            for h in range(2):
                z = zz_i[:, h * t:(h + 1) * t]
                s = jnp.maximum(z, 0.0) + jnp.log2(1.0 + jnp.exp2(-jnp.abs(z)))
                sm = jnp.where(strict, s, 0.0) if depth == 0 else s
                hi = sm.astype(BF16)
                lo = (sm - hi.astype(F32)).astype(BF16)
                logb.append(z - s)
                hilo.append(jnp.concatenate([hi, lo], axis=1))
                sums.append(jnp.sum(sm, axis=-1, keepdims=True))
        within = [_dot(x, later2) for x in hilo]
        ws, carries = [], []
        for n, (_, _, depth) in enumerate(items):
            carry = carries[-1] if depth > 0 else carry_in
            w_heads, after = [], []
            for h in range(2):
                arg = logb[2 * n + h] - within[2 * n + h]
                if carry is not None:
                    arg = arg - carry[h]
                w = jnp.exp2(arg)
                if depth == 0:
                    w = jnp.where(strict, w, 0.0)
                w_heads.append(w.astype(BF16))
                after.append(sums[2 * n + h] if carry is None else carry[h] + sums[2 * n + h])
            ws.append(jnp.concatenate(w_heads, axis=1))
            carries.append(after)
        pvs = [_dot(w, split_heads(v_ref, j)) for w, (_, j, _) in zip(ws, items)]
        return pvs, carries

    def first_blocks(depths):
        items, owner = [], []
        for k in range(n_sub):
            q2 = q_ref[0, 0, k * t:(k + 1) * t, :]
            for d in range(depths[k]):
                items.append((q2, base + k - d, d))
                owner.append(k)
        pvs, carries = blocks(items)
        for k in range(n_sub):
            mine = [n for n, o in enumerate(owner) if o == k]
            acc = pvs[mine[0]]
            for n in mine[1:]:
                acc = acc + pvs[n]
            carry = carries[mine[-1]]
            acc_ref[k] = acc
            for h in range(2):
                carry_ref[k, h] = carry[h]
            next_ref[k] = base + k - depths[k]
            bound_ref[k] = jnp.min(jnp.minimum(carry[0], carry[1]))

    @pl.when(base >= ATT_WINDOW - 1)
    def _():
        first_blocks([ATT_WINDOW] * n_sub)

    @pl.when(base < ATT_WINDOW - 1)
    def _():
        first_blocks([ATT_WINDOW if k >= ATT_WINDOW - 1 else 1 for k in range(n_sub)])

    def finish(k, _):
        rows = pl.ds(pl.multiple_of(k * t, t), t)
        q2 = q_ref[0, 0, rows, :]

        def cond(state):
            j, bound = state
            return jnp.logical_and(j >= 0, bound < UNDERFLOW_BOUND)

        def body(state):
            j, _ = state
            carry = [carry_ref[k, 0], carry_ref[k, 1]]
            (pv,), (carry,) = blocks([(q2, j, -1)], carry)
            acc_ref[k] += pv
            for h in range(2):
                carry_ref[k, h] = carry[h]
            return j - 1, jnp.min(jnp.minimum(carry[0], carry[1]))

        lax.while_loop(cond, body, (next_ref[k], bound_ref[k]))
        o_ref[0, rows, :] = acc_ref[k].astype(o_ref.dtype)
        return 0

    lax.fori_loop(0, n_sub, finish, 0)


def _attn(q, k, v):
    b, n_pair, l, _ = q.shape
    t = ATT_T
    qs = min(ATT_QS, l)
    kv_spec = pl.BlockSpec((1, 1, l, LANES), lambda bi, hp, i: (bi, hp, 0, 0))
    return pl.pallas_call(
        _attn_kernel,
        grid=(b, n_pair, l // qs),
        in_specs=[pl.BlockSpec((1, 1, qs, LANES), lambda bi, hp, i: (bi, hp, i, 0)),
                  kv_spec, kv_spec],
        out_specs=pl.BlockSpec((1, qs, LANES), lambda bi, hp, i: (bi, i, hp)),
        out_shape=jax.ShapeDtypeStruct((b, l, n_pair * LANES), BF16),
        scratch_shapes=[pltpu.VMEM((qs // t, t, LANES), F32),
                        pltpu.VMEM((qs // t, 2, t, 1), F32),
                        pltpu.SMEM((qs // t,), jnp.int32),
                        pltpu.SMEM((qs // t,), F32)],
        compiler_params=_cparams(("parallel", "parallel", "parallel")),
        name="attn",
    )(q, k, v)


def _ssm_kernel(u_ref, wb_ref, are_ref, aim_ref, wcre_ref, wcim_ref, d_ref,
                wglu_ref, bglu_ref, g_ref, o_ref, sre_ref, sim_ref, st_ref, y_ref,
                *, n_oct, batch):
    n_tok = u_ref.shape[0]
    sw = u_ref.shape[1] // batch
    oct_states = sre_ref.shape[1] // n_oct
    half = sre_ref.shape[1] // 2

    @pl.when(pl.program_id(0) == 0)
    def _():
        st_ref[...] = jnp.zeros_like(st_ref)

    for bi in range(batch):
        for o in range(n_oct):
            c0 = bi * sw + o * LANES
            y_ref[o, pl.ds(bi, n_tok, stride=batch), :] = u_ref[:, c0:c0 + LANES].astype(F32)
    for o in range(n_oct):
        bu = _dot(y_ref[o].astype(BF16), wb_ref[o])
        sre_ref[:, o * oct_states:(o + 1) * oct_states] = bu[:, :oct_states]
        sim_ref[:, o * oct_states:(o + 1) * oct_states] = bu[:, oct_states:]

    for part in range(2):
        cols = slice(part * half, (part + 1) * half)
        are = jnp.broadcast_to(are_ref[:, cols], (batch, half))
        aim = jnp.broadcast_to(aim_ref[:, cols], (batch, half))

        def step(t, state, cols=cols, are=are, aim=aim):
            s_re, s_im = state
            r = pl.ds(pl.multiple_of(t * batch, batch), batch)
            n_re = are * s_re - aim * s_im + sre_ref[r, cols]
            n_im = are * s_im + aim * s_re + sim_ref[r, cols]
            sre_ref[r, cols] = n_re
            sim_ref[r, cols] = n_im
            return n_re, n_im

        s_re, s_im = lax.fori_loop(0, n_tok, step,
                                   (st_ref[0, :, cols], st_ref[1, :, cols]),
                                   unroll=4)
        st_ref[0, :, cols] = s_re
        st_ref[1, :, cols] = s_im

    sumsq = None
    for o in range(n_oct):
        cs = slice(o * oct_states, (o + 1) * oct_states)
        lanes = slice(o * LANES, (o + 1) * LANES)
        y = (_dot(sre_ref[:, cs].astype(BF16), wcre_ref[o])
             + _dot(sim_ref[:, cs].astype(BF16), wcim_ref[o])
             + d_ref[:, lanes] * y_ref[o])
        y = jax.nn.gelu(y)
        gate = jax.nn.sigmoid(_dot(y.astype(BF16), wglu_ref[o]) + bglu_ref[:, lanes])
        out = y * gate
        y_ref[o] = out
        sq = jnp.sum(out * out, axis=-1, keepdims=True)
        sumsq = sq if sumsq is None else sumsq + sq
    inv = lax.rsqrt(sumsq * (1.0 / sw) + EPS)
    for o in range(n_oct):
        y_ref[o] = y_ref[o] * inv * g_ref[:, o * LANES:(o + 1) * LANES]
    for bi in range(batch):
        for o in range(n_oct):
            c0 = bi * sw + o * LANES
            o_ref[:, c0:c0 + LANES] = (
                y_ref[o, pl.ds(bi, n_tok, stride=batch), :].astype(o_ref.dtype))


def _ssm(u2, prm, batch):
    l, width = u2.shape
    sw = width // batch
    n_oct = sw // LANES
    n_state = prm["are"].shape[1]
    rows = SSM_T * batch
    full = lambda a: pl.BlockSpec(a.shape, lambda i: (0,) * a.ndim)
    names = ["wb", "are", "aim", "wcre", "wcim", "d", "wglu", "bglu", "g"]
    return pl.pallas_call(
        functools.partial(_ssm_kernel, n_oct=n_oct, batch=batch),
        grid=(l // SSM_T,),
        in_specs=[pl.BlockSpec((SSM_T, width), lambda i: (i, 0))]
                 + [full(prm[n]) for n in names],
        out_specs=pl.BlockSpec((SSM_T, width), lambda i: (i, 0)),
        out_shape=jax.ShapeDtypeStruct((l, width), BF16),
        scratch_shapes=[pltpu.VMEM((rows, n_state), F32),
                        pltpu.VMEM((rows, n_state), F32),
                        pltpu.VMEM((2, batch, n_state), F32),
                        pltpu.VMEM((n_oct, rows, LANES), F32)],
        compiler_params=_cparams(("arbitrary",)),
        name="ssm",
    )(u2, *[prm[n] for n in names])


def _ssm_params(a_re, a_im, log_dt, b_re, b_im, c_re, c_im, d_skip, glu_w, glu_b, out_g):
    g, p, h = b_re.shape
    per_oct = LANES // h
    n_oct = g // per_oct
    dt = jnp.exp(log_dt)[:, None]
    mag = jnp.exp(dt * a_re)
    abar_re = mag * jnp.cos(dt * a_im)
    abar_im = mag * jnp.sin(dt * a_im)
    em_re = abar_re - 1.0
    em_im = abar_im
    den = a_re * a_re + a_im * a_im
    f_re = (em_re * a_re + em_im * a_im) / den
    f_im = (em_im * a_re - em_re * a_im) / den
    bb_re = f_re[..., None] * b_re - f_im[..., None] * b_im
    bb_im = f_re[..., None] * b_im + f_im[..., None] * b_re
    eye = jnp.eye(per_oct, dtype=F32)

    def in_blockdiag(bb):
        bb = bb.reshape(n_oct, per_oct, p, h)
        m = jnp.einsum("ogph,gk->oghkp", bb, eye)
        return m.reshape(n_oct, per_oct * h, per_oct * p)

    def out_blockdiag(c):
        c = c.reshape(n_oct, per_oct, h, p)
        m = jnp.einsum("oghp,gk->ogpkh", c, eye)
        return m.reshape(n_oct, per_oct * p, per_oct * h)

    wb = jnp.concatenate([in_blockdiag(bb_re), in_blockdiag(bb_im)], axis=-1).astype(BF16)
    wglu = jnp.einsum("oghk,gf->oghfk", glu_w.reshape(n_oct, per_oct, h, h), eye)
    wglu = wglu.reshape(n_oct, per_oct * h, per_oct * h).astype(BF16)
    return {
        "wb": wb,
        "are": abar_re.reshape(1, g * p), "aim": abar_im.reshape(1, g * p),
        "wcre": out_blockdiag(c_re).astype(BF16),
        "wcim": out_blockdiag(-c_im).astype(BF16),
        "d": d_skip.reshape(1, g * h), "wglu": wglu,
        "bglu": glu_b.reshape(1, g * h), "g": out_g.reshape(1, g * h),
    }


def _out_proj_kernel(x_ref, oa_ref, os_ref, ga_ref, wa_ref, ws_ref, g1_ref, o_ref):
    oa = oa_ref[0].astype(F32)
    inv = lax.rsqrt(jnp.mean(oa * oa, axis=-1, keepdims=True) + EPS)
    on = (oa * inv * ga_ref[...]).astype(BF16)
    y = _dot(on, wa_ref[...]) + _dot(os_ref[...], ws_ref[...])
    o_ref[0] = x_ref[0] + g1_ref[0] * y


def _out_proj(x, o_attn, o_ssm2, ga, wa, ws, g1):
    b, l, d = x.shape
    aw = o_attn.shape[-1]
    sw = o_ssm2.shape[-1] // b
    tm = min(PROJ_TM, l)
    return pl.pallas_call(
        _out_proj_kernel,
        grid=(b, l // tm),
        in_specs=[
            pl.BlockSpec((1, tm, d), lambda bi, i: (bi, i, 0)),
            pl.BlockSpec((1, tm, aw), lambda bi, i: (bi, i, 0)),
            pl.BlockSpec((tm, sw), lambda bi, i: (i, bi)),
            pl.BlockSpec((1, aw), lambda bi, i: (0, 0)),
            pl.BlockSpec(wa.shape, lambda bi, i: (0, 0)),
            pl.BlockSpec(ws.shape, lambda bi, i: (0, 0)),
            pl.BlockSpec((1, 1, d), lambda bi, i: (bi, 0, 0)),
        ],
        out_specs=pl.BlockSpec((1, tm, d), lambda bi, i: (bi, i, 0)),
        out_shape=jax.ShapeDtypeStruct(x.shape, F32),
        compiler_params=_cparams(("parallel", "parallel")),
        name="out_proj",
    )(x, o_attn, o_ssm2, ga, wa, ws, g1)


def _ffn_kernel(x_ref, xh_ref, sh_ref, sc_ref, g2_ref, g_ref, wup_ref, cw_ref, cb_ref,
                wdn_ref, o_ref, perm_ref, h_ref, up_ref, a_ref, *, d_ff):
    tm, d = x_ref.shape[1], x_ref.shape[2]
    n_grp = tm // SUBLANES
    n_slab = d // LANES
    n_chunk = d_ff // FFN_CK
    g, sh, sc = g_ref[...], sh_ref[0], sc_ref[0]

    def natural_rows(s):
        return slice(s * n_grp, (s + 1) * n_grp)

    def interleaved_rows(s):
        return pl.ds(s, n_grp, stride=SUBLANES)

    hn = _norm_mod(x_ref[0], g, sh, sc)
    for slab in range(n_slab):
        lanes = slice(slab * LANES, (slab + 1) * LANES)
        for s in range(SUBLANES):
            perm_ref[slab, interleaved_rows(s), :] = hn[natural_rows(s), lanes]
    for slab in range(n_slab):
        h_ref[:tm, slab * LANES:(slab + 1) * LANES] = perm_ref[slab].astype(BF16)
    keep = (pl.program_id(1) > 0).astype(F32)
    h_ref[tm:, :] = (_norm_mod(xh_ref[0], g, sh, sc) * keep).astype(BF16)
    sublane = lax.broadcasted_iota(jnp.int32, (SUBLANES, FFN_CK), 0)

    def chunk(start):
        if not isinstance(start, int):
            start = pl.multiple_of(start, FFN_CK)
        return pl.ds(start, FFN_CK)

    def cols(c, part):
        return chunk(part * d_ff + c * FFN_CK)

    def wrapped(group, before):
        return jnp.where(sublane == 0, before, pltpu.roll(group, 1, axis=0))

    def up_proj(c, slot):
        for part in range(2):
            r = _dot(h_ref[...], wup_ref[:, cols(c, part)])
            pad = (CONV_WIDTH - 1) * SUBLANES
            up_ref[slot, part, pad:, :] = r[:tm]
            for back in range(1, CONV_WIDTH):
                up_ref[slot, part, pad - back * SUBLANES:pad - (back - 1) * SUBLANES, :] = wrapped(
                    r[tm - back * SUBLANES:tm - (back - 1) * SUBLANES],
                    r[tm + FFN_HALO - back:tm + FFN_HALO - back + 1])

    def conv(c, slot, part):
        out = cb_ref[:, cols(c, part)]
        for i in range(CONV_WIDTH):
            off = i * SUBLANES
            out = out + up_ref[slot, part, off:off + tm, :] * cw_ref[i:i + 1, cols(c, part)]
        return out

    def activate(c, slot):
        a_ref[:, chunk(c * FFN_CK)] = (
            jax.nn.gelu(conv(c, slot, 1)) * conv(c, slot, 0)).astype(BF16)

    for c in range(n_chunk):
        up_proj(c, c % 2)
        activate(c, c % 2)
    y = _dot(a_ref[...], wdn_ref[...])
    for slab in range(n_slab):
        perm_ref[slab] = y[:, slab * LANES:(slab + 1) * LANES]
    for slab in range(n_slab):
        lanes = slice(slab * LANES, (slab + 1) * LANES)
        for s in range(SUBLANES):
            o_ref[0, natural_rows(s), lanes] = (
                x_ref[0, natural_rows(s), lanes]
                + g2_ref[0, :, lanes] * perm_ref[slab, interleaved_rows(s), :])


def _ffn(x, sh, sc, g2, g, wup, cw, cb, wdn):
    b, l, d = x.shape
    d_ff = wdn.shape[0]
    tm = min(FFN_TM, l)
    halo_per_tile = tm // FFN_HALO
    bvec = pl.BlockSpec((1, 1, d), lambda bi, i: (bi, 0, 0))
    const = lambda a: pl.BlockSpec(a.shape, lambda bi, i: (0,) * a.ndim,
                                   pipeline_mode=pl.Buffered(1))
    return pl.pallas_call(
        functools.partial(_ffn_kernel, d_ff=d_ff),
        grid=(b, l // tm),
        in_specs=[
            pl.BlockSpec((1, tm, d), lambda bi, i: (bi, i, 0)),
            pl.BlockSpec((1, FFN_HALO, d),
                         lambda bi, i: (bi, jnp.maximum(i * halo_per_tile - 1, 0), 0)),
            bvec, bvec, bvec, const(g), const(wup), const(cw), const(cb), const(wdn),
        ],
        out_specs=pl.BlockSpec((1, tm, d), lambda bi, i: (bi, i, 0)),
        out_shape=jax.ShapeDtypeStruct(x.shape, F32),
        scratch_shapes=[pltpu.VMEM((d // LANES, tm, LANES), F32),
                        pltpu.VMEM((tm + FFN_HALO, d), BF16),
                        pltpu.VMEM((2, 2, tm + (CONV_WIDTH - 1) * SUBLANES, FFN_CK), F32),
                        pltpu.VMEM((tm, d_ff), BF16)],
        compiler_params=_cparams(("parallel", "parallel")),
        name="ffn",
    )(x, x, sh, sc, g2, g, wup, cw, cb, wdn)


def kernel(x, c, ada_w, ada_b, norm1_g, w_in, q_norm_g, k_norm_g, ssm_a_re, ssm_a_im,
           ssm_log_dt, ssm_b_re, ssm_b_im, ssm_c_re, ssm_c_im, ssm_d, glu_w, glu_b,
           attn_out_g, ssm_out_g, w_out, norm2_g, ffn_w_up, ffn_conv_w, ffn_conv_b,
           ffn_w_down):
    b, l, d = x.shape
    depth = ada_w.shape[0]
    aw = attn_out_g.shape[-1]
    sw = ssm_out_g.shape[-1]
    n_heads = aw // HEAD_DIM
    assert b == SUBLANES and aw % LANES == 0 and sw % LANES == 0
    assert l % ATT_T == 0 and l % SSM_T == 0

    mod = _ada_mod(c, ada_w, ada_b)
    head = jnp.arange(aw) // HEAD_DIM
    ones_blk = (head[:, None] == head[None, :]).astype(BF16)
    inv_sqrt = LOG2_E / math.sqrt(HEAD_DIM)

    for li in range(depth):
        m = mod[li].reshape(b, N_MOD, 1, d)
        sh1, sc1, g1, sh2, sc2, g2 = (m[:, i] for i in range(N_MOD))
        gq = jnp.tile(q_norm_g[li] * inv_sqrt, n_heads).reshape(1, aw)
        gk = jnp.tile(k_norm_g[li], n_heads).reshape(1, aw)
        q, k, v, u2 = _in_proj(x, sh1, sc1, norm1_g[li].reshape(1, d),
                               w_in[li].astype(BF16), gq, gk, ones_blk, aw, sw)
        o_attn = _attn(q, k, v)
        prm = _ssm_params(ssm_a_re[li], ssm_a_im[li], ssm_log_dt[li], ssm_b_re[li],
                          ssm_b_im[li], ssm_c_re[li], ssm_c_im[li], ssm_d[li],
                          glu_w[li], glu_b[li], ssm_out_g[li])
        o_ssm = _ssm(u2, prm, b)
        w_o = w_out[li].astype(BF16)
        x = _out_proj(x, o_attn, o_ssm, attn_out_g[li].reshape(1, aw),
                      w_o[:aw], w_o[aw:], g1)
        x = _ffn(x, sh2, sc2, g2, norm2_g[li].reshape(1, d), ffn_w_up[li].astype(BF16),
                 ffn_conv_w[li], ffn_conv_b[li].reshape(1, -1), ffn_w_down[li].astype(BF16))
    return x
```

```python
import functools
import math

import jax
import jax.numpy as jnp
from jax import lax
from jax.experimental import pallas as pl
from jax.experimental.pallas import tpu as pltpu

F32 = jnp.float32
BF16 = jnp.bfloat16

EPS = 1e-6
HEAD_DIM = 64
SSM_GROUP = 16
SSM_STATE = 64
CONV_WIDTH = 3
N_MOD = 6

LANES = 128
SUBLANES = 8
VMEM_LIMIT_BYTES = 56 * 1024 * 1024

ADA_TN = 1536
PROJ_TM = 512
ATT_T = 128
ATT_QS = 512
ATT_WINDOW = 3
SSM_T = 64
FFN_TM = 512
FFN_CK = 256
FFN_HALO = 16

UNDERFLOW_BOUND = 152.0
LOG2_E = 1.4426950408889634


def _cparams(sem):
    return pltpu.CompilerParams(dimension_semantics=sem,
                                vmem_limit_bytes=VMEM_LIMIT_BYTES)


def _dot(a, b):
    return jnp.dot(a, b, preferred_element_type=F32)


def _ada_kernel(c_ref, w_ref, b_ref, o_ref):
    c = c_ref[...]
    c_act = c * jax.nn.sigmoid(c)
    c_hi = c_act.astype(BF16)
    c_lo = (c_act - c_hi.astype(F32)).astype(BF16)
    w = w_ref[0]
    w_hi = w.astype(BF16)
    w_lo = (w - w_hi.astype(F32)).astype(BF16)
    acc = _dot(c_hi, w_hi) + (_dot(c_hi, w_lo) + _dot(c_lo, w_hi))
    o_ref[0] = acc + b_ref[0]


def _ada_mod(c, ada_w, ada_b):
    depth, d, n = ada_w.shape
    b = c.shape[0]
    return pl.pallas_call(
        _ada_kernel,
        grid=(depth, n // ADA_TN),
        in_specs=[
            pl.BlockSpec((b, d), lambda l, j: (0, 0)),
            pl.BlockSpec((1, d, ADA_TN), lambda l, j: (l, 0, j)),
            pl.BlockSpec((1, 1, ADA_TN), lambda l, j: (l, 0, j)),
        ],
        out_specs=pl.BlockSpec((1, b, ADA_TN), lambda l, j: (l, 0, j)),
        out_shape=jax.ShapeDtypeStruct((depth, b, n), F32),
        compiler_params=_cparams(("parallel", "parallel")),
        name="ada_mod",
    )(c, ada_w, ada_b.reshape(depth, 1, n))


def _norm_mod(xf, g, sh, sc):
    inv = lax.rsqrt(jnp.mean(xf * xf, axis=-1, keepdims=True) + EPS)
    return xf * inv * (g * (1.0 + sc)) + sh


def _in_proj_kernel(x_ref, sh_ref, sc_ref, g_ref, w_ref, gq_ref, gk_ref, ones_ref,
                    q_ref, k_ref, v_ref, u_ref, *, aw):
    hb = _norm_mod(x_ref[0], g_ref[...], sh_ref[0], sc_ref[0]).astype(BF16)
    n_pair = aw // LANES

    def head_norm(p, g):
        ms = _dot((p * p).astype(BF16), ones_ref[...]) * (1.0 / HEAD_DIM)
        return p * lax.rsqrt(ms + EPS) * g

    q = head_norm(_dot(hb, w_ref[:, 0:aw]), gq_ref[...]).astype(BF16)
    for hp in range(n_pair):
        q_ref[0, hp] = q[:, hp * LANES:(hp + 1) * LANES]
    k = head_norm(_dot(hb, w_ref[:, aw:2 * aw]), gk_ref[...]).astype(BF16)
    for hp in range(n_pair):
        k_ref[0, hp] = k[:, hp * LANES:(hp + 1) * LANES]
    v = _dot(hb, w_ref[:, 2 * aw:3 * aw]).astype(BF16)
    for hp in range(n_pair):
        v_ref[0, hp] = v[:, hp * LANES:(hp + 1) * LANES]
    u_ref[...] = _dot(hb, w_ref[:, 3 * aw:]).astype(BF16)


def _in_proj(x, sh, sc, g, w_in_bf, gq, gk, ones_blk, aw, sw):
    b, l, d = x.shape
    tm = min(PROJ_TM, l)
    n_pair = aw // LANES
    qkv_shape = jax.ShapeDtypeStruct((b, n_pair, l, LANES), BF16)
    qkv_spec = pl.BlockSpec((1, n_pair, tm, LANES), lambda bi, i: (bi, 0, i, 0))
    vec = lambda n: pl.BlockSpec((1, n), lambda bi, i: (0, 0))
    bvec = pl.BlockSpec((1, 1, d), lambda bi, i: (bi, 0, 0))
    return pl.pallas_call(
        functools.partial(_in_proj_kernel, aw=aw),
        grid=(b, l // tm),
        in_specs=[
            pl.BlockSpec((1, tm, d), lambda bi, i: (bi, i, 0)),
            bvec, bvec, vec(d),
            pl.BlockSpec(w_in_bf.shape, lambda bi, i: (0, 0)),
            vec(aw), vec(aw),
            pl.BlockSpec((aw, aw), lambda bi, i: (0, 0)),
        ],
        out_specs=[qkv_spec, qkv_spec, qkv_spec,
                   pl.BlockSpec((tm, sw), lambda bi, i: (i, bi))],
        out_shape=[qkv_shape, qkv_shape, qkv_shape,
                   jax.ShapeDtypeStruct((l, b * sw), BF16)],
        compiler_params=_cparams(("parallel", "parallel")),
        name="in_proj",
    )(x, sh, sc, g, w_in_bf, gq, gk, ones_blk)


def _attn_kernel(q_ref, k_ref, v_ref, o_ref, acc_ref, carry_ref, next_ref, bound_ref):
    t = ATT_T
    n_sub = q_ref.shape[2] // t
    base = pl.program_id(2) * n_sub
    lane = lax.broadcasted_iota(jnp.int32, (1, LANES), 1)
    first = lane < HEAD_DIM
    row = lax.broadcasted_iota(jnp.int32, (t, t), 0)
    col = lax.broadcasted_iota(jnp.int32, (t, t), 1)
    later = jnp.where(row > col, 1.0, 0.0).astype(BF16)
    later2 = jnp.concatenate([later, later], axis=0)
    strict = col < row

    def split_heads(ref, j):
        blk = ref[0, 0, pl.ds(pl.multiple_of(j * t, t), t), :]
        zero = jnp.zeros_like(blk)
        return jnp.concatenate([jnp.where(first, blk, zero), jnp.where(first, zero, blk)], axis=0)

    def blocks(items, carry_in=None):
        zz = [lax.dot_general(q2, split_heads(k_ref, j), (((1,), (1,)), ((), ())),
                              preferred_element_type=F32) for q2, j, _ in items]
        logb, hilo, sums = [], [], []
        for (_, _, depth), zz_i in zip(items, zz):
            for h in range(2):
                z = zz_i[:, h * t:(h + 1) * t]
                s = jnp.maximum(z, 0.0) + jnp.log2(1.0 + jnp.exp2(-jnp.abs(z)))
                sm = jnp.where(strict, s, 0.0) if depth == 0 else s
                hi = sm.astype(BF16)
                lo = (sm - hi.astype(F32)).astype(BF16)
                logb.append(z - s)
                hilo.append(jnp.concatenate([hi, lo], axis=1))
                sums.append(jnp.sum(sm, axis=-1, keepdims=True))
        within = [_dot(x, later2) for x in hilo]
        ws, carries = [], []
        for n, (_, _, depth) in enumerate(items):
            carry = carries[-1] if depth > 0 else carry_in
            w_heads, after = [], []
            for h in range(2):
                arg = logb[2 * n + h] - within[2 * n + h]
                if carry is not None:
                    arg = arg - carry[h]
                w = jnp.exp2(arg)
                if depth == 0:
                    w = jnp.where(strict, w, 0.0)
                w_heads.append(w.astype(BF16))
                after.append(sums[2 * n + h] if carry is None else carry[h] + sums[2 * n + h])
            ws.append(jnp.concatenate(w_heads, axis=1))
            carries.append(after)
        pvs = [_dot(w, split_heads(v_ref, j)) for w, (_, j, _) in zip(ws, items)]
        return pvs, carries

    def first_blocks(depths):
        items, owner = [], []
        for k in range(n_sub):
            q2 = q_ref[0, 0, k * t:(k + 1) * t, :]
            for d in range(depths[k]):
                items.append((q2, base + k - d, d))
                owner.append(k)
        pvs, carries = blocks(items)
        for k in range(n_sub):
            mine = [n for n, o in enumerate(owner) if o == k]
            acc = pvs[mine[0]]
            for n in mine[1:]:
                acc = acc + pvs[n]
            carry = carries[mine[-1]]
            acc_ref[k] = acc
            for h in range(2):
                carry_ref[k, h] = carry[h]
            next_ref[k] = base + k - depths[k]
            bound_ref[k] = jnp.min(jnp.minimum(carry[0], carry[1]))

    @pl.when(base >= ATT_WINDOW - 1)
    def _():
        first_blocks([ATT_WINDOW] * n_sub)

    @pl.when(base < ATT_WINDOW - 1)
    def _():
        first_blocks([ATT_WINDOW if k >= ATT_WINDOW - 1 else 1 for k in range(n_sub)])

    def finish(k, _):
        rows = pl.ds(pl.multiple_of(k * t, t), t)
        q2 = q_ref[0, 0, rows, :]

        def cond(state):
            j, bound = state
            return jnp.logical_and(j >= 0, bound < UNDERFLOW_BOUND)

        def body(state):
            j, _ = state
            carry = [carry_ref[k, 0], carry_ref[k, 1]]
            (pv,), (carry,) = blocks([(q2, j, -1)], carry)
            acc_ref[k] += pv
            for h in range(2):
                carry_ref[k, h] = carry[h]
            return j - 1, jnp.min(jnp.minimum(carry[0], carry[1]))

        lax.while_loop(cond, body, (next_ref[k], bound_ref[k]))
        o_ref[0, rows, :] = acc_ref[k].astype(o_ref.dtype)
        return 0

    lax.fori_loop(0, n_sub, finish, 0)


def _attn(q, k, v):
    b, n_pair, l, _ = q.shape
    t = ATT_T
    qs = min(ATT_QS, l)
    kv_spec = pl.BlockSpec((1, 1, l, LANES), lambda bi, hp, i: (bi, hp, 0, 0))
    return pl.pallas_call(
        _attn_kernel,
        grid=(b, n_pair, l // qs),
        in_specs=[pl.BlockSpec((1, 1, qs, LANES), lambda bi, hp, i: (bi, hp, i, 0)),
                  kv_spec, kv_spec],
        out_specs=pl.BlockSpec((1, qs, LANES), lambda bi, hp, i: (bi, i, hp)),
        out_shape=jax.ShapeDtypeStruct((b, l, n_pair * LANES), BF16),
        scratch_shapes=[pltpu.VMEM((qs // t, t, LANES), F32),
                        pltpu.VMEM((qs // t, 2, t, 1), F32),
                        pltpu.SMEM((qs // t,), jnp.int32),
                        pltpu.SMEM((qs // t,), F32)],
        compiler_params=_cparams(("parallel", "parallel", "parallel")),
        name="attn",
    )(q, k, v)


def _ssm_kernel(u_ref, wb_ref, are_ref, aim_ref, wcre_ref, wcim_ref, d_ref,
                wglu_ref, bglu_ref, g_ref, o_ref, sre_ref, sim_ref, st_ref, y_ref,
                *, n_oct, batch):
    n_tok = u_ref.shape[0]
    sw = u_ref.shape[1] // batch
    oct_states = sre_ref.shape[1] // n_oct
    half = sre_ref.shape[1] // 2

    @pl.when(pl.program_id(0) == 0)
    def _():
        st_ref[...] = jnp.zeros_like(st_ref)

    for bi in range(batch):
        for o in range(n_oct):
            c0 = bi * sw + o * LANES
            y_ref[o, pl.ds(bi, n_tok, stride=batch), :] = u_ref[:, c0:c0 + LANES].astype(F32)
    for o in range(n_oct):
        bu = _dot(y_ref[o].astype(BF16), wb_ref[o])
        sre_ref[:, o * oct_states:(o + 1) * oct_states] = bu[:, :oct_states]
        sim_ref[:, o * oct_states:(o + 1) * oct_states] = bu[:, oct_states:]

    for part in range(2):
        cols = slice(part * half, (part + 1) * half)
        are = jnp.broadcast_to(are_ref[:, cols], (batch, half))
        aim = jnp.broadcast_to(aim_ref[:, cols], (batch, half))

        def step(t, state, cols=cols, are=are, aim=aim):
            s_re, s_im = state
            r = pl.ds(pl.multiple_of(t * batch, batch), batch)
            n_re = are * s_re - aim * s_im + sre_ref[r, cols]
            n_im = are * s_im + aim * s_re + sim_ref[r, cols]
            sre_ref[r, cols] = n_re
            sim_ref[r, cols] = n_im
            return n_re, n_im

        s_re, s_im = lax.fori_loop(0, n_tok, step,
                                   (st_ref[0, :, cols], st_ref[1, :, cols]),
                                   unroll=4)
        st_ref[0, :, cols] = s_re
        st_ref[1, :, cols] = s_im

    sumsq = None
    for o in range(n_oct):
        cs = slice(o * oct_states, (o + 1) * oct_states)
        lanes = slice(o * LANES, (o + 1) * LANES)
        y = (_dot(sre_ref[:, cs].astype(BF16), wcre_ref[o])
             + _dot(sim_ref[:, cs].astype(BF16), wcim_ref[o])
             + d_ref[:, lanes] * y_ref[o])
        y = jax.nn.gelu(y)
        gate = jax.nn.sigmoid(_dot(y.astype(BF16), wglu_ref[o]) + bglu_ref[:, lanes])
        out = y * gate
        y_ref[o] = out
        sq = jnp.sum(out * out, axis=-1, keepdims=True)
        sumsq = sq if sumsq is None else sumsq + sq
    inv = lax.rsqrt(sumsq * (1.0 / sw) + EPS)
    for o in range(n_oct):
        y_ref[o] = y_ref[o] * inv * g_ref[:, o * LANES:(o + 1) * LANES]
    for bi in range(batch):
        for o in range(n_oct):
            c0 = bi * sw + o * LANES
            o_ref[:, c0:c0 + LANES] = (
                y_ref[o, pl.ds(bi, n_tok, stride=batch), :].astype(o_ref.dtype))


def _ssm(u2, prm, batch):
    l, width = u2.shape
    sw = width // batch
    n_oct = sw // LANES
    n_state = prm["are"].shape[1]
    rows = SSM_T * batch
    full = lambda a: pl.BlockSpec(a.shape, lambda i: (0,) * a.ndim)
    names = ["wb", "are", "aim", "wcre", "wcim", "d", "wglu", "bglu", "g"]
    return pl.pallas_call(
        functools.partial(_ssm_kernel, n_oct=n_oct, batch=batch),
        grid=(l // SSM_T,),
        in_specs=[pl.BlockSpec((SSM_T, width), lambda i: (i, 0))]
                 + [full(prm[n]) for n in names],
        out_specs=pl.BlockSpec((SSM_T, width), lambda i: (i, 0)),
        out_shape=jax.ShapeDtypeStruct((l, width), BF16),
        scratch_shapes=[pltpu.VMEM((rows, n_state), F32),
                        pltpu.VMEM((rows, n_state), F32),
                        pltpu.VMEM((2, batch, n_state), F32),
                        pltpu.VMEM((n_oct, rows, LANES), F32)],
        compiler_params=_cparams(("arbitrary",)),
        name="ssm",
    )(u2, *[prm[n] for n in names])


def _ssm_params(a_re, a_im, log_dt, b_re, b_im, c_re, c_im, d_skip, glu_w, glu_b, out_g):
    g, p, h = b_re.shape
    per_oct = LANES // h
    n_oct = g // per_oct
    dt = jnp.exp(log_dt)[:, None]
    mag = jnp.exp(dt * a_re)
    abar_re = mag * jnp.cos(dt * a_im)
    abar_im = mag * jnp.sin(dt * a_im)
    em_re = abar_re - 1.0
    em_im = abar_im
    den = a_re * a_re + a_im * a_im
    f_re = (em_re * a_re + em_im * a_im) / den
    f_im = (em_im * a_re - em_re * a_im) / den
    bb_re = f_re[..., None] * b_re - f_im[..., None] * b_im
    bb_im = f_re[..., None] * b_im + f_im[..., None] * b_re
    eye = jnp.eye(per_oct, dtype=F32)

    def in_blockdiag(bb):
        bb = bb.reshape(n_oct, per_oct, p, h)
        m = jnp.einsum("ogph,gk->oghkp", bb, eye)
        return m.reshape(n_oct, per_oct * h, per_oct * p)

    def out_blockdiag(c):
        c = c.reshape(n_oct, per_oct, h, p)
        m = jnp.einsum("oghp,gk->ogpkh", c, eye)
        return m.reshape(n_oct, per_oct * p, per_oct * h)

    wb = jnp.concatenate([in_blockdiag(bb_re), in_blockdiag(bb_im)], axis=-1).astype(BF16)
    wglu = jnp.einsum("oghk,gf->oghfk", glu_w.reshape(n_oct, per_oct, h, h), eye)
    wglu = wglu.reshape(n_oct, per_oct * h, per_oct * h).astype(BF16)
    return {
        "wb": wb,
        "are": abar_re.reshape(1, g * p), "aim": abar_im.reshape(1, g * p),
        "wcre": out_blockdiag(c_re).astype(BF16),
        "wcim": out_blockdiag(-c_im).astype(BF16),
        "d": d_skip.reshape(1, g * h), "wglu": wglu,
        "bglu": glu_b.reshape(1, g * h), "g": out_g.reshape(1, g * h),
    }


def _mix_ffn_kernel(x_ref, xh_ref, oa_ref, oah_ref, os_ref, osh_ref, ga_ref, wa_ref, ws_ref,
                    g1_ref, sh_ref, sc_ref, g2_ref, g_ref, wup_ref, cw_ref, cb_ref, wdn_ref,
                    o_ref, perm_ref, h_ref, up_ref, a_ref, *, d_ff):
    tm, d = x_ref.shape[1], x_ref.shape[2]
    n_grp = tm // SUBLANES
    n_slab = d // LANES
    n_chunk = d_ff // FFN_CK
    g, sh, sc = g_ref[...], sh_ref[0], sc_ref[0]

    def natural_rows(s):
        return slice(s * n_grp, (s + 1) * n_grp)

    def interleaved_rows(s):
        return pl.ds(s, n_grp, stride=SUBLANES)

    oa = jnp.concatenate([oa_ref[0], oah_ref[0]], axis=0).astype(F32)
    inv = lax.rsqrt(jnp.mean(oa * oa, axis=-1, keepdims=True) + EPS)
    mixed = (_dot((oa * inv * ga_ref[...]).astype(BF16), wa_ref[...])
             + _dot(jnp.concatenate([os_ref[...], osh_ref[...]], axis=0), ws_ref[...]))
    o_ref[0] = x_ref[0] + g1_ref[0] * mixed[:tm]
    x1_halo = xh_ref[0] + g1_ref[0] * mixed[tm:]

    hn = _norm_mod(o_ref[0], g, sh, sc)
    for slab in range(n_slab):
        lanes = slice(slab * LANES, (slab + 1) * LANES)
        for s in range(SUBLANES):
            perm_ref[slab, interleaved_rows(s), :] = hn[natural_rows(s), lanes]
    for slab in range(n_slab):
        h_ref[:tm, slab * LANES:(slab + 1) * LANES] = perm_ref[slab].astype(BF16)
    keep = (pl.program_id(1) > 0).astype(F32)
    h_ref[tm:, :] = (_norm_mod(x1_halo, g, sh, sc) * keep).astype(BF16)
    sublane = lax.broadcasted_iota(jnp.int32, (SUBLANES, FFN_CK), 0)

    def chunk(start):
        if not isinstance(start, int):
            start = pl.multiple_of(start, FFN_CK)
        return pl.ds(start, FFN_CK)

    def cols(c, part):
        return chunk(part * d_ff + c * FFN_CK)

    def wrapped(group, before):
        return jnp.where(sublane == 0, before, pltpu.roll(group, 1, axis=0))

    def up_proj(c, slot):
        for part in range(2):
            r = _dot(h_ref[...], wup_ref[:, cols(c, part)])
            pad = (CONV_WIDTH - 1) * SUBLANES
            up_ref[slot, part, pad:, :] = r[:tm]
            for back in range(1, CONV_WIDTH):
                up_ref[slot, part, pad - back * SUBLANES:pad - (back - 1) * SUBLANES, :] = wrapped(
                    r[tm - back * SUBLANES:tm - (back - 1) * SUBLANES],
                    r[tm + FFN_HALO - back:tm + FFN_HALO - back + 1])

    def conv(c, slot, part):
        out = cb_ref[:, cols(c, part)]
        for i in range(CONV_WIDTH):
            off = i * SUBLANES
            out = out + up_ref[slot, part, off:off + tm, :] * cw_ref[i:i + 1, cols(c, part)]
        return out

    def activate(c, slot):
        a_ref[:, chunk(c * FFN_CK)] = (
            jax.nn.gelu(conv(c, slot, 1)) * conv(c, slot, 0)).astype(BF16)

    for c in range(n_chunk):
        up_proj(c, c % 2)
        activate(c, c % 2)
    y = _dot(a_ref[...], wdn_ref[...])
    for slab in range(n_slab):
        perm_ref[slab] = y[:, slab * LANES:(slab + 1) * LANES]
    for slab in range(n_slab):
        lanes = slice(slab * LANES, (slab + 1) * LANES)
        for s in range(SUBLANES):
            o_ref[0, natural_rows(s), lanes] += (
                g2_ref[0, :, lanes] * perm_ref[slab, interleaved_rows(s), :])


def _mix_ffn(x, o_attn, o_ssm2, ga, wa, ws, g1, sh, sc, g2, g, wup, cw, cb, wdn):
    b, l, d = x.shape
    aw = o_attn.shape[-1]
    sw = o_ssm2.shape[-1] // b
    d_ff = wdn.shape[0]
    tm = min(FFN_TM, l)
    halo_per_tile = tm // FFN_HALO
    halo = lambda i: jnp.maximum(i * halo_per_tile - 1, 0)
    bvec = pl.BlockSpec((1, 1, d), lambda bi, i: (bi, 0, 0))
    const = lambda a: pl.BlockSpec(a.shape, lambda bi, i: (0,) * a.ndim,
                                   pipeline_mode=pl.Buffered(1))
    return pl.pallas_call(
        functools.partial(_mix_ffn_kernel, d_ff=d_ff),
        grid=(b, l // tm),
        in_specs=[
            pl.BlockSpec((1, tm, d), lambda bi, i: (bi, i, 0)),
            pl.BlockSpec((1, FFN_HALO, d), lambda bi, i: (bi, halo(i), 0)),
            pl.BlockSpec((1, tm, aw), lambda bi, i: (bi, i, 0)),
            pl.BlockSpec((1, FFN_HALO, aw), lambda bi, i: (bi, halo(i), 0)),
            pl.BlockSpec((tm, sw), lambda bi, i: (i, bi)),
            pl.BlockSpec((FFN_HALO, sw), lambda bi, i: (halo(i), bi)),
            const(ga), const(wa), const(ws), bvec,
            bvec, bvec, bvec, const(g), const(wup), const(cw), const(cb), const(wdn),
        ],
        out_specs=pl.BlockSpec((1, tm, d), lambda bi, i: (bi, i, 0)),
        out_shape=jax.ShapeDtypeStruct(x.shape, F32),
        scratch_shapes=[pltpu.VMEM((d // LANES, tm, LANES), F32),
                        pltpu.VMEM((tm + FFN_HALO, d), BF16),
                        pltpu.VMEM((2, 2, tm + (CONV_WIDTH - 1) * SUBLANES, FFN_CK), F32),
                        pltpu.VMEM((tm, d_ff), BF16)],
        compiler_params=_cparams(("parallel", "parallel")),
        name="mix_ffn",
    )(x, x, o_attn, o_attn, o_ssm2, o_ssm2, ga, wa, ws, g1, sh, sc, g2, g, wup, cw, cb, wdn)


def kernel(x, c, ada_w, ada_b, norm1_g, w_in, q_norm_g, k_norm_g, ssm_a_re, ssm_a_im,
           ssm_log_dt, ssm_b_re, ssm_b_im, ssm_c_re, ssm_c_im, ssm_d, glu_w, glu_b,
           attn_out_g, ssm_out_g, w_out, norm2_g, ffn_w_up, ffn_conv_w, ffn_conv_b,
           ffn_w_down):
    b, l, d = x.shape
    depth = ada_w.shape[0]
    aw = attn_out_g.shape[-1]
    sw = ssm_out_g.shape[-1]
    n_heads = aw // HEAD_DIM
    assert b == SUBLANES and aw % LANES == 0 and sw % LANES == 0
    assert l % ATT_T == 0 and l % SSM_T == 0

    mod = _ada_mod(c, ada_w, ada_b)
    head = jnp.arange(aw) // HEAD_DIM
    ones_blk = (head[:, None] == head[None, :]).astype(BF16)
    inv_sqrt = LOG2_E / math.sqrt(HEAD_DIM)

    for li in range(depth):
        m = mod[li].reshape(b, N_MOD, 1, d)
        sh1, sc1, g1, sh2, sc2, g2 = (m[:, i] for i in range(N_MOD))
        gq = jnp.tile(q_norm_g[li] * inv_sqrt, n_heads).reshape(1, aw)
        gk = jnp.tile(k_norm_g[li], n_heads).reshape(1, aw)
        q, k, v, u2 = _in_proj(x, sh1, sc1, norm1_g[li].reshape(1, d),
                               w_in[li].astype(BF16), gq, gk, ones_blk, aw, sw)
        o_attn = _attn(q, k, v)
        prm = _ssm_params(ssm_a_re[li], ssm_a_im[li], ssm_log_dt[li], ssm_b_re[li],
                          ssm_b_im[li], ssm_c_re[li], ssm_c_im[li], ssm_d[li],
                          glu_w[li], glu_b[li], ssm_out_g[li])
        o_ssm = _ssm(u2, prm, b)
        w_o = w_out[li].astype(BF16)
        x = _mix_ffn(x, o_attn, o_ssm, attn_out_g[li].reshape(1, aw), w_o[:aw], w_o[aw:], g1,
                     sh2, sc2, g2, norm2_g[li].reshape(1, d), ffn_w_up[li].astype(BF16),
                     ffn_conv_w[li], ffn_conv_b[li].reshape(1, -1),
                     ffn_w_down[li].astype(BF16))
    return x
```

```python
import functools
import math

import jax
import jax.numpy as jnp
from jax import lax
from jax.experimental import pallas as pl
from jax.experimental.pallas import tpu as pltpu

F32 = jnp.float32
BF16 = jnp.bfloat16

EPS = 1e-6
HEAD_DIM = 64
SSM_GROUP = 16
SSM_STATE = 64
CONV_WIDTH = 3
N_MOD = 6

LANES = 128
SUBLANES = 8
VMEM_LIMIT_BYTES = 56 * 1024 * 1024

ADA_TN = 1536
PROJ_TM = 512
ATT_T = 128
ATT_QS = 512
ATT_WINDOW = 3
SSM_T = 64
FFN_TM = 512
FFN_CK = 256
FFN_HALO = 16

UNDERFLOW_BOUND = 152.0
LOG2_E = 1.4426950408889634


def _cparams(sem):
    return pltpu.CompilerParams(dimension_semantics=sem,
                                vmem_limit_bytes=VMEM_LIMIT_BYTES)


def _dot(a, b):
    return jnp.dot(a, b, preferred_element_type=F32)


def _ada_kernel(c_ref, w_ref, b_ref, o_ref):
    c = c_ref[...]
    c_act = c * jax.nn.sigmoid(c)
    c_hi = c_act.astype(BF16)
    c_lo = (c_act - c_hi.astype(F32)).astype(BF16)
    w = w_ref[0]
    w_hi = w.astype(BF16)
    w_lo = (w - w_hi.astype(F32)).astype(BF16)
    acc = _dot(c_hi, w_hi) + (_dot(c_hi, w_lo) + _dot(c_lo, w_hi))
    o_ref[0] = acc + b_ref[0]


def _ada_mod(c, ada_w, ada_b):
    depth, d, n = ada_w.shape
    b = c.shape[0]
    return pl.pallas_call(
        _ada_kernel,
        grid=(depth, n // ADA_TN),
        in_specs=[
            pl.BlockSpec((b, d), lambda l, j: (0, 0)),
            pl.BlockSpec((1, d, ADA_TN), lambda l, j: (l, 0, j)),
            pl.BlockSpec((1, 1, ADA_TN), lambda l, j: (l, 0, j)),
        ],
        out_specs=pl.BlockSpec((1, b, ADA_TN), lambda l, j: (l, 0, j)),
        out_shape=jax.ShapeDtypeStruct((depth, b, n), F32),
        compiler_params=_cparams(("parallel", "parallel")),
        name="ada_mod",
    )(c, ada_w, ada_b.reshape(depth, 1, n))


def _norm_mod(xf, g, sh, sc):
    inv = lax.rsqrt(jnp.mean(xf * xf, axis=-1, keepdims=True) + EPS)
    return xf * inv * (g * (1.0 + sc)) + sh


def _in_proj_kernel(x_ref, sh_ref, sc_ref, g_ref, w_ref, gq_ref, gk_ref, ones_ref,
                    q_ref, k_ref, v_ref, u_ref, *, aw):
    hb = _norm_mod(x_ref[0], g_ref[...], sh_ref[0], sc_ref[0]).astype(BF16)
    n_pair = aw // LANES

    def head_norm(p, g):
        ms = _dot((p * p).astype(BF16), ones_ref[...]) * (1.0 / HEAD_DIM)
        return p * lax.rsqrt(ms + EPS) * g

    q = head_norm(_dot(hb, w_ref[:, 0:aw]), gq_ref[...]).astype(BF16)
    for hp in range(n_pair):
        q_ref[0, hp] = q[:, hp * LANES:(hp + 1) * LANES]
    k = head_norm(_dot(hb, w_ref[:, aw:2 * aw]), gk_ref[...]).astype(BF16)
    for hp in range(n_pair):
        k_ref[0, hp] = k[:, hp * LANES:(hp + 1) * LANES]
    v = _dot(hb, w_ref[:, 2 * aw:3 * aw]).astype(BF16)
    for hp in range(n_pair):
        v_ref[0, hp] = v[:, hp * LANES:(hp + 1) * LANES]
    u_ref[...] = _dot(hb, w_ref[:, 3 * aw:]).astype(BF16)


def _in_proj(x, sh, sc, g, w_in_bf, gq, gk, ones_blk, aw, sw):
    b, l, d = x.shape
    tm = min(PROJ_TM, l)
    n_pair = aw // LANES
    qkv_shape = jax.ShapeDtypeStruct((b, n_pair, l, LANES), BF16)
    qkv_spec = pl.BlockSpec((1, n_pair, tm, LANES), lambda bi, i: (bi, 0, i, 0))
    vec = lambda n: pl.BlockSpec((1, n), lambda bi, i: (0, 0))
    bvec = pl.BlockSpec((1, 1, d), lambda bi, i: (bi, 0, 0))
    return pl.pallas_call(
        functools.partial(_in_proj_kernel, aw=aw),
        grid=(b, l // tm),
        in_specs=[
            pl.BlockSpec((1, tm, d), lambda bi, i: (bi, i, 0)),
            bvec, bvec, vec(d),
            pl.BlockSpec(w_in_bf.shape, lambda bi, i: (0, 0)),
            vec(aw), vec(aw),
            pl.BlockSpec((aw, aw), lambda bi, i: (0, 0)),
        ],
        out_specs=[qkv_spec, qkv_spec, qkv_spec,
                   pl.BlockSpec((tm, sw), lambda bi, i: (i, bi))],
        out_shape=[qkv_shape, qkv_shape, qkv_shape,
                   jax.ShapeDtypeStruct((l, b * sw), BF16)],
        compiler_params=_cparams(("parallel", "parallel")),
        name="in_proj",
    )(x, sh, sc, g, w_in_bf, gq, gk, ones_blk)


def _attn_kernel(q_ref, k_ref, v_ref, o_ref, acc_ref, carry_ref, next_ref, bound_ref):
    t = ATT_T
    n_sub = q_ref.shape[2] // t
    base = pl.program_id(2) * n_sub
    lane = lax.broadcasted_iota(jnp.int32, (1, LANES), 1)
    first = lane < HEAD_DIM
    row = lax.broadcasted_iota(jnp.int32, (t, t), 0)
    col = lax.broadcasted_iota(jnp.int32, (t, t), 1)
    later = jnp.where(row > col, 1.0, 0.0).astype(BF16)
    strict = col < row

    def split_heads(ref, j):
        blk = ref[0, 0, pl.ds(pl.multiple_of(j * t, t), t), :]
        zero = jnp.zeros_like(blk)
        return jnp.where(first, blk, zero), jnp.where(first, zero, blk)

    def blocks(items, carry_in=None):
        zz = [lax.dot_general(q2, jnp.concatenate(split_heads(k_ref, j), axis=0),
                              (((1,), (1,)), ((), ())),
                              preferred_element_type=F32) for q2, j, _ in items]
        logb, hilo, sums = [], [], []
        for (_, _, depth), zz_i in zip(items, zz):
            for h in range(2):
                z = zz_i[:, h * t:(h + 1) * t]
                neg_abs = lax.bitcast_convert_type(
                    lax.bitcast_convert_type(z, jnp.uint32) | jnp.uint32(0x80000000), F32)
                s = jnp.maximum(z, 0.0) + jnp.log2(1.0 + jnp.exp2(neg_abs))
                sm = jnp.where(strict, s, 0.0) if depth == 0 else s
                hi = sm.astype(BF16)
                lo = (sm - hi.astype(F32)).astype(BF16)
                logb.append(z - s)
                hilo.append((hi, lo))
                sums.append(jnp.sum(sm, axis=-1, keepdims=True))
        within = [_dot(hi, later) + _dot(lo, later) for hi, lo in hilo]
        ws, carries = [], []
        for n, (_, _, depth) in enumerate(items):
            carry = carries[-1] if depth > 0 else carry_in
            w_heads, after = [], []
            for h in range(2):
                arg = logb[2 * n + h] - within[2 * n + h]
                if carry is not None:
                    arg = arg - carry[h]
                w = jnp.exp2(arg)
                if depth == 0:
                    w = jnp.where(strict, w, 0.0)
                w_heads.append(w.astype(BF16))
                after.append(sums[2 * n + h] if carry is None else carry[h] + sums[2 * n + h])
            ws.append(w_heads)
            carries.append(after)
        pvs = []
        for (w0, w1), (_, j, _) in zip(ws, items):
            v0, v1 = split_heads(v_ref, j)
            pvs.append(_dot(w0, v0) + _dot(w1, v1))
        return pvs, carries

    def first_blocks(depths):
        items, owner = [], []
        for k in range(n_sub):
            q2 = q_ref[0, 0, k * t:(k + 1) * t, :]
            for d in range(depths[k]):
                items.append((q2, base + k - d, d))
                owner.append(k)
        pvs, carries = blocks(items)
        for k in range(n_sub):
            mine = [n for n, o in enumerate(owner) if o == k]
            acc = pvs[mine[0]]
            for n in mine[1:]:
                acc = acc + pvs[n]
            carry = carries[mine[-1]]
            acc_ref[k] = acc
            for h in range(2):
                carry_ref[k, h] = carry[h]
            next_ref[k] = base + k - depths[k]
            bound_ref[k] = jnp.min(jnp.minimum(carry[0], carry[1]))

    @pl.when(base >= ATT_WINDOW - 1)
    def _():
        first_blocks([ATT_WINDOW] * n_sub)

    @pl.when(base < ATT_WINDOW - 1)
    def _():
        first_blocks([ATT_WINDOW if k >= ATT_WINDOW - 1 else 1 for k in range(n_sub)])

    def finish(k, _):
        rows = pl.ds(pl.multiple_of(k * t, t), t)
        q2 = q_ref[0, 0, rows, :]

        def cond(state):
            j, bound = state
            return jnp.logical_and(j >= 0, bound < UNDERFLOW_BOUND)

        def body(state):
            j, _ = state
            carry = [carry_ref[k, 0], carry_ref[k, 1]]
            (pv,), (carry,) = blocks([(q2, j, -1)], carry)
            acc_ref[k] += pv
            for h in range(2):
                carry_ref[k, h] = carry[h]
            return j - 1, jnp.min(jnp.minimum(carry[0], carry[1]))

        lax.while_loop(cond, body, (next_ref[k], bound_ref[k]))
        o_ref[0, rows, :] = acc_ref[k].astype(o_ref.dtype)
        return 0

    lax.fori_loop(0, n_sub, finish, 0)


def _attn(q, k, v):
    b, n_pair, l, _ = q.shape
    t = ATT_T
    qs = min(ATT_QS, l)
    kv_spec = pl.BlockSpec((1, 1, l, LANES), lambda bi, hp, i: (bi, hp, 0, 0))
    return pl.pallas_call(
        _attn_kernel,
        grid=(b, n_pair, l // qs),
        in_specs=[pl.BlockSpec((1, 1, qs, LANES), lambda bi, hp, i: (bi, hp, i, 0)),
                  kv_spec, kv_spec],
        out_specs=pl.BlockSpec((1, qs, LANES), lambda bi, hp, i: (bi, i, hp)),
        out_shape=jax.ShapeDtypeStruct((b, l, n_pair * LANES), BF16),
        scratch_shapes=[pltpu.VMEM((qs // t, t, LANES), F32),
                        pltpu.VMEM((qs // t, 2, t, 1), F32),
                        pltpu.SMEM((qs // t,), jnp.int32),
                        pltpu.SMEM((qs // t,), F32)],
        compiler_params=_cparams(("parallel", "parallel", "parallel")),
        name="attn",
    )(q, k, v)


def _ssm_kernel(u_ref, wb_ref, are_ref, aim_ref, wcre_ref, wcim_ref, d_ref,
                wglu_ref, bglu_ref, g_ref, o_ref, sre_ref, sim_ref, st_ref, y_ref,
                *, n_oct, batch):
    n_tok = u_ref.shape[0]
    sw = u_ref.shape[1] // batch
    oct_states = sre_ref.shape[2]

    @pl.when(pl.program_id(0) == 0)
    def _():
        st_ref[...] = jnp.zeros_like(st_ref)

    for bi in range(batch):
        for o in range(n_oct):
            c0 = bi * sw + o * LANES
            y_ref[o, pl.ds(bi, n_tok, stride=batch), :] = u_ref[:, c0:c0 + LANES].astype(F32)
    def input_proj(o):
        bu = _dot(y_ref[o].astype(BF16), wb_ref[o])
        sre_ref[o] = bu[:, :oct_states]
        sim_ref[o] = bu[:, oct_states:]

    def recurrence(o):
        cols = slice(o * oct_states, (o + 1) * oct_states)
        are = jnp.broadcast_to(are_ref[:, cols], (batch, oct_states))
        aim = jnp.broadcast_to(aim_ref[:, cols], (batch, oct_states))
        s_re, s_im = st_ref[0, :, cols], st_ref[1, :, cols]
        for t in range(n_tok):
            r = slice(t * batch, (t + 1) * batch)
            s_re, s_im = (are * s_re - aim * s_im + sre_ref[o, r, :],
                          are * s_im + aim * s_re + sim_ref[o, r, :])
            sre_ref[o, r, :] = s_re
            sim_ref[o, r, :] = s_im
        st_ref[0, :, cols] = s_re
        st_ref[1, :, cols] = s_im

    def output_proj(o):
        lanes = slice(o * LANES, (o + 1) * LANES)
        y = (_dot(sre_ref[o].astype(BF16), wcre_ref[o])
             + _dot(sim_ref[o].astype(BF16), wcim_ref[o])
             + d_ref[:, lanes] * y_ref[o])
        y = jax.nn.gelu(y)
        gate = jax.nn.sigmoid(_dot(y.astype(BF16), wglu_ref[o]) + bglu_ref[:, lanes])
        out = y * gate
        y_ref[o] = out
        return jnp.sum(out * out, axis=-1, keepdims=True)

    sumsq = 0.0
    input_proj(0)
    for o in range(n_oct):
        if o + 1 < n_oct:
            input_proj(o + 1)
        recurrence(o)
        if o >= 1:
            sumsq = sumsq + output_proj(o - 1)
    sumsq = sumsq + output_proj(n_oct - 1)
    inv = lax.rsqrt(sumsq * (1.0 / sw) + EPS)
    for o in range(n_oct):
        y_ref[o] = y_ref[o] * inv * g_ref[:, o * LANES:(o + 1) * LANES]
    for bi in range(batch):
        for o in range(n_oct):
            c0 = bi * sw + o * LANES
            o_ref[:, c0:c0 + LANES] = (
                y_ref[o, pl.ds(bi, n_tok, stride=batch), :].astype(o_ref.dtype))


def _ssm(u2, prm, batch):
    l, width = u2.shape
    sw = width // batch
    n_oct = sw // LANES
    n_state = prm["are"].shape[1]
    rows = SSM_T * batch
    full = lambda a: pl.BlockSpec(a.shape, lambda i: (0,) * a.ndim)
    names = ["wb", "are", "aim", "wcre", "wcim", "d", "wglu", "bglu", "g"]
    return pl.pallas_call(
        functools.partial(_ssm_kernel, n_oct=n_oct, batch=batch),
        grid=(l // SSM_T,),
        in_specs=[pl.BlockSpec((SSM_T, width), lambda i: (i, 0))]
                 + [full(prm[n]) for n in names],
        out_specs=pl.BlockSpec((SSM_T, width), lambda i: (i, 0)),
        out_shape=jax.ShapeDtypeStruct((l, width), BF16),
        scratch_shapes=[pltpu.VMEM((n_oct, rows, n_state // n_oct), F32),
                        pltpu.VMEM((n_oct, rows, n_state // n_oct), F32),
                        pltpu.VMEM((2, batch, n_state), F32),
                        pltpu.VMEM((n_oct, rows, LANES), F32)],
        compiler_params=_cparams(("arbitrary",)),
        name="ssm",
    )(u2, *[prm[n] for n in names])


def _ssm_params(a_re, a_im, log_dt, b_re, b_im, c_re, c_im, d_skip, glu_w, glu_b, out_g):
    g, p, h = b_re.shape
    per_oct = LANES // h
    n_oct = g // per_oct
    dt = jnp.exp(log_dt)[:, None]
    mag = jnp.exp(dt * a_re)
    abar_re = mag * jnp.cos(dt * a_im)
    abar_im = mag * jnp.sin(dt * a_im)
    em_re = abar_re - 1.0
    em_im = abar_im
    den = a_re * a_re + a_im * a_im
    f_re = (em_re * a_re + em_im * a_im) / den
    f_im = (em_im * a_re - em_re * a_im) / den
    bb_re = f_re[..., None] * b_re - f_im[..., None] * b_im
    bb_im = f_re[..., None] * b_im + f_im[..., None] * b_re
    eye = jnp.eye(per_oct, dtype=F32)

    def in_blockdiag(bb):
        bb = bb.reshape(n_oct, per_oct, p, h)
        m = jnp.einsum("ogph,gk->oghkp", bb, eye)
        return m.reshape(n_oct, per_oct * h, per_oct * p)

    def out_blockdiag(c):
        c = c.reshape(n_oct, per_oct, h, p)
        m = jnp.einsum("oghp,gk->ogpkh", c, eye)
        return m.reshape(n_oct, per_oct * p, per_oct * h)

    wb = jnp.concatenate([in_blockdiag(bb_re), in_blockdiag(bb_im)], axis=-1).astype(BF16)
    wglu = jnp.einsum("oghk,gf->oghfk", glu_w.reshape(n_oct, per_oct, h, h), eye)
    wglu = wglu.reshape(n_oct, per_oct * h, per_oct * h).astype(BF16)
    return {
        "wb": wb,
        "are": abar_re.reshape(1, g * p), "aim": abar_im.reshape(1, g * p),
        "wcre": out_blockdiag(c_re).astype(BF16),
        "wcim": out_blockdiag(-c_im).astype(BF16),
        "d": d_skip.reshape(1, g * h), "wglu": wglu,
        "bglu": glu_b.reshape(1, g * h), "g": out_g.reshape(1, g * h),
    }


def _mix_ffn_kernel(x_ref, xh_ref, oa_ref, oah_ref, os_ref, osh_ref, ga_ref, wa_ref, ws_ref,
                    g1_ref, sh_ref, sc_ref, g2_ref, g_ref, wup_ref, cw_ref, cb_ref, wdn_ref,
                    o_ref, perm_ref, h_ref, up_ref, a_ref, *, d_ff):
    tm, d = x_ref.shape[1], x_ref.shape[2]
    n_grp = tm // SUBLANES
    n_slab = d // LANES
    n_chunk = d_ff // FFN_CK
    g, sh, sc = g_ref[...], sh_ref[0], sc_ref[0]

    def natural_rows(s):
        return slice(s * n_grp, (s + 1) * n_grp)

    def interleaved_rows(s):
        return pl.ds(s, n_grp, stride=SUBLANES)

    oa = jnp.concatenate([oa_ref[0], oah_ref[0]], axis=0).astype(F32)
    inv = lax.rsqrt(jnp.mean(oa * oa, axis=-1, keepdims=True) + EPS)
    mixed = (_dot((oa * inv * ga_ref[...]).astype(BF16), wa_ref[...])
             + _dot(jnp.concatenate([os_ref[...], osh_ref[...]], axis=0), ws_ref[...]))
    o_ref[0] = x_ref[0] + g1_ref[0] * mixed[:tm]
    x1_halo = xh_ref[0] + g1_ref[0] * mixed[tm:]

    hn = _norm_mod(o_ref[0], g, sh, sc)
    for slab in range(n_slab):
        lanes = slice(slab * LANES, (slab + 1) * LANES)
        for s in range(SUBLANES):
            perm_ref[slab, interleaved_rows(s), :] = hn[natural_rows(s), lanes]
    for slab in range(n_slab):
        h_ref[:tm, slab * LANES:(slab + 1) * LANES] = perm_ref[slab].astype(BF16)
    keep = (pl.program_id(1) > 0).astype(F32)
    h_ref[tm:, :] = (_norm_mod(x1_halo, g, sh, sc) * keep).astype(BF16)
    sublane = lax.broadcasted_iota(jnp.int32, (SUBLANES, FFN_CK), 0)

    def chunk(start):
        if not isinstance(start, int):
            start = pl.multiple_of(start, FFN_CK)
        return pl.ds(start, FFN_CK)

    def cols(c, part):
        return chunk(part * d_ff + c * FFN_CK)

    def wrapped(group, before):
        return jnp.where(sublane == 0, before, pltpu.roll(group, 1, axis=0))

    def up_proj(c, slot):
        for part in range(2):
            r = _dot(h_ref[...], wup_ref[:, cols(c, part)])
            pad = (CONV_WIDTH - 1) * SUBLANES
            up_ref[slot, part, pad:, :] = r[:tm]
            for back in range(1, CONV_WIDTH):
                up_ref[slot, part, pad - back * SUBLANES:pad - (back - 1) * SUBLANES, :] = wrapped(
                    r[tm - back * SUBLANES:tm - (back - 1) * SUBLANES],
                    r[tm + FFN_HALO - back:tm + FFN_HALO - back + 1])

    def conv(c, slot, part):
        out = cb_ref[:, cols(c, part)]
        for i in range(CONV_WIDTH):
            off = i * SUBLANES
            out = out + up_ref[slot, part, off:off + tm, :] * cw_ref[i:i + 1, cols(c, part)]
        return out

    def activate(c, slot):
        a_ref[:, chunk(c * FFN_CK)] = (
            jax.nn.gelu(conv(c, slot, 1)) * conv(c, slot, 0)).astype(BF16)

    for c in range(n_chunk):
        up_proj(c, c % 2)
        activate(c, c % 2)
    y = _dot(a_ref[...], wdn_ref[...])
    for slab in range(n_slab):
        perm_ref[slab] = y[:, slab * LANES:(slab + 1) * LANES]
    for slab in range(n_slab):
        lanes = slice(slab * LANES, (slab + 1) * LANES)
        for s in range(SUBLANES):
            o_ref[0, natural_rows(s), lanes] += (
                g2_ref[0, :, lanes] * perm_ref[slab, interleaved_rows(s), :])


def _mix_ffn(x, o_attn, o_ssm2, ga, wa, ws, g1, sh, sc, g2, g, wup, cw, cb, wdn):
    b, l, d = x.shape
    aw = o_attn.shape[-1]
    sw = o_ssm2.shape[-1] // b
    d_ff = wdn.shape[0]
    tm = min(FFN_TM, l)
    halo_per_tile = tm // FFN_HALO
    halo = lambda i: jnp.maximum(i * halo_per_tile - 1, 0)
    bvec = pl.BlockSpec((1, 1, d), lambda bi, i: (bi, 0, 0))
    const = lambda a: pl.BlockSpec(a.shape, lambda bi, i: (0,) * a.ndim,
                                   pipeline_mode=pl.Buffered(1))
    return pl.pallas_call(
        functools.partial(_mix_ffn_kernel, d_ff=d_ff),
        grid=(b, l // tm),
        in_specs=[
            pl.BlockSpec((1, tm, d), lambda bi, i: (bi, i, 0)),
            pl.BlockSpec((1, FFN_HALO, d), lambda bi, i: (bi, halo(i), 0)),
            pl.BlockSpec((1, tm, aw), lambda bi, i: (bi, i, 0)),
            pl.BlockSpec((1, FFN_HALO, aw), lambda bi, i: (bi, halo(i), 0)),
            pl.BlockSpec((tm, sw), lambda bi, i: (i, bi)),
            pl.BlockSpec((FFN_HALO, sw), lambda bi, i: (halo(i), bi)),
            const(ga), const(wa), const(ws), bvec,
            bvec, bvec, bvec, const(g), const(wup), const(cw), const(cb), const(wdn),
        ],
        out_specs=pl.BlockSpec((1, tm, d), lambda bi, i: (bi, i, 0)),
        out_shape=jax.ShapeDtypeStruct(x.shape, F32),
        scratch_shapes=[pltpu.VMEM((d // LANES, tm, LANES), F32),
                        pltpu.VMEM((tm + FFN_HALO, d), BF16),
                        pltpu.VMEM((2, 2, tm + (CONV_WIDTH - 1) * SUBLANES, FFN_CK), F32),
                        pltpu.VMEM((tm, d_ff), BF16)],
        compiler_params=_cparams(("parallel", "parallel")),
        name="mix_ffn",
    )(x, x, o_attn, o_attn, o_ssm2, o_ssm2, ga, wa, ws, g1, sh, sc, g2, g, wup, cw, cb, wdn)


def kernel(x, c, ada_w, ada_b, norm1_g, w_in, q_norm_g, k_norm_g, ssm_a_re, ssm_a_im,
           ssm_log_dt, ssm_b_re, ssm_b_im, ssm_c_re, ssm_c_im, ssm_d, glu_w, glu_b,
           attn_out_g, ssm_out_g, w_out, norm2_g, ffn_w_up, ffn_conv_w, ffn_conv_b,
           ffn_w_down):
    b, l, d = x.shape
    depth = ada_w.shape[0]
    aw = attn_out_g.shape[-1]
    sw = ssm_out_g.shape[-1]
    n_heads = aw // HEAD_DIM
    assert b == SUBLANES and aw % LANES == 0 and sw % LANES == 0
    assert l % ATT_T == 0 and l % SSM_T == 0

    mod = _ada_mod(c, ada_w, ada_b)
    head = jnp.arange(aw) // HEAD_DIM
    ones_blk = (head[:, None] == head[None, :]).astype(BF16)
    inv_sqrt = LOG2_E / math.sqrt(HEAD_DIM)

    for li in range(depth):
        m = mod[li].reshape(b, N_MOD, 1, d)
        sh1, sc1, g1, sh2, sc2, g2 = (m[:, i] for i in range(N_MOD))
        gq = jnp.tile(q_norm_g[li] * inv_sqrt, n_heads).reshape(1, aw)
        gk = jnp.tile(k_norm_g[li], n_heads).reshape(1, aw)
        q, k, v, u2 = _in_proj(x, sh1, sc1, norm1_g[li].reshape(1, d),
                               w_in[li].astype(BF16), gq, gk, ones_blk, aw, sw)
        o_attn = _attn(q, k, v)
        prm = _ssm_params(ssm_a_re[li], ssm_a_im[li], ssm_log_dt[li], ssm_b_re[li],
                          ssm_b_im[li], ssm_c_re[li], ssm_c_im[li], ssm_d[li],
                          glu_w[li], glu_b[li], ssm_out_g[li])
        o_ssm = _ssm(u2, prm, b)
        w_o = w_out[li].astype(BF16)
        x = _mix_ffn(x, o_attn, o_ssm, attn_out_g[li].reshape(1, aw), w_o[:aw], w_o[aw:], g1,
                     sh2, sc2, g2, norm2_g[li].reshape(1, d), ffn_w_up[li].astype(BF16),
                     ffn_conv_w[li], ffn_conv_b[li].reshape(1, -1),
                     ffn_w_down[li].astype(BF16))
    return x
```

```python
import functools
import math

import jax
import jax.numpy as jnp
from jax import lax
from jax.experimental import pallas as pl
from jax.experimental.pallas import tpu as pltpu

F32 = jnp.float32
BF16 = jnp.bfloat16

EPS = 1e-6
HEAD_DIM = 64
SSM_GROUP = 16
SSM_STATE = 64
CONV_WIDTH = 3
N_MOD = 6

LANES = 128
SUBLANES = 8
MXU_TILE = 256
VMEM_LIMIT_BYTES = 60 * 1024 * 1024

ADA_TN = 1536
PROJ_TM = 512
ATT_T = 128
ATT_QS = 512
ATT_WINDOW = 3
SSM_T = 64
FFN_TM = 1024
FFN_CK = 256
FFN_HALO = 16

UNDERFLOW_BOUND = 152.0
LOG2_E = 1.4426950408889634


def _cparams(sem):
    return pltpu.CompilerParams(dimension_semantics=sem,
                                vmem_limit_bytes=VMEM_LIMIT_BYTES)


def _dot(a, b):
    return jnp.dot(a, b, preferred_element_type=F32)


def _ada_kernel(c_ref, w_ref, b_ref, o_ref):
    c = c_ref[...]
    c_act = c * jax.nn.sigmoid(c)
    c_hi = c_act.astype(BF16)
    c_lo = (c_act - c_hi.astype(F32)).astype(BF16)
    w = w_ref[0]
    w_hi = w.astype(BF16)
    w_lo = (w - w_hi.astype(F32)).astype(BF16)
    acc = _dot(c_hi, w_hi) + (_dot(c_hi, w_lo) + _dot(c_lo, w_hi))
    o_ref[0] = acc + b_ref[0]


def _ada_mod(c, ada_w, ada_b):
    depth, d, n = ada_w.shape
    b = c.shape[0]
    return pl.pallas_call(
        _ada_kernel,
        grid=(depth, n // ADA_TN),
        in_specs=[
            pl.BlockSpec((b, d), lambda l, j: (0, 0)),
            pl.BlockSpec((1, d, ADA_TN), lambda l, j: (l, 0, j)),
            pl.BlockSpec((1, 1, ADA_TN), lambda l, j: (l, 0, j)),
        ],
        out_specs=pl.BlockSpec((1, b, ADA_TN), lambda l, j: (l, 0, j)),
        out_shape=jax.ShapeDtypeStruct((depth, b, n), F32),
        compiler_params=_cparams(("parallel", "parallel")),
        name="ada_mod",
    )(c, ada_w, ada_b.reshape(depth, 1, n))


def _norm_mod(xf, g, sh, sc):
    inv = lax.rsqrt(jnp.mean(xf * xf, axis=-1, keepdims=True) + EPS)
    return xf * inv * (g * (1.0 + sc)) + sh


def _in_proj_kernel(x_ref, sh_ref, sc_ref, g_ref, w_ref, gq_ref, gk_ref, ones_ref,
                    q_ref, k_ref, v_ref, u_ref, *, aw):
    hb = _norm_mod(x_ref[0], g_ref[...], sh_ref[0], sc_ref[0]).astype(BF16)
    n_pair = aw // LANES

    def head_norm(p, g):
        sq = (p * p).astype(BF16)
        ms = jnp.concatenate([_dot(sq[:, c:c + MXU_TILE], ones_ref[...])
                              for c in range(0, aw, MXU_TILE)], axis=1) * (1.0 / HEAD_DIM)
        return p * lax.rsqrt(ms + EPS) * g

    q = head_norm(_dot(hb, w_ref[:, 0:aw]), gq_ref[...]).astype(BF16)
    for hp in range(n_pair):
        q_ref[0, hp] = q[:, hp * LANES:(hp + 1) * LANES]
    k = head_norm(_dot(hb, w_ref[:, aw:2 * aw]), gk_ref[...]).astype(BF16)
    for hp in range(n_pair):
        k_ref[0, hp] = k[:, hp * LANES:(hp + 1) * LANES]
    v = _dot(hb, w_ref[:, 2 * aw:3 * aw]).astype(BF16)
    for hp in range(n_pair):
        v_ref[0, hp] = v[:, hp * LANES:(hp + 1) * LANES]
    u_ref[...] = _dot(hb, w_ref[:, 3 * aw:]).astype(BF16)


def _in_proj(x, sh, sc, g, w_in_bf, gq, gk, ones_blk, aw, sw):
    b, l, d = x.shape
    tm = min(PROJ_TM, l)
    n_pair = aw // LANES
    qkv_shape = jax.ShapeDtypeStruct((b, n_pair, l, LANES), BF16)
    qkv_spec = pl.BlockSpec((1, n_pair, tm, LANES), lambda bi, i: (bi, 0, i, 0))
    vec = lambda n: pl.BlockSpec((1, n), lambda bi, i: (0, 0))
    bvec = pl.BlockSpec((1, 1, d), lambda bi, i: (bi, 0, 0))
    return pl.pallas_call(
        functools.partial(_in_proj_kernel, aw=aw),
        grid=(b, l // tm),
        in_specs=[
            pl.BlockSpec((1, tm, d), lambda bi, i: (bi, i, 0)),
            bvec, bvec, vec(d),
            pl.BlockSpec(w_in_bf.shape, lambda bi, i: (0, 0)),
            vec(aw), vec(aw),
            pl.BlockSpec(ones_blk.shape, lambda bi, i: (0, 0)),
        ],
        out_specs=[qkv_spec, qkv_spec, qkv_spec,
                   pl.BlockSpec((tm, sw), lambda bi, i: (i, bi))],
        out_shape=[qkv_shape, qkv_shape, qkv_shape,
                   jax.ShapeDtypeStruct((l, b * sw), BF16)],
        compiler_params=_cparams(("parallel", "parallel")),
        name="in_proj",
    )(x, sh, sc, g, w_in_bf, gq, gk, ones_blk)


def _attn_kernel(q_ref, k_ref, v_ref, o_ref, acc_ref, carry_ref, next_ref, bound_ref):
    t = ATT_T
    n_sub = q_ref.shape[2] // t
    base = pl.program_id(2) * n_sub
    lane = lax.broadcasted_iota(jnp.int32, (1, LANES), 1)
    first = lane < HEAD_DIM
    row = lax.broadcasted_iota(jnp.int32, (t, t), 0)
    col = lax.broadcasted_iota(jnp.int32, (t, t), 1)
    later = jnp.where(row > col, 1.0, 0.0).astype(BF16)
    strict = col < row

    def split_heads(ref, j):
        blk = ref[0, 0, pl.ds(pl.multiple_of(j * t, t), t), :]
        zero = jnp.zeros_like(blk)
        return jnp.where(first, blk, zero), jnp.where(first, zero, blk)

    def blocks(items, carry_in=None):
        zz = [lax.dot_general(q2, jnp.concatenate(split_heads(k_ref, j), axis=0),
                              (((1,), (1,)), ((), ())),
                              preferred_element_type=F32) for q2, j, _ in items]
        logb, hilo, sums = [], [], []
        for (_, _, depth), zz_i in zip(items, zz):
            for h in range(2):
                z = zz_i[:, h * t:(h + 1) * t]
                neg_abs = lax.bitcast_convert_type(
                    lax.bitcast_convert_type(z, jnp.uint32) | jnp.uint32(0x80000000), F32)
                s = jnp.maximum(z, 0.0) + jnp.log2(1.0 + jnp.exp2(neg_abs))
                sm = jnp.where(strict, s, 0.0) if depth == 0 else s
                hi = sm.astype(BF16)
                lo = (sm - hi.astype(F32)).astype(BF16)
                logb.append(z - s)
                hilo.append((hi, lo))
                sums.append(jnp.sum(sm, axis=-1, keepdims=True))
        within = [_dot(hi, later) + _dot(lo, later) for hi, lo in hilo]
        ws, carries = [], []
        for n, (_, _, depth) in enumerate(items):
            carry = carries[-1] if depth > 0 else carry_in
            w_heads, after = [], []
            for h in range(2):
                arg = logb[2 * n + h] - within[2 * n + h]
                if carry is not None:
                    arg = arg - carry[h]
                w = jnp.exp2(arg)
                if depth == 0:
                    w = jnp.where(strict, w, 0.0)
                w_heads.append(w.astype(BF16))
                after.append(sums[2 * n + h] if carry is None else carry[h] + sums[2 * n + h])
            ws.append(w_heads)
            carries.append(after)
        pvs = []
        for (w0, w1), (_, j, _) in zip(ws, items):
            v0, v1 = split_heads(v_ref, j)
            pvs.append(_dot(w0, v0) + _dot(w1, v1))
        return pvs, carries

    def first_blocks(depths):
        items, owner = [], []
        for k in range(n_sub):
            q2 = q_ref[0, 0, k * t:(k + 1) * t, :]
            for d in range(depths[k]):
                items.append((q2, base + k - d, d))
                owner.append(k)
        pvs, carries = blocks(items)
        for k in range(n_sub):
            mine = [n for n, o in enumerate(owner) if o == k]
            acc = pvs[mine[0]]
            for n in mine[1:]:
                acc = acc + pvs[n]
            carry = carries[mine[-1]]
            acc_ref[k] = acc
            for h in range(2):
                carry_ref[k, h] = carry[h]
            next_ref[k] = base + k - depths[k]
            bound_ref[k] = jnp.min(jnp.minimum(carry[0], carry[1]))

    @pl.when(base >= ATT_WINDOW - 1)
    def _():
        first_blocks([ATT_WINDOW] * n_sub)

    @pl.when(base < ATT_WINDOW - 1)
    def _():
        first_blocks([ATT_WINDOW if k >= ATT_WINDOW - 1 else 1 for k in range(n_sub)])

    def finish(k, _):
        rows = pl.ds(pl.multiple_of(k * t, t), t)
        q2 = q_ref[0, 0, rows, :]

        def cond(state):
            j, bound = state
            return jnp.logical_and(j >= 0, bound < UNDERFLOW_BOUND)

        def body(state):
            j, _ = state
            carry = [carry_ref[k, 0], carry_ref[k, 1]]
            (pv,), (carry,) = blocks([(q2, j, -1)], carry)
            acc_ref[k] += pv
            for h in range(2):
                carry_ref[k, h] = carry[h]
            return j - 1, jnp.min(jnp.minimum(carry[0], carry[1]))

        lax.while_loop(cond, body, (next_ref[k], bound_ref[k]))
        o_ref[0, rows, :] = acc_ref[k].astype(o_ref.dtype)
        return 0

    lax.fori_loop(0, n_sub, finish, 0)


def _attn(q, k, v):
    b, n_pair, l, _ = q.shape
    t = ATT_T
    qs = min(ATT_QS, l)
    kv_spec = pl.BlockSpec((1, 1, l, LANES), lambda bi, hp, i: (bi, hp, 0, 0))
    return pl.pallas_call(
        _attn_kernel,
        grid=(b, n_pair, l // qs),
        in_specs=[pl.BlockSpec((1, 1, qs, LANES), lambda bi, hp, i: (bi, hp, i, 0)),
                  kv_spec, kv_spec],
        out_specs=pl.BlockSpec((1, qs, LANES), lambda bi, hp, i: (bi, i, hp)),
        out_shape=jax.ShapeDtypeStruct((b, l, n_pair * LANES), BF16),
        scratch_shapes=[pltpu.VMEM((qs // t, t, LANES), F32),
                        pltpu.VMEM((qs // t, 2, t, 1), F32),
                        pltpu.SMEM((qs // t,), jnp.int32),
                        pltpu.SMEM((qs // t,), F32)],
        compiler_params=_cparams(("parallel", "parallel", "parallel")),
        name="attn",
    )(q, k, v)


def _ssm_kernel(u_ref, wb_ref, are_ref, aim_ref, wcre_ref, wcim_ref, d_ref,
                wglu_ref, bglu_ref, g_ref, o_ref, sre_ref, sim_ref, st_ref, y_ref,
                *, n_oct, batch):
    n_tok = u_ref.shape[0]
    sw = u_ref.shape[1] // batch
    oct_states = sre_ref.shape[2]

    @pl.when(pl.program_id(0) == 0)
    def _():
        st_ref[...] = jnp.zeros_like(st_ref)

    for bi in range(batch):
        for o in range(n_oct):
            c0 = bi * sw + o * LANES
            y_ref[o, pl.ds(bi, n_tok, stride=batch), :] = u_ref[:, c0:c0 + LANES].astype(F32)
    def input_proj(o):
        bu = _dot(y_ref[o].astype(BF16), wb_ref[o])
        sre_ref[o] = bu[:, :oct_states]
        sim_ref[o] = bu[:, oct_states:]

    def recurrence(o):
        cols = slice(o * oct_states, (o + 1) * oct_states)
        are = jnp.broadcast_to(are_ref[:, cols], (batch, oct_states))
        aim = jnp.broadcast_to(aim_ref[:, cols], (batch, oct_states))
        s_re, s_im = st_ref[0, :, cols], st_ref[1, :, cols]
        for t in range(n_tok):
            r = slice(t * batch, (t + 1) * batch)
            s_re, s_im = (are * s_re - aim * s_im + sre_ref[o, r, :],
                          are * s_im + aim * s_re + sim_ref[o, r, :])
            sre_ref[o, r, :] = s_re
            sim_ref[o, r, :] = s_im
        st_ref[0, :, cols] = s_re
        st_ref[1, :, cols] = s_im

    def output_proj(o):
        lanes = slice(o * LANES, (o + 1) * LANES)
        y = (_dot(sre_ref[o].astype(BF16), wcre_ref[o])
             + _dot(sim_ref[o].astype(BF16), wcim_ref[o])
             + d_ref[:, lanes] * y_ref[o])
        y = jax.nn.gelu(y)
        gate = jax.nn.sigmoid(_dot(y.astype(BF16), wglu_ref[o]) + bglu_ref[:, lanes])
        out = y * gate
        y_ref[o] = out
        return jnp.sum(out * out, axis=-1, keepdims=True)

    sumsq = 0.0
    input_proj(0)
    for o in range(n_oct):
        if o + 1 < n_oct:
            input_proj(o + 1)
        recurrence(o)
        if o >= 1:
            sumsq = sumsq + output_proj(o - 1)
    sumsq = sumsq + output_proj(n_oct - 1)
    inv = lax.rsqrt(sumsq * (1.0 / sw) + EPS)
    for o in range(n_oct):
        y_ref[o] = y_ref[o] * inv * g_ref[:, o * LANES:(o + 1) * LANES]
    for bi in range(batch):
        for o in range(n_oct):
            c0 = bi * sw + o * LANES
            o_ref[:, c0:c0 + LANES] = (
                y_ref[o, pl.ds(bi, n_tok, stride=batch), :].astype(o_ref.dtype))


def _ssm(u2, prm, batch):
    l, width = u2.shape
    sw = width // batch
    n_oct = sw // LANES
    n_state = prm["are"].shape[1]
    rows = SSM_T * batch
    full = lambda a: pl.BlockSpec(a.shape, lambda i: (0,) * a.ndim)
    names = ["wb", "are", "aim", "wcre", "wcim", "d", "wglu", "bglu", "g"]
    return pl.pallas_call(
        functools.partial(_ssm_kernel, n_oct=n_oct, batch=batch),
        grid=(l // SSM_T,),
        in_specs=[pl.BlockSpec((SSM_T, width), lambda i: (i, 0))]
                 + [full(prm[n]) for n in names],
        out_specs=pl.BlockSpec((SSM_T, width), lambda i: (i, 0)),
        out_shape=jax.ShapeDtypeStruct((l, width), BF16),
        scratch_shapes=[pltpu.VMEM((n_oct, rows, n_state // n_oct), F32),
                        pltpu.VMEM((n_oct, rows, n_state // n_oct), F32),
                        pltpu.VMEM((2, batch, n_state), F32),
                        pltpu.VMEM((n_oct, rows, LANES), F32)],
        compiler_params=_cparams(("arbitrary",)),
        name="ssm",
    )(u2, *[prm[n] for n in names])


def _ssm_params(a_re, a_im, log_dt, b_re, b_im, c_re, c_im, d_skip, glu_w, glu_b, out_g):
    g, p, h = b_re.shape
    per_oct = LANES // h
    n_oct = g // per_oct
    dt = jnp.exp(log_dt)[:, None]
    mag = jnp.exp(dt * a_re)
    abar_re = mag * jnp.cos(dt * a_im)
    abar_im = mag * jnp.sin(dt * a_im)
    em_re = abar_re - 1.0
    em_im = abar_im
    den = a_re * a_re + a_im * a_im
    f_re = (em_re * a_re + em_im * a_im) / den
    f_im = (em_im * a_re - em_re * a_im) / den
    bb_re = f_re[..., None] * b_re - f_im[..., None] * b_im
    bb_im = f_re[..., None] * b_im + f_im[..., None] * b_re
    eye = jnp.eye(per_oct, dtype=F32)

    def in_blockdiag(bb):
        bb = bb.reshape(n_oct, per_oct, p, h)
        m = jnp.einsum("ogph,gk->oghkp", bb, eye)
        return m.reshape(n_oct, per_oct * h, per_oct * p)

    def out_blockdiag(c):
        c = c.reshape(n_oct, per_oct, h, p)
        m = jnp.einsum("oghp,gk->ogpkh", c, eye)
        return m.reshape(n_oct, per_oct * p, per_oct * h)

    wb = jnp.concatenate([in_blockdiag(bb_re), in_blockdiag(bb_im)], axis=-1).astype(BF16)
    wglu = jnp.einsum("oghk,gf->oghfk", glu_w.reshape(n_oct, per_oct, h, h), eye)
    wglu = wglu.reshape(n_oct, per_oct * h, per_oct * h).astype(BF16)
    return {
        "wb": wb,
        "are": abar_re.reshape(1, g * p), "aim": abar_im.reshape(1, g * p),
        "wcre": out_blockdiag(c_re).astype(BF16),
        "wcim": out_blockdiag(-c_im).astype(BF16),
        "d": d_skip.reshape(1, g * h), "wglu": wglu,
        "bglu": glu_b.reshape(1, g * h), "g": out_g.reshape(1, g * h),
    }


def _mix_ffn_kernel(x_ref, xh_ref, oa_ref, oah_ref, os_ref, osh_ref, ga_ref, wa_ref, ws_ref,
                    g1_ref, sh_ref, sc_ref, g2_ref, g_ref, wup_ref, cw_ref, cb_ref, wdn_ref,
                    o_ref, perm_ref, h_ref, up_ref, a_ref, *, d_ff):
    tm, d = x_ref.shape[1], x_ref.shape[2]
    n_grp = tm // SUBLANES
    n_slab = d // LANES
    n_chunk = d_ff // FFN_CK
    g, sh, sc = g_ref[...], sh_ref[0], sc_ref[0]

    def natural_rows(s):
        return slice(s * n_grp, (s + 1) * n_grp)

    def interleaved_rows(s):
        return pl.ds(s, n_grp, stride=SUBLANES)

    oa = jnp.concatenate([oa_ref[0], oah_ref[0]], axis=0).astype(F32)
    inv = lax.rsqrt(jnp.mean(oa * oa, axis=-1, keepdims=True) + EPS)
    mixed = (_dot((oa * inv * ga_ref[...]).astype(BF16), wa_ref[...])
             + _dot(jnp.concatenate([os_ref[...], osh_ref[...]], axis=0), ws_ref[...]))
    o_ref[0] = x_ref[0] + g1_ref[0] * mixed[:tm]
    x1_halo = xh_ref[0] + g1_ref[0] * mixed[tm:]

    hn = _norm_mod(o_ref[0], g, sh, sc)
    for slab in range(n_slab):
        lanes = slice(slab * LANES, (slab + 1) * LANES)
        for s in range(SUBLANES):
            perm_ref[slab, interleaved_rows(s), :] = hn[natural_rows(s), lanes]
    for slab in range(n_slab):
        h_ref[:tm, slab * LANES:(slab + 1) * LANES] = perm_ref[slab].astype(BF16)
    keep = (pl.program_id(1) > 0).astype(F32)
    h_ref[tm:, :] = (_norm_mod(x1_halo, g, sh, sc) * keep).astype(BF16)
    sublane = lax.broadcasted_iota(jnp.int32, (SUBLANES, FFN_CK), 0)

    def chunk(start):
        if not isinstance(start, int):
            start = pl.multiple_of(start, FFN_CK)
        return pl.ds(start, FFN_CK)

    def cols(c, part):
        return chunk(part * d_ff + c * FFN_CK)

    def wrapped(group, before):
        return jnp.where(sublane == 0, before, pltpu.roll(group, 1, axis=0))

    def up_proj(c, slot):
        for part in range(2):
            r = _dot(h_ref[...], wup_ref[:, cols(c, part)])
            pad = (CONV_WIDTH - 1) * SUBLANES
            up_ref[slot, part, pad:, :] = r[:tm]
            for back in range(1, CONV_WIDTH):
                up_ref[slot, part, pad - back * SUBLANES:pad - (back - 1) * SUBLANES, :] = wrapped(
                    r[tm - back * SUBLANES:tm - (back - 1) * SUBLANES],
                    r[tm + FFN_HALO - back:tm + FFN_HALO - back + 1])

    def conv(c, slot, part):
        out = cb_ref[:, cols(c, part)]
        for i in range(CONV_WIDTH):
            off = i * SUBLANES
            out = out + up_ref[slot, part, off:off + tm, :] * cw_ref[i:i + 1, cols(c, part)]
        return out

    def activate(c, slot):
        a_ref[:, chunk(c * FFN_CK)] = (
            jax.nn.gelu(conv(c, slot, 1)) * conv(c, slot, 0)).astype(BF16)

    for c in range(n_chunk):
        up_proj(c, c % 2)
        activate(c, c % 2)
    y = _dot(a_ref[...], wdn_ref[...])
    for slab in range(n_slab):
        perm_ref[slab] = y[:, slab * LANES:(slab + 1) * LANES]
    for slab in range(n_slab):
        lanes = slice(slab * LANES, (slab + 1) * LANES)
        for s in range(SUBLANES):
            o_ref[0, natural_rows(s), lanes] += (
                g2_ref[0, :, lanes] * perm_ref[slab, interleaved_rows(s), :])


def _mix_ffn(x, o_attn, o_ssm2, ga, wa, ws, g1, sh, sc, g2, g, wup, cw, cb, wdn):
    b, l, d = x.shape
    aw = o_attn.shape[-1]
    sw = o_ssm2.shape[-1] // b
    d_ff = wdn.shape[0]
    tm = min(FFN_TM, l)
    halo_per_tile = tm // FFN_HALO
    halo = lambda i: jnp.maximum(i * halo_per_tile - 1, 0)
    bvec = pl.BlockSpec((1, 1, d), lambda bi, i: (bi, 0, 0))
    const = lambda a: pl.BlockSpec(a.shape, lambda bi, i: (0,) * a.ndim,
                                   pipeline_mode=pl.Buffered(1))
    return pl.pallas_call(
        functools.partial(_mix_ffn_kernel, d_ff=d_ff),
        grid=(b, l // tm),
        in_specs=[
            pl.BlockSpec((1, tm, d), lambda bi, i: (bi, i, 0)),
            pl.BlockSpec((1, FFN_HALO, d), lambda bi, i: (bi, halo(i), 0)),
            pl.BlockSpec((1, tm, aw), lambda bi, i: (bi, i, 0)),
            pl.BlockSpec((1, FFN_HALO, aw), lambda bi, i: (bi, halo(i), 0)),
            pl.BlockSpec((tm, sw), lambda bi, i: (i, bi)),
            pl.BlockSpec((FFN_HALO, sw), lambda bi, i: (halo(i), bi)),
            const(ga), const(wa), const(ws), bvec,
            bvec, bvec, bvec, const(g), const(wup), const(cw), const(cb), const(wdn),
        ],
        out_specs=pl.BlockSpec((1, tm, d), lambda bi, i: (bi, i, 0)),
        out_shape=jax.ShapeDtypeStruct(x.shape, F32),
        scratch_shapes=[pltpu.VMEM((d // LANES, tm, LANES), F32),
                        pltpu.VMEM((tm + FFN_HALO, d), BF16),
                        pltpu.VMEM((2, 2, tm + (CONV_WIDTH - 1) * SUBLANES, FFN_CK), F32),
                        pltpu.VMEM((tm, d_ff), BF16)],
        compiler_params=_cparams(("parallel", "parallel")),
        name="mix_ffn",
    )(x, x, o_attn, o_attn, o_ssm2, o_ssm2, ga, wa, ws, g1, sh, sc, g2, g, wup, cw, cb, wdn)


def kernel(x, c, ada_w, ada_b, norm1_g, w_in, q_norm_g, k_norm_g, ssm_a_re, ssm_a_im,
           ssm_log_dt, ssm_b_re, ssm_b_im, ssm_c_re, ssm_c_im, ssm_d, glu_w, glu_b,
           attn_out_g, ssm_out_g, w_out, norm2_g, ffn_w_up, ffn_conv_w, ffn_conv_b,
           ffn_w_down):
    b, l, d = x.shape
    depth = ada_w.shape[0]
    aw = attn_out_g.shape[-1]
    sw = ssm_out_g.shape[-1]
    n_heads = aw // HEAD_DIM
    assert b == SUBLANES and aw % LANES == 0 and sw % LANES == 0
    assert l % ATT_T == 0 and l % SSM_T == 0

    mod = _ada_mod(c, ada_w, ada_b)
    head = jnp.arange(aw) // HEAD_DIM
    ones_blk = (head[:MXU_TILE, None] == head[None, :MXU_TILE]).astype(BF16)
    inv_sqrt = LOG2_E / math.sqrt(HEAD_DIM)

    for li in range(depth):
        m = mod[li].reshape(b, N_MOD, 1, d)
        sh1, sc1, g1, sh2, sc2, g2 = (m[:, i] for i in range(N_MOD))
        gq = jnp.tile(q_norm_g[li] * inv_sqrt, n_heads).reshape(1, aw)
        gk = jnp.tile(k_norm_g[li], n_heads).reshape(1, aw)
        q, k, v, u2 = _in_proj(x, sh1, sc1, norm1_g[li].reshape(1, d),
                               w_in[li].astype(BF16), gq, gk, ones_blk, aw, sw)
        o_attn = _attn(q, k, v)
        prm = _ssm_params(ssm_a_re[li], ssm_a_im[li], ssm_log_dt[li], ssm_b_re[li],
                          ssm_b_im[li], ssm_c_re[li], ssm_c_im[li], ssm_d[li],
                          glu_w[li], glu_b[li], ssm_out_g[li])
        o_ssm = _ssm(u2, prm, b)
        w_o = w_out[li].astype(BF16)
        x = _mix_ffn(x, o_attn, o_ssm, attn_out_g[li].reshape(1, aw), w_o[:aw], w_o[aw:], g1,
                     sh2, sc2, g2, norm2_g[li].reshape(1, d), ffn_w_up[li].astype(BF16),
                     ffn_conv_w[li], ffn_conv_b[li].reshape(1, -1),
                     ffn_w_down[li].astype(BF16))
    return x
```

```python
import functools
import math

import jax
import jax.numpy as jnp
from jax import lax
from jax.experimental import pallas as pl
from jax.experimental.pallas import tpu as pltpu

F32 = jnp.float32
BF16 = jnp.bfloat16

EPS = 1e-6
HEAD_DIM = 64
SSM_GROUP = 16
SSM_STATE = 64
CONV_WIDTH = 3
N_MOD = 6

LANES = 128
SUBLANES = 8
MXU_TILE = 256
VMEM_LIMIT_BYTES = 56 * 1024 * 1024

ADA_TN = 1536
PROJ_TM = 512
ATT_T = 128
ATT_QS = 2048
ATT_WINDOW = 3
ATT_GROUP = 16
SSM_T = 128
FFN_TM = 512
FFN_CK = 256
FFN_HALO = 16

UNDERFLOW_BOUND = 152.0
LOG2_E = 1.4426950408889634


def _cparams(sem):
    return pltpu.CompilerParams(dimension_semantics=sem,
                                vmem_limit_bytes=VMEM_LIMIT_BYTES)


def _dot(a, b):
    return jnp.dot(a, b, preferred_element_type=F32)


def _ada_kernel(c_ref, w_ref, b_ref, o_ref):
    c = c_ref[...]
    c_act = c * jax.nn.sigmoid(c)
    c_hi = c_act.astype(BF16)
    c_lo = (c_act - c_hi.astype(F32)).astype(BF16)
    w = w_ref[0]
    w_hi = w.astype(BF16)
    w_lo = (w - w_hi.astype(F32)).astype(BF16)
    acc = _dot(c_hi, w_hi) + (_dot(c_hi, w_lo) + _dot(c_lo, w_hi))
    o_ref[0] = acc + b_ref[0]


def _ada_mod(c, ada_w, ada_b):
    depth, d, n = ada_w.shape
    b = c.shape[0]
    return pl.pallas_call(
        _ada_kernel,
        grid=(depth, n // ADA_TN),
        in_specs=[
            pl.BlockSpec((b, d), lambda l, j: (0, 0)),
            pl.BlockSpec((1, d, ADA_TN), lambda l, j: (l, 0, j)),
            pl.BlockSpec((1, 1, ADA_TN), lambda l, j: (l, 0, j)),
        ],
        out_specs=pl.BlockSpec((1, b, ADA_TN), lambda l, j: (l, 0, j)),
        out_shape=jax.ShapeDtypeStruct((depth, b, n), F32),
        compiler_params=_cparams(("parallel", "parallel")),
        name="ada_mod",
    )(c, ada_w, ada_b.reshape(depth, 1, n))


def _norm_mod(xf, g, sh, sc):
    inv = lax.rsqrt(jnp.mean(xf * xf, axis=-1, keepdims=True) + EPS)
    return xf * inv * (g * (1.0 + sc)) + sh


def _in_proj_kernel(x_ref, sh_ref, sc_ref, g_ref, w_ref, gq_ref, gk_ref, ones_ref,
                    q_ref, k_ref, v_ref, u_ref, *, aw):
    hb = _norm_mod(x_ref[0], g_ref[...], sh_ref[0], sc_ref[0]).astype(BF16)
    n_pair = aw // LANES

    def head_norm(p, g):
        sq = (p * p).astype(BF16)
        ms = jnp.concatenate([_dot(sq[:, c:c + MXU_TILE], ones_ref[...])
                              for c in range(0, aw, MXU_TILE)], axis=1) * (1.0 / HEAD_DIM)
        return p * lax.rsqrt(ms + EPS) * g

    q = head_norm(_dot(hb, w_ref[:, 0:aw]), gq_ref[...]).astype(BF16)
    for hp in range(n_pair):
        q_ref[0, hp] = q[:, hp * LANES:(hp + 1) * LANES]
    k = head_norm(_dot(hb, w_ref[:, aw:2 * aw]), gk_ref[...]).astype(BF16)
    for hp in range(n_pair):
        k_ref[0, hp] = k[:, hp * LANES:(hp + 1) * LANES]
    v = _dot(hb, w_ref[:, 2 * aw:3 * aw]).astype(BF16)
    for hp in range(n_pair):
        v_ref[0, hp] = v[:, hp * LANES:(hp + 1) * LANES]
    u_ref[...] = _dot(hb, w_ref[:, 3 * aw:]).astype(BF16)


def _in_proj(x, sh, sc, g, w_in_bf, gq, gk, ones_blk, aw, sw):
    b, l, d = x.shape
    tm = min(PROJ_TM, l)
    n_pair = aw // LANES
    qkv_shape = jax.ShapeDtypeStruct((b, n_pair, l, LANES), BF16)
    qkv_spec = pl.BlockSpec((1, n_pair, tm, LANES), lambda bi, i: (bi, 0, i, 0))
    vec = lambda n: pl.BlockSpec((1, n), lambda bi, i: (0, 0))
    bvec = pl.BlockSpec((1, 1, d), lambda bi, i: (bi, 0, 0))
    return pl.pallas_call(
        functools.partial(_in_proj_kernel, aw=aw),
        grid=(b, l // tm),
        in_specs=[
            pl.BlockSpec((1, tm, d), lambda bi, i: (bi, i, 0)),
            bvec, bvec, vec(d),
            pl.BlockSpec(w_in_bf.shape, lambda bi, i: (0, 0)),
            vec(aw), vec(aw),
            pl.BlockSpec(ones_blk.shape, lambda bi, i: (0, 0)),
        ],
        out_specs=[qkv_spec, qkv_spec, qkv_spec,
                   pl.BlockSpec((tm, sw), lambda bi, i: (i, bi))],
        out_shape=[qkv_shape, qkv_shape, qkv_shape,
                   jax.ShapeDtypeStruct((l, b * sw), BF16)],
        compiler_params=_cparams(("parallel", "parallel")),
        name="in_proj",
    )(x, sh, sc, g, w_in_bf, gq, gk, ones_blk)


def _attn_kernel(q_ref, k_ref, v_ref, o_ref, acc_ref, carry_ref, next_ref, bound_ref):
    t = ATT_T
    n_sub = q_ref.shape[2] // t
    base = pl.program_id(2) * n_sub
    lane = lax.broadcasted_iota(jnp.int32, (1, LANES), 1)
    first = lane < HEAD_DIM
    row = lax.broadcasted_iota(jnp.int32, (t, t), 0)
    col = lax.broadcasted_iota(jnp.int32, (t, t), 1)
    later = jnp.where(row > col, 1.0, 0.0).astype(BF16)
    strict = col < row

    def split_heads(ref, j):
        blk = ref[0, 0, pl.ds(pl.multiple_of(j * t, t), t), :]
        zero = jnp.zeros_like(blk)
        return jnp.where(first, blk, zero), jnp.where(first, zero, blk)

    def blocks(items, carry_in=None):
        zz = [lax.dot_general(q2, jnp.concatenate(split_heads(k_ref, j), axis=0),
                              (((1,), (1,)), ((), ())),
                              preferred_element_type=F32) for q2, j, _ in items]
        logb, hilo, sums = [], [], []
        for (_, _, depth), zz_i in zip(items, zz):
            for h in range(2):
                z = zz_i[:, h * t:(h + 1) * t]
                neg_abs = lax.bitcast_convert_type(
                    lax.bitcast_convert_type(z, jnp.uint32) | jnp.uint32(0x80000000), F32)
                s = jnp.maximum(z, 0.0) + jnp.log2(1.0 + jnp.exp2(neg_abs))
                sm = jnp.where(strict, s, 0.0) if depth == 0 else s
                hi = sm.astype(BF16)
                lo = (sm - hi.astype(F32)).astype(BF16)
                logb.append(z - s)
                hilo.append((hi, lo))
                sums.append(jnp.sum(sm, axis=-1, keepdims=True))
        within = [_dot(hi, later) + _dot(lo, later) for hi, lo in hilo]
        ws, carries = [], []
        for n, (_, _, depth) in enumerate(items):
            carry = carries[-1] if depth > 0 else carry_in
            w_heads, after = [], []
            for h in range(2):
                arg = logb[2 * n + h] - within[2 * n + h]
                if carry is not None:
                    arg = arg - carry[h]
                w = jnp.exp2(arg)
                if depth == 0:
                    w = jnp.where(strict, w, 0.0)
                w_heads.append(w.astype(BF16))
                after.append(sums[2 * n + h] if carry is None else carry[h] + sums[2 * n + h])
            ws.append(w_heads)
            carries.append(after)
        pvs = []
        for (w0, w1), (_, j, _) in zip(ws, items):
            v0, v1 = split_heads(v_ref, j)
            pvs.append(_dot(w0, v0) + _dot(w1, v1))
        return pvs, carries

    def first_blocks(depths):
        for g in range(0, n_sub, ATT_GROUP):
            first_blocks_of(depths, range(g, min(g + ATT_GROUP, n_sub)))

    def first_blocks_of(depths, group):
        items, owner = [], []
        for k in group:
            q2 = q_ref[0, 0, k * t:(k + 1) * t, :]
            for d in range(depths[k]):
                items.append((q2, base + k - d, d))
                owner.append(k)
        pvs, carries = blocks(items)
        for k in group:
            mine = [n for n, o in enumerate(owner) if o == k]
            acc = pvs[mine[0]]
            for n in mine[1:]:
                acc = acc + pvs[n]
            carry = carries[mine[-1]]
            acc_ref[k] = acc
            for h in range(2):
                carry_ref[k, h] = carry[h]
            next_ref[k] = base + k - depths[k]
            bound_ref[k] = jnp.min(jnp.minimum(carry[0], carry[1]))

    @pl.when(base >= ATT_WINDOW - 1)
    def _():
        first_blocks([ATT_WINDOW] * n_sub)

    @pl.when(base < ATT_WINDOW - 1)
    def _():
        first_blocks([ATT_WINDOW if k >= ATT_WINDOW - 1 else 1 for k in range(n_sub)])

    def finish(k, _):
        rows = pl.ds(pl.multiple_of(k * t, t), t)
        q2 = q_ref[0, 0, rows, :]

        def cond(state):
            j, bound = state
            return jnp.logical_and(j >= 0, bound < UNDERFLOW_BOUND)

        def body(state):
            j, _ = state
            carry = [carry_ref[k, 0], carry_ref[k, 1]]
            (pv,), (carry,) = blocks([(q2, j, -1)], carry)
            acc_ref[k] += pv
            for h in range(2):
                carry_ref[k, h] = carry[h]
            return j - 1, jnp.min(jnp.minimum(carry[0], carry[1]))

        lax.while_loop(cond, body, (next_ref[k], bound_ref[k]))
        o_ref[0, rows, :] = acc_ref[k].astype(o_ref.dtype)
        return 0

    lax.fori_loop(0, n_sub, finish, 0)


def _attn(q, k, v):
    b, n_pair, l, _ = q.shape
    t = ATT_T
    qs = min(ATT_QS, l)
    kv_spec = pl.BlockSpec((1, 1, l, LANES), lambda bi, hp, i: (bi, hp, 0, 0))
    return pl.pallas_call(
        _attn_kernel,
        grid=(b, n_pair, l // qs),
        in_specs=[pl.BlockSpec((1, 1, qs, LANES), lambda bi, hp, i: (bi, hp, i, 0)),
                  kv_spec, kv_spec],
        out_specs=pl.BlockSpec((1, qs, LANES), lambda bi, hp, i: (bi, i, hp)),
        out_shape=jax.ShapeDtypeStruct((b, l, n_pair * LANES), BF16),
        scratch_shapes=[pltpu.VMEM((qs // t, t, LANES), F32),
                        pltpu.VMEM((qs // t, 2, t, 1), F32),
                        pltpu.SMEM((qs // t,), jnp.int32),
                        pltpu.SMEM((qs // t,), F32)],
        compiler_params=_cparams(("parallel", "parallel", "parallel")),
        name="attn",
    )(q, k, v)


def _ssm_kernel(u_ref, wb_ref, are_ref, aim_ref, wcre_ref, wcim_ref, d_ref,
                wglu_ref, bglu_ref, g_ref, o_ref, sre_ref, sim_ref, st_ref, y_ref,
                *, n_oct, batch):
    n_tok = u_ref.shape[0]
    sw = u_ref.shape[1] // batch
    oct_states = sre_ref.shape[2]

    @pl.when(pl.program_id(0) == 0)
    def _():
        st_ref[...] = jnp.zeros_like(st_ref)

    for bi in range(batch):
        for o in range(n_oct):
            c0 = bi * sw + o * LANES
            y_ref[o, pl.ds(bi, n_tok, stride=batch), :] = u_ref[:, c0:c0 + LANES].astype(F32)
    def input_proj(o):
        bu = _dot(y_ref[o].astype(BF16), wb_ref[o])
        sre_ref[o] = bu[:, :oct_states]
        sim_ref[o] = bu[:, oct_states:]

    def recurrence(o):
        cols = slice(o * oct_states, (o + 1) * oct_states)
        are = jnp.broadcast_to(are_ref[:, cols], (batch, oct_states))
        aim = jnp.broadcast_to(aim_ref[:, cols], (batch, oct_states))
        s_re, s_im = st_ref[0, :, cols], st_ref[1, :, cols]
        for t in range(n_tok):
            r = slice(t * batch, (t + 1) * batch)
            s_re, s_im = (are * s_re - aim * s_im + sre_ref[o, r, :],
                          are * s_im + aim * s_re + sim_ref[o, r, :])
            sre_ref[o, r, :] = s_re
            sim_ref[o, r, :] = s_im
        st_ref[0, :, cols] = s_re
        st_ref[1, :, cols] = s_im

    def output_proj(o):
        lanes = slice(o * LANES, (o + 1) * LANES)
        y = (_dot(sre_ref[o].astype(BF16), wcre_ref[o])
             + _dot(sim_ref[o].astype(BF16), wcim_ref[o])
             + d_ref[:, lanes] * y_ref[o])
        y = jax.nn.gelu(y)
        gate = jax.nn.sigmoid(_dot(y.astype(BF16), wglu_ref[o]) + bglu_ref[:, lanes])
        out = y * gate
        y_ref[o] = out
        return jnp.sum(out * out, axis=-1, keepdims=True)

    sumsq = 0.0
    input_proj(0)
    for o in range(n_oct):
        if o + 1 < n_oct:
            input_proj(o + 1)
        recurrence(o)
        if o >= 1:
            sumsq = sumsq + output_proj(o - 1)
    sumsq = sumsq + output_proj(n_oct - 1)
    inv = lax.rsqrt(sumsq * (1.0 / sw) + EPS)
    for o in range(n_oct):
        y_ref[o] = y_ref[o] * inv * g_ref[:, o * LANES:(o + 1) * LANES]
    for bi in range(batch):
        for o in range(n_oct):
            c0 = bi * sw + o * LANES
            o_ref[:, c0:c0 + LANES] = (
                y_ref[o, pl.ds(bi, n_tok, stride=batch), :].astype(o_ref.dtype))


def _ssm(u2, prm, batch):
    l, width = u2.shape
    sw = width // batch
    n_oct = sw // LANES
    n_state = prm["are"].shape[1]
    rows = SSM_T * batch
    full = lambda a: pl.BlockSpec(a.shape, lambda i: (0,) * a.ndim)
    names = ["wb", "are", "aim", "wcre", "wcim", "d", "wglu", "bglu", "g"]
    return pl.pallas_call(
        functools.partial(_ssm_kernel, n_oct=n_oct, batch=batch),
        grid=(l // SSM_T,),
        in_specs=[pl.BlockSpec((SSM_T, width), lambda i: (i, 0))]
                 + [full(prm[n]) for n in names],
        out_specs=pl.BlockSpec((SSM_T, width), lambda i: (i, 0)),
        out_shape=jax.ShapeDtypeStruct((l, width), BF16),
        scratch_shapes=[pltpu.VMEM((n_oct, rows, n_state // n_oct), F32),
                        pltpu.VMEM((n_oct, rows, n_state // n_oct), F32),
                        pltpu.VMEM((2, batch, n_state), F32),
                        pltpu.VMEM((n_oct, rows, LANES), F32)],
        compiler_params=_cparams(("arbitrary",)),
        name="ssm",
    )(u2, *[prm[n] for n in names])


def _ssm_params(a_re, a_im, log_dt, b_re, b_im, c_re, c_im, d_skip, glu_w, glu_b, out_g):
    g, p, h = b_re.shape
    per_oct = LANES // h
    n_oct = g // per_oct
    dt = jnp.exp(log_dt)[:, None]
    mag = jnp.exp(dt * a_re)
    abar_re = mag * jnp.cos(dt * a_im)
    abar_im = mag * jnp.sin(dt * a_im)
    em_re = abar_re - 1.0
    em_im = abar_im
    den = a_re * a_re + a_im * a_im
    f_re = (em_re * a_re + em_im * a_im) / den
    f_im = (em_im * a_re - em_re * a_im) / den
    bb_re = f_re[..., None] * b_re - f_im[..., None] * b_im
    bb_im = f_re[..., None] * b_im + f_im[..., None] * b_re
    eye = jnp.eye(per_oct, dtype=F32)

    def in_blockdiag(bb):
        bb = bb.reshape(n_oct, per_oct, p, h)
        m = jnp.einsum("ogph,gk->oghkp", bb, eye)
        return m.reshape(n_oct, per_oct * h, per_oct * p)

    def out_blockdiag(c):
        c = c.reshape(n_oct, per_oct, h, p)
        m = jnp.einsum("oghp,gk->ogpkh", c, eye)
        return m.reshape(n_oct, per_oct * p, per_oct * h)

    wb = jnp.concatenate([in_blockdiag(bb_re), in_blockdiag(bb_im)], axis=-1).astype(BF16)
    wglu = jnp.einsum("oghk,gf->oghfk", glu_w.reshape(n_oct, per_oct, h, h), eye)
    wglu = wglu.reshape(n_oct, per_oct * h, per_oct * h).astype(BF16)
    return {
        "wb": wb,
        "are": abar_re.reshape(1, g * p), "aim": abar_im.reshape(1, g * p),
        "wcre": out_blockdiag(c_re).astype(BF16),
        "wcim": out_blockdiag(-c_im).astype(BF16),
        "d": d_skip.reshape(1, g * h), "wglu": wglu,
        "bglu": glu_b.reshape(1, g * h), "g": out_g.reshape(1, g * h),
    }


def _mix_ffn_kernel(x_ref, xh_ref, oa_ref, oah_ref, os_ref, osh_ref, ga_ref, wa_ref, ws_ref,
                    g1_ref, sh_ref, sc_ref, g2_ref, g_ref, wup_ref, cw_ref, cb_ref, wdn_ref,
                    o_ref, perm_ref, h_ref, up_ref, a_ref, *, d_ff):
    tm, d = x_ref.shape[1], x_ref.shape[2]
    n_grp = tm // SUBLANES
    n_slab = d // LANES
    n_chunk = d_ff // FFN_CK
    g, sh, sc = g_ref[...], sh_ref[0], sc_ref[0]

    def natural_rows(s):
        return slice(s * n_grp, (s + 1) * n_grp)

    def interleaved_rows(s):
        return pl.ds(s, n_grp, stride=SUBLANES)

    oa = jnp.concatenate([oa_ref[0], oah_ref[0]], axis=0).astype(F32)
    inv = lax.rsqrt(jnp.mean(oa * oa, axis=-1, keepdims=True) + EPS)
    mixed = (_dot((oa * inv * ga_ref[...]).astype(BF16), wa_ref[...])
             + _dot(jnp.concatenate([os_ref[...], osh_ref[...]], axis=0), ws_ref[...]))
    o_ref[0] = x_ref[0] + g1_ref[0] * mixed[:tm]
    x1_halo = xh_ref[0] + g1_ref[0] * mixed[tm:]

    hn = _norm_mod(o_ref[0], g, sh, sc)
    for slab in range(n_slab):
        lanes = slice(slab * LANES, (slab + 1) * LANES)
        for s in range(SUBLANES):
            perm_ref[slab, interleaved_rows(s), :] = hn[natural_rows(s), lanes]
    for slab in range(n_slab):
        h_ref[:tm, slab * LANES:(slab + 1) * LANES] = perm_ref[slab].astype(BF16)
    keep = (pl.program_id(1) > 0).astype(F32)
    h_ref[tm:, :] = (_norm_mod(x1_halo, g, sh, sc) * keep).astype(BF16)
    sublane = lax.broadcasted_iota(jnp.int32, (SUBLANES, FFN_CK), 0)

    def chunk(start):
        if not isinstance(start, int):
            start = pl.multiple_of(start, FFN_CK)
        return pl.ds(start, FFN_CK)

    def cols(c, part):
        return chunk(part * d_ff + c * FFN_CK)

    def wrapped(group, before):
        return jnp.where(sublane == 0, before, pltpu.roll(group, 1, axis=0))

    def up_proj(c, slot):
        for part in range(2):
            r = _dot(h_ref[...], wup_ref[:, cols(c, part)])
            pad = (CONV_WIDTH - 1) * SUBLANES
            up_ref[slot, part, pad:, :] = r[:tm]
            for back in range(1, CONV_WIDTH):
                up_ref[slot, part, pad - back * SUBLANES:pad - (back - 1) * SUBLANES, :] = wrapped(
                    r[tm - back * SUBLANES:tm - (back - 1) * SUBLANES],
                    r[tm + FFN_HALO - back:tm + FFN_HALO - back + 1])

    def conv(c, slot, part):
        out = cb_ref[:, cols(c, part)]
        for i in range(CONV_WIDTH):
            off = i * SUBLANES
            out = out + up_ref[slot, part, off:off + tm, :] * cw_ref[i:i + 1, cols(c, part)]
        return out

    def activate(c, slot):
        a_ref[:, chunk(c * FFN_CK)] = (
            jax.nn.gelu(conv(c, slot, 1)) * conv(c, slot, 0)).astype(BF16)

    for c in range(n_chunk):
        up_proj(c, c % 2)
        activate(c, c % 2)
    y = _dot(a_ref[...], wdn_ref[...])
    for slab in range(n_slab):
        perm_ref[slab] = y[:, slab * LANES:(slab + 1) * LANES]
    for slab in range(n_slab):
        lanes = slice(slab * LANES, (slab + 1) * LANES)
        for s in range(SUBLANES):
            o_ref[0, natural_rows(s), lanes] += (
                g2_ref[0, :, lanes] * perm_ref[slab, interleaved_rows(s), :])


def _mix_ffn(x, o_attn, o_ssm2, ga, wa, ws, g1, sh, sc, g2, g, wup, cw, cb, wdn):
    b, l, d = x.shape
    aw = o_attn.shape[-1]
    sw = o_ssm2.shape[-1] // b
    d_ff = wdn.shape[0]
    tm = min(FFN_TM, l)
    halo_per_tile = tm // FFN_HALO
    halo = lambda i: jnp.maximum(i * halo_per_tile - 1, 0)
    bvec = pl.BlockSpec((1, 1, d), lambda bi, i: (bi, 0, 0))
    const = lambda a: pl.BlockSpec(a.shape, lambda bi, i: (0,) * a.ndim,
                                   pipeline_mode=pl.Buffered(1))
    return pl.pallas_call(
        functools.partial(_mix_ffn_kernel, d_ff=d_ff),
        grid=(b, l // tm),
        in_specs=[
            pl.BlockSpec((1, tm, d), lambda bi, i: (bi, i, 0)),
            pl.BlockSpec((1, FFN_HALO, d), lambda bi, i: (bi, halo(i), 0)),
            pl.BlockSpec((1, tm, aw), lambda bi, i: (bi, i, 0)),
            pl.BlockSpec((1, FFN_HALO, aw), lambda bi, i: (bi, halo(i), 0)),
            pl.BlockSpec((tm, sw), lambda bi, i: (i, bi)),
            pl.BlockSpec((FFN_HALO, sw), lambda bi, i: (halo(i), bi)),
            const(ga), const(wa), const(ws), bvec,
            bvec, bvec, bvec, const(g), const(wup), const(cw), const(cb), const(wdn),
        ],
        out_specs=pl.BlockSpec((1, tm, d), lambda bi, i: (bi, i, 0)),
        out_shape=jax.ShapeDtypeStruct(x.shape, F32),
        scratch_shapes=[pltpu.VMEM((d // LANES, tm, LANES), F32),
                        pltpu.VMEM((tm + FFN_HALO, d), BF16),
                        pltpu.VMEM((2, 2, tm + (CONV_WIDTH - 1) * SUBLANES, FFN_CK), F32),
                        pltpu.VMEM((tm, d_ff), BF16)],
        compiler_params=_cparams(("parallel", "parallel")),
        name="mix_ffn",
    )(x, x, o_attn, o_attn, o_ssm2, o_ssm2, ga, wa, ws, g1, sh, sc, g2, g, wup, cw, cb, wdn)


def kernel(x, c, ada_w, ada_b, norm1_g, w_in, q_norm_g, k_norm_g, ssm_a_re, ssm_a_im,
           ssm_log_dt, ssm_b_re, ssm_b_im, ssm_c_re, ssm_c_im, ssm_d, glu_w, glu_b,
           attn_out_g, ssm_out_g, w_out, norm2_g, ffn_w_up, ffn_conv_w, ffn_conv_b,
           ffn_w_down):
    b, l, d = x.shape
    depth = ada_w.shape[0]
    aw = attn_out_g.shape[-1]
    sw = ssm_out_g.shape[-1]
    n_heads = aw // HEAD_DIM
    assert b == SUBLANES and aw % LANES == 0 and sw % LANES == 0
    assert l % ATT_T == 0 and l % SSM_T == 0

    mod = _ada_mod(c, ada_w, ada_b)
    head = jnp.arange(aw) // HEAD_DIM
    ones_blk = (head[:MXU_TILE, None] == head[None, :MXU_TILE]).astype(BF16)
    inv_sqrt = LOG2_E / math.sqrt(HEAD_DIM)

    for li in range(depth):
        m = mod[li].reshape(b, N_MOD, 1, d)
        sh1, sc1, g1, sh2, sc2, g2 = (m[:, i] for i in range(N_MOD))
        gq = jnp.tile(q_norm_g[li] * inv_sqrt, n_heads).reshape(1, aw)
        gk = jnp.tile(k_norm_g[li], n_heads).reshape(1, aw)
        q, k, v, u2 = _in_proj(x, sh1, sc1, norm1_g[li].reshape(1, d),
                               w_in[li].astype(BF16), gq, gk, ones_blk, aw, sw)
        o_attn = _attn(q, k, v)
        prm = _ssm_params(ssm_a_re[li], ssm_a_im[li], ssm_log_dt[li], ssm_b_re[li],
                          ssm_b_im[li], ssm_c_re[li], ssm_c_im[li], ssm_d[li],
                          glu_w[li], glu_b[li], ssm_out_g[li])
        o_ssm = _ssm(u2, prm, b)
        w_o = w_out[li].astype(BF16)
        x = _mix_ffn(x, o_attn, o_ssm, attn_out_g[li].reshape(1, aw), w_o[:aw], w_o[aw:], g1,
                     sh2, sc2, g2, norm2_g[li].reshape(1, d), ffn_w_up[li].astype(BF16),
                     ffn_conv_w[li], ffn_conv_b[li].reshape(1, -1),
                     ffn_w_down[li].astype(BF16))
    return x
```

```python
import functools
import math

import jax
import jax.numpy as jnp
from jax import lax
from jax.experimental import pallas as pl
from jax.experimental.pallas import tpu as pltpu

F32 = jnp.float32
BF16 = jnp.bfloat16

EPS = 1e-6
HEAD_DIM = 64
SSM_GROUP = 16
SSM_STATE = 64
CONV_WIDTH = 3
N_MOD = 6

LANES = 128
SUBLANES = 8
MXU_TILE = 256
VMEM_LIMIT_BYTES = 56 * 1024 * 1024

ADA_TN = 1536
PROJ_TM = 512
ATT_T = 128
ATT_QS = 2048
ATT_WINDOW = 3
ATT_GROUP = 16
SSM_T = 128
FFN_TM = 512
FFN_CK = 256
FFN_HALO = 16

UNDERFLOW_BOUND = 152.0
LOG2_E = 1.4426950408889634


def _cparams(sem):
    return pltpu.CompilerParams(dimension_semantics=sem,
                                vmem_limit_bytes=VMEM_LIMIT_BYTES)


def _dot(a, b):
    return jnp.dot(a, b, preferred_element_type=F32)


def _ada_kernel(c_ref, w_ref, b_ref, o_ref):
    c = c_ref[...]
    c_act = c * jax.nn.sigmoid(c)
    c_hi = c_act.astype(BF16)
    c_lo = (c_act - c_hi.astype(F32)).astype(BF16)
    w = w_ref[0]
    w_hi = w.astype(BF16)
    w_lo = (w - w_hi.astype(F32)).astype(BF16)
    acc = _dot(c_hi, w_hi) + (_dot(c_hi, w_lo) + _dot(c_lo, w_hi))
    o_ref[0] = acc + b_ref[0]


def _ada_mod(c, ada_w, ada_b):
    depth, d, n = ada_w.shape
    b = c.shape[0]
    return pl.pallas_call(
        _ada_kernel,
        grid=(depth, n // ADA_TN),
        in_specs=[
            pl.BlockSpec((b, d), lambda l, j: (0, 0)),
            pl.BlockSpec((1, d, ADA_TN), lambda l, j: (l, 0, j)),
            pl.BlockSpec((1, 1, ADA_TN), lambda l, j: (l, 0, j)),
        ],
        out_specs=pl.BlockSpec((1, b, ADA_TN), lambda l, j: (l, 0, j)),
        out_shape=jax.ShapeDtypeStruct((depth, b, n), F32),
        compiler_params=_cparams(("parallel", "parallel")),
        name="ada_mod",
    )(c, ada_w, ada_b.reshape(depth, 1, n))


def _norm_mod(xf, g, sh, sc):
    inv = lax.rsqrt(jnp.mean(xf * xf, axis=-1, keepdims=True) + EPS)
    return xf * inv * (g * (1.0 + sc)) + sh


def _in_proj_kernel(x_ref, sh_ref, sc_ref, g_ref, w_ref, gq_ref, gk_ref, ones_ref,
                    q_ref, k_ref, v_ref, u_ref, *, aw):
    hb = _norm_mod(x_ref[0], g_ref[...], sh_ref[0], sc_ref[0]).astype(BF16)
    n_pair = aw // LANES

    def head_norm(p, g):
        sq = (p * p).astype(BF16)
        ms = jnp.concatenate([_dot(sq[:, c:c + MXU_TILE], ones_ref[...])
                              for c in range(0, aw, MXU_TILE)], axis=1) * (1.0 / HEAD_DIM)
        return p * lax.rsqrt(ms + EPS) * g

    q = head_norm(_dot(hb, w_ref[:, 0:aw]), gq_ref[...]).astype(BF16)
    for hp in range(n_pair):
        q_ref[0, hp] = q[:, hp * LANES:(hp + 1) * LANES]
    k = head_norm(_dot(hb, w_ref[:, aw:2 * aw]), gk_ref[...]).astype(BF16)
    for hp in range(n_pair):
        k_ref[0, hp] = k[:, hp * LANES:(hp + 1) * LANES]
    v = _dot(hb, w_ref[:, 2 * aw:3 * aw]).astype(BF16)
    for hp in range(n_pair):
        v_ref[0, hp] = v[:, hp * LANES:(hp + 1) * LANES]
    u_ref[...] = _dot(hb, w_ref[:, 3 * aw:]).astype(BF16)


def _layer_block(stacked, li, index_args=2):
    zeros = (0,) * (stacked.ndim - 1)
    if index_args == 1:
        return pl.BlockSpec((None,) + stacked.shape[1:], lambda i: (li,) + zeros)
    return pl.BlockSpec((None,) + stacked.shape[1:], lambda bi, i: (li,) + zeros,
                        pipeline_mode=pl.Buffered(1))


def _in_proj(x, sh, sc, g, w_in_bf, li, gq, gk, ones_blk, aw, sw):
    b, l, d = x.shape
    tm = min(PROJ_TM, l)
    n_pair = aw // LANES
    qkv_shape = jax.ShapeDtypeStruct((b, n_pair, l, LANES), BF16)
    qkv_spec = pl.BlockSpec((1, n_pair, tm, LANES), lambda bi, i: (bi, 0, i, 0))
    vec = lambda n: pl.BlockSpec((1, n), lambda bi, i: (0, 0))
    bvec = pl.BlockSpec((1, 1, d), lambda bi, i: (bi, 0, 0))
    return pl.pallas_call(
        functools.partial(_in_proj_kernel, aw=aw),
        grid=(b, l // tm),
        in_specs=[
            pl.BlockSpec((1, tm, d), lambda bi, i: (bi, i, 0)),
            bvec, bvec, vec(d),
            _layer_block(w_in_bf, li),
            vec(aw), vec(aw),
            pl.BlockSpec(ones_blk.shape, lambda bi, i: (0, 0)),
        ],
        out_specs=[qkv_spec, qkv_spec, qkv_spec,
                   pl.BlockSpec((tm, sw), lambda bi, i: (i, bi))],
        out_shape=[qkv_shape, qkv_shape, qkv_shape,
                   jax.ShapeDtypeStruct((l, b * sw), BF16)],
        compiler_params=_cparams(("parallel", "parallel")),
        name="in_proj",
    )(x, sh, sc, g, w_in_bf, gq, gk, ones_blk)


def _attn_kernel(q_ref, k_ref, v_ref, o_ref, acc_ref, carry_ref, next_ref, bound_ref):
    t = ATT_T
    n_sub = q_ref.shape[2] // t
    base = pl.program_id(2) * n_sub
    lane = lax.broadcasted_iota(jnp.int32, (1, LANES), 1)
    first = lane < HEAD_DIM
    row = lax.broadcasted_iota(jnp.int32, (t, t), 0)
    col = lax.broadcasted_iota(jnp.int32, (t, t), 1)
    later = jnp.where(row > col, 1.0, 0.0).astype(BF16)
    strict = col < row

    def split_heads(ref, j):
        blk = ref[0, 0, pl.ds(pl.multiple_of(j * t, t), t), :]
        zero = jnp.zeros_like(blk)
        return jnp.where(first, blk, zero), jnp.where(first, zero, blk)

    def blocks(items, carry_in=None):
        zz = [lax.dot_general(q2, jnp.concatenate(split_heads(k_ref, j), axis=0),
                              (((1,), (1,)), ((), ())),
                              preferred_element_type=F32) for q2, j, _ in items]
        logb, hilo, sums = [], [], []
        for (_, _, depth), zz_i in zip(items, zz):
            for h in range(2):
                z = zz_i[:, h * t:(h + 1) * t]
                neg_abs = lax.bitcast_convert_type(
                    lax.bitcast_convert_type(z, jnp.uint32) | jnp.uint32(0x80000000), F32)
                s = jnp.maximum(z, 0.0) + jnp.log2(1.0 + jnp.exp2(neg_abs))
                sm = jnp.where(strict, s, 0.0) if depth == 0 else s
                hi = sm.astype(BF16)
                lo = (sm - hi.astype(F32)).astype(BF16)
                logb.append(z - s)
                hilo.append((hi, lo))
                sums.append(jnp.sum(sm, axis=-1, keepdims=True))
        within = [_dot(hi, later) + _dot(lo, later) for hi, lo in hilo]
        ws, carries = [], []
        for n, (_, _, depth) in enumerate(items):
            carry = carries[-1] if depth > 0 else carry_in
            w_heads, after = [], []
            for h in range(2):
                arg = logb[2 * n + h] - within[2 * n + h]
                if carry is not None:
                    arg = arg - carry[h]
                w = jnp.exp2(arg)
                if depth == 0:
                    w = jnp.where(strict, w, 0.0)
                w_heads.append(w.astype(BF16))
                after.append(sums[2 * n + h] if carry is None else carry[h] + sums[2 * n + h])
            ws.append(w_heads)
            carries.append(after)
        pvs = []
        for (w0, w1), (_, j, _) in zip(ws, items):
            v0, v1 = split_heads(v_ref, j)
            pvs.append(_dot(w0, v0) + _dot(w1, v1))
        return pvs, carries

    def first_blocks(depths):
        for g in range(0, n_sub, ATT_GROUP):
            first_blocks_of(depths, range(g, min(g + ATT_GROUP, n_sub)))

    def first_blocks_of(depths, group):
        items, owner = [], []
        for k in group:
            q2 = q_ref[0, 0, k * t:(k + 1) * t, :]
            for d in range(depths[k]):
                items.append((q2, base + k - d, d))
                owner.append(k)
        pvs, carries = blocks(items)
        for k in group:
            mine = [n for n, o in enumerate(owner) if o == k]
            acc = pvs[mine[0]]
            for n in mine[1:]:
                acc = acc + pvs[n]
            carry = carries[mine[-1]]
            acc_ref[k] = acc
            for h in range(2):
                carry_ref[k, h] = carry[h]
            next_ref[k] = base + k - depths[k]
            bound_ref[k] = jnp.min(jnp.minimum(carry[0], carry[1]))

    @pl.when(base >= ATT_WINDOW - 1)
    def _():
        first_blocks([ATT_WINDOW] * n_sub)

    @pl.when(base < ATT_WINDOW - 1)
    def _():
        first_blocks([ATT_WINDOW if k >= ATT_WINDOW - 1 else 1 for k in range(n_sub)])

    def finish(k, _):
        rows = pl.ds(pl.multiple_of(k * t, t), t)
        q2 = q_ref[0, 0, rows, :]

        def cond(state):
            j, bound = state
            return jnp.logical_and(j >= 0, bound < UNDERFLOW_BOUND)

        def body(state):
            j, _ = state
            carry = [carry_ref[k, 0], carry_ref[k, 1]]
            (pv,), (carry,) = blocks([(q2, j, -1)], carry)
            acc_ref[k] += pv
            for h in range(2):
                carry_ref[k, h] = carry[h]
            return j - 1, jnp.min(jnp.minimum(carry[0], carry[1]))

        lax.while_loop(cond, body, (next_ref[k], bound_ref[k]))
        o_ref[0, rows, :] = acc_ref[k].astype(o_ref.dtype)
        return 0

    lax.fori_loop(0, n_sub, finish, 0)


def _attn(q, k, v):
    b, n_pair, l, _ = q.shape
    t = ATT_T
    qs = min(ATT_QS, l)
    kv_spec = pl.BlockSpec((1, 1, l, LANES), lambda bi, hp, i: (bi, hp, 0, 0))
    return pl.pallas_call(
        _attn_kernel,
        grid=(b, n_pair, l // qs),
        in_specs=[pl.BlockSpec((1, 1, qs, LANES), lambda bi, hp, i: (bi, hp, i, 0)),
                  kv_spec, kv_spec],
        out_specs=pl.BlockSpec((1, qs, LANES), lambda bi, hp, i: (bi, i, hp)),
        out_shape=jax.ShapeDtypeStruct((b, l, n_pair * LANES), BF16),
        scratch_shapes=[pltpu.VMEM((qs // t, t, LANES), F32),
                        pltpu.VMEM((qs // t, 2, t, 1), F32),
                        pltpu.SMEM((qs // t,), jnp.int32),
                        pltpu.SMEM((qs // t,), F32)],
        compiler_params=_cparams(("parallel", "parallel", "parallel")),
        name="attn",
    )(q, k, v)


def _ssm_kernel(u_ref, wb_ref, are_ref, aim_ref, wcre_ref, wcim_ref, d_ref,
                wglu_ref, bglu_ref, g_ref, o_ref, sre_ref, sim_ref, st_ref, y_ref,
                *, n_oct, batch):
    n_tok = u_ref.shape[0]
    sw = u_ref.shape[1] // batch
    oct_states = sre_ref.shape[2]

    @pl.when(pl.program_id(0) == 0)
    def _():
        st_ref[...] = jnp.zeros_like(st_ref)

    for bi in range(batch):
        for o in range(n_oct):
            c0 = bi * sw + o * LANES
            y_ref[o, pl.ds(bi, n_tok, stride=batch), :] = u_ref[:, c0:c0 + LANES].astype(F32)
    def input_proj(o):
        bu = _dot(y_ref[o].astype(BF16), wb_ref[o])
        sre_ref[o] = bu[:, :oct_states]
        sim_ref[o] = bu[:, oct_states:]

    def recurrence(o):
        cols = slice(o * oct_states, (o + 1) * oct_states)
        are = jnp.broadcast_to(are_ref[:, cols], (batch, oct_states))
        aim = jnp.broadcast_to(aim_ref[:, cols], (batch, oct_states))
        s_re, s_im = st_ref[0, :, cols], st_ref[1, :, cols]
        for t in range(n_tok):
            r = slice(t * batch, (t + 1) * batch)
            s_re, s_im = (are * s_re - aim * s_im + sre_ref[o, r, :],
                          are * s_im + aim * s_re + sim_ref[o, r, :])
            sre_ref[o, r, :] = s_re
            sim_ref[o, r, :] = s_im
        st_ref[0, :, cols] = s_re
        st_ref[1, :, cols] = s_im

    def output_proj(o):
        lanes = slice(o * LANES, (o + 1) * LANES)
        y = (_dot(sre_ref[o].astype(BF16), wcre_ref[o])
             + _dot(sim_ref[o].astype(BF16), wcim_ref[o])
             + d_ref[:, lanes] * y_ref[o])
        y = jax.nn.gelu(y)
        gate = jax.nn.sigmoid(_dot(y.astype(BF16), wglu_ref[o]) + bglu_ref[:, lanes])
        out = y * gate
        y_ref[o] = out
        return jnp.sum(out * out, axis=-1, keepdims=True)

    sumsq = 0.0
    input_proj(0)
    for o in range(n_oct):
        if o + 1 < n_oct:
            input_proj(o + 1)
        recurrence(o)
        if o >= 1:
            sumsq = sumsq + output_proj(o - 1)
    sumsq = sumsq + output_proj(n_oct - 1)
    inv = lax.rsqrt(sumsq * (1.0 / sw) + EPS)
    for o in range(n_oct):
        y_ref[o] = y_ref[o] * inv * g_ref[:, o * LANES:(o + 1) * LANES]
    for bi in range(batch):
        for o in range(n_oct):
            c0 = bi * sw + o * LANES
            o_ref[:, c0:c0 + LANES] = (
                y_ref[o, pl.ds(bi, n_tok, stride=batch), :].astype(o_ref.dtype))


def _ssm(u2, prm, li, batch):
    l, width = u2.shape
    sw = width // batch
    n_oct = sw // LANES
    n_state = prm["are"].shape[-1]
    rows = SSM_T * batch
    full = lambda a: _layer_block(a, li, index_args=1)
    names = ["wb", "are", "aim", "wcre", "wcim", "d", "wglu", "bglu", "g"]
    return pl.pallas_call(
        functools.partial(_ssm_kernel, n_oct=n_oct, batch=batch),
        grid=(l // SSM_T,),
        in_specs=[pl.BlockSpec((SSM_T, width), lambda i: (i, 0))]
                 + [full(prm[n]) for n in names],
        out_specs=pl.BlockSpec((SSM_T, width), lambda i: (i, 0)),
        out_shape=jax.ShapeDtypeStruct((l, width), BF16),
        scratch_shapes=[pltpu.VMEM((n_oct, rows, n_state // n_oct), F32),
                        pltpu.VMEM((n_oct, rows, n_state // n_oct), F32),
                        pltpu.VMEM((2, batch, n_state), F32),
                        pltpu.VMEM((n_oct, rows, LANES), F32)],
        compiler_params=_cparams(("arbitrary",)),
        name="ssm",
    )(u2, *[prm[n] for n in names])


def _ssm_params(a_re, a_im, log_dt, b_re, b_im, c_re, c_im, d_skip, glu_w, glu_b, out_g):
    g, p, h = b_re.shape
    per_oct = LANES // h
    n_oct = g // per_oct
    dt = jnp.exp(log_dt)[:, None]
    mag = jnp.exp(dt * a_re)
    abar_re = mag * jnp.cos(dt * a_im)
    abar_im = mag * jnp.sin(dt * a_im)
    em_re = abar_re - 1.0
    em_im = abar_im
    den = a_re * a_re + a_im * a_im
    f_re = (em_re * a_re + em_im * a_im) / den
    f_im = (em_im * a_re - em_re * a_im) / den
    bb_re = f_re[..., None] * b_re - f_im[..., None] * b_im
    bb_im = f_re[..., None] * b_im + f_im[..., None] * b_re
    eye = jnp.eye(per_oct, dtype=F32)

    def in_blockdiag(bb):
        bb = bb.reshape(n_oct, per_oct, p, h)
        m = jnp.einsum("ogph,gk->oghkp", bb, eye)
        return m.reshape(n_oct, per_oct * h, per_oct * p)

    def out_blockdiag(c):
        c = c.reshape(n_oct, per_oct, h, p)
        m = jnp.einsum("oghp,gk->ogpkh", c, eye)
        return m.reshape(n_oct, per_oct * p, per_oct * h)

    wb = jnp.concatenate([in_blockdiag(bb_re), in_blockdiag(bb_im)], axis=-1).astype(BF16)
    wglu = jnp.einsum("oghk,gf->oghfk", glu_w.reshape(n_oct, per_oct, h, h), eye)
    wglu = wglu.reshape(n_oct, per_oct * h, per_oct * h).astype(BF16)
    return {
        "wb": wb,
        "are": abar_re.reshape(1, g * p), "aim": abar_im.reshape(1, g * p),
        "wcre": out_blockdiag(c_re).astype(BF16),
        "wcim": out_blockdiag(-c_im).astype(BF16),
        "d": d_skip.reshape(1, g * h), "wglu": wglu,
        "bglu": glu_b.reshape(1, g * h), "g": out_g.reshape(1, g * h),
    }


def _mix_ffn_kernel(x_ref, xh_ref, oa_ref, oah_ref, os_ref, osh_ref, ga_ref, wa_ref, ws_ref,
                    g1_ref, sh_ref, sc_ref, g2_ref, g_ref, wup_ref, cw_ref, cb_ref, wdn_ref,
                    o_ref, perm_ref, h_ref, up_ref, a_ref, *, d_ff):
    tm, d = x_ref.shape[1], x_ref.shape[2]
    n_grp = tm // SUBLANES
    n_slab = d // LANES
    n_chunk = d_ff // FFN_CK
    g, sh, sc = g_ref[...], sh_ref[0], sc_ref[0]

    def natural_rows(s):
        return slice(s * n_grp, (s + 1) * n_grp)

    def interleaved_rows(s):
        return pl.ds(s, n_grp, stride=SUBLANES)

    oa = jnp.concatenate([oa_ref[0], oah_ref[0]], axis=0).astype(F32)
    inv = lax.rsqrt(jnp.mean(oa * oa, axis=-1, keepdims=True) + EPS)
    mixed = (_dot((oa * inv * ga_ref[...]).astype(BF16), wa_ref[...])
             + _dot(jnp.concatenate([os_ref[...], osh_ref[...]], axis=0), ws_ref[...]))
    o_ref[0] = x_ref[0] + g1_ref[0] * mixed[:tm]
    x1_halo = xh_ref[0] + g1_ref[0] * mixed[tm:]

    hn = _norm_mod(o_ref[0], g, sh, sc)
    for slab in range(n_slab):
        lanes = slice(slab * LANES, (slab + 1) * LANES)
        for s in range(SUBLANES):
            perm_ref[slab, interleaved_rows(s), :] = hn[natural_rows(s), lanes]
    for slab in range(n_slab):
        h_ref[:tm, slab * LANES:(slab + 1) * LANES] = perm_ref[slab].astype(BF16)
    keep = (pl.program_id(1) > 0).astype(F32)
    h_ref[tm:, :] = (_norm_mod(x1_halo, g, sh, sc) * keep).astype(BF16)
    sublane = lax.broadcasted_iota(jnp.int32, (SUBLANES, FFN_CK), 0)

    def chunk(start):
        if not isinstance(start, int):
            start = pl.multiple_of(start, FFN_CK)
        return pl.ds(start, FFN_CK)

    def cols(c, part):
        return chunk(part * d_ff + c * FFN_CK)

    def wrapped(group, before):
        return jnp.where(sublane == 0, before, pltpu.roll(group, 1, axis=0))

    def up_proj(c, slot):
        for part in range(2):
            r = _dot(h_ref[...], wup_ref[:, cols(c, part)])
            pad = (CONV_WIDTH - 1) * SUBLANES
            up_ref[slot, part, pad:, :] = r[:tm]
            for back in range(1, CONV_WIDTH):
                up_ref[slot, part, pad - back * SUBLANES:pad - (back - 1) * SUBLANES, :] = wrapped(
                    r[tm - back * SUBLANES:tm - (back - 1) * SUBLANES],
                    r[tm + FFN_HALO - back:tm + FFN_HALO - back + 1])

    def conv(c, slot, part):
        out = cb_ref[:, cols(c, part)]
        for i in range(CONV_WIDTH):
            off = i * SUBLANES
            out = out + up_ref[slot, part, off:off + tm, :] * cw_ref[i:i + 1, cols(c, part)]
        return out

    def activate(c, slot):
        a_ref[:, chunk(c * FFN_CK)] = (
            jax.nn.gelu(conv(c, slot, 1)) * conv(c, slot, 0)).astype(BF16)

    for c in range(n_chunk):
        up_proj(c, c % 2)
        activate(c, c % 2)
    y = _dot(a_ref[...], wdn_ref[...])
    for slab in range(n_slab):
        perm_ref[slab] = y[:, slab * LANES:(slab + 1) * LANES]
    for slab in range(n_slab):
        lanes = slice(slab * LANES, (slab + 1) * LANES)
        for s in range(SUBLANES):
            o_ref[0, natural_rows(s), lanes] += (
                g2_ref[0, :, lanes] * perm_ref[slab, interleaved_rows(s), :])


def _mix_ffn(x, o_attn, o_ssm2, ga, w_out, li, g1, sh, sc, g2, g, wup, cw, cb, wdn):
    b, l, d = x.shape
    aw = o_attn.shape[-1]
    sw = o_ssm2.shape[-1] // b
    d_ff = wdn.shape[1]
    assert aw == sw
    w_half = lambda half: pl.BlockSpec((None, aw, d), lambda bi, i: (li, half, 0),
                                       pipeline_mode=pl.Buffered(1))
    tm = min(FFN_TM, l)
    halo_per_tile = tm // FFN_HALO
    halo = lambda i: jnp.maximum(i * halo_per_tile - 1, 0)
    bvec = pl.BlockSpec((1, 1, d), lambda bi, i: (bi, 0, 0))
    const = lambda a: pl.BlockSpec(a.shape, lambda bi, i: (0,) * a.ndim,
                                   pipeline_mode=pl.Buffered(1))
    return pl.pallas_call(
        functools.partial(_mix_ffn_kernel, d_ff=d_ff),
        grid=(b, l // tm),
        in_specs=[
            pl.BlockSpec((1, tm, d), lambda bi, i: (bi, i, 0)),
            pl.BlockSpec((1, FFN_HALO, d), lambda bi, i: (bi, halo(i), 0)),
            pl.BlockSpec((1, tm, aw), lambda bi, i: (bi, i, 0)),
            pl.BlockSpec((1, FFN_HALO, aw), lambda bi, i: (bi, halo(i), 0)),
            pl.BlockSpec((tm, sw), lambda bi, i: (i, bi)),
            pl.BlockSpec((FFN_HALO, sw), lambda bi, i: (halo(i), bi)),
            const(ga), w_half(0), w_half(1), bvec,
            bvec, bvec, bvec, const(g), _layer_block(wup, li), const(cw), const(cb),
            _layer_block(wdn, li),
        ],
        out_specs=pl.BlockSpec((1, tm, d), lambda bi, i: (bi, i, 0)),
        out_shape=jax.ShapeDtypeStruct(x.shape, F32),
        scratch_shapes=[pltpu.VMEM((d // LANES, tm, LANES), F32),
                        pltpu.VMEM((tm + FFN_HALO, d), BF16),
                        pltpu.VMEM((2, 2, tm + (CONV_WIDTH - 1) * SUBLANES, FFN_CK), F32),
                        pltpu.VMEM((tm, d_ff), BF16)],
        compiler_params=_cparams(("parallel", "parallel")),
        name="mix_ffn",
    )(x, x, o_attn, o_attn, o_ssm2, o_ssm2, ga, w_out, w_out, g1, sh, sc, g2, g, wup, cw, cb,
      wdn)


def kernel(x, c, ada_w, ada_b, norm1_g, w_in, q_norm_g, k_norm_g, ssm_a_re, ssm_a_im,
           ssm_log_dt, ssm_b_re, ssm_b_im, ssm_c_re, ssm_c_im, ssm_d, glu_w, glu_b,
           attn_out_g, ssm_out_g, w_out, norm2_g, ffn_w_up, ffn_conv_w, ffn_conv_b,
           ffn_w_down):
    b, l, d = x.shape
    depth = ada_w.shape[0]
    aw = attn_out_g.shape[-1]
    sw = ssm_out_g.shape[-1]
    n_heads = aw // HEAD_DIM
    assert b == SUBLANES and aw % LANES == 0 and sw % LANES == 0
    assert l % ATT_T == 0 and l % SSM_T == 0

    mod = _ada_mod(c, ada_w, ada_b)
    head = jnp.arange(aw) // HEAD_DIM
    ones_blk = (head[:MXU_TILE, None] == head[None, :MXU_TILE]).astype(BF16)
    inv_sqrt = LOG2_E / math.sqrt(HEAD_DIM)

    w_in_bf, w_out_bf = w_in.astype(BF16), w_out.astype(BF16)
    w_up_bf, w_down_bf = ffn_w_up.astype(BF16), ffn_w_down.astype(BF16)
    prm = jax.vmap(_ssm_params)(ssm_a_re, ssm_a_im, ssm_log_dt, ssm_b_re, ssm_b_im, ssm_c_re,
                                ssm_c_im, ssm_d, glu_w, glu_b, ssm_out_g)
    gq_all = jnp.tile(q_norm_g * inv_sqrt, (1, n_heads))
    gk_all = jnp.tile(k_norm_g, (1, n_heads))

    for li in range(depth):
        m = mod[li].reshape(b, N_MOD, 1, d)
        sh1, sc1, g1, sh2, sc2, g2 = (m[:, i] for i in range(N_MOD))
        q, k, v, u2 = _in_proj(x, sh1, sc1, norm1_g[li].reshape(1, d), w_in_bf, li,
                               gq_all[li].reshape(1, aw), gk_all[li].reshape(1, aw),
                               ones_blk, aw, sw)
        o_attn = _attn(q, k, v)
        o_ssm = _ssm(u2, prm, li, b)
        x = _mix_ffn(x, o_attn, o_ssm, attn_out_g[li].reshape(1, aw), w_out_bf, li, g1,
                     sh2, sc2, g2, norm2_g[li].reshape(1, d), w_up_bf,
                     ffn_conv_w[li], ffn_conv_b[li].reshape(1, -1), w_down_bf)
    return x
```

```python
import functools
import math

import jax
import jax.numpy as jnp
from jax import lax
from jax.experimental import pallas as pl
from jax.experimental.pallas import tpu as pltpu

F32 = jnp.float32
BF16 = jnp.bfloat16

EPS = 1e-6
HEAD_DIM = 64
SSM_GROUP = 16
SSM_STATE = 64
CONV_WIDTH = 3
N_MOD = 6

LANES = 128
SUBLANES = 8
MXU_TILE = 256
VMEM_LIMIT_BYTES = 56 * 1024 * 1024

ADA_TN = 1536
PROJ_TM = 512
ATT_T = 128
ATT_QS = 2048
ATT_WINDOW = 3
ATT_GROUP = 16
ATT_TAIL_ROWS = 48
SSM_T = 128
FFN_TM = 512
FFN_CK = 256
FFN_HALO = 16

UNDERFLOW_BOUND = 152.0
LOG2_E = 1.4426950408889634


def _cparams(sem):
    return pltpu.CompilerParams(dimension_semantics=sem,
                                vmem_limit_bytes=VMEM_LIMIT_BYTES)


def _dot(a, b):
    return jnp.dot(a, b, preferred_element_type=F32)


def _ada_kernel(c_ref, w_ref, b_ref, o_ref):
    c = c_ref[...]
    c_act = c * jax.nn.sigmoid(c)
    c_hi = c_act.astype(BF16)
    c_lo = (c_act - c_hi.astype(F32)).astype(BF16)
    w = w_ref[0]
    w_hi = w.astype(BF16)
    w_lo = (w - w_hi.astype(F32)).astype(BF16)
    acc = _dot(c_hi, w_hi) + (_dot(c_hi, w_lo) + _dot(c_lo, w_hi))
    o_ref[0] = acc + b_ref[0]


def _ada_mod(c, ada_w, ada_b):
    depth, d, n = ada_w.shape
    b = c.shape[0]
    return pl.pallas_call(
        _ada_kernel,
        grid=(depth, n // ADA_TN),
        in_specs=[
            pl.BlockSpec((b, d), lambda l, j: (0, 0)),
            pl.BlockSpec((1, d, ADA_TN), lambda l, j: (l, 0, j)),
            pl.BlockSpec((1, 1, ADA_TN), lambda l, j: (l, 0, j)),
        ],
        out_specs=pl.BlockSpec((1, b, ADA_TN), lambda l, j: (l, 0, j)),
        out_shape=jax.ShapeDtypeStruct((depth, b, n), F32),
        compiler_params=_cparams(("parallel", "parallel")),
        name="ada_mod",
    )(c, ada_w, ada_b.reshape(depth, 1, n))


def _norm_mod(xf, g, sh, sc):
    inv = lax.rsqrt(jnp.mean(xf * xf, axis=-1, keepdims=True) + EPS)
    return xf * inv * (g * (1.0 + sc)) + sh


def _in_proj_kernel(x_ref, sh_ref, sc_ref, g_ref, w_ref, gq_ref, gk_ref, ones_ref,
                    q_ref, k_ref, v_ref, u_ref, *, aw):
    hb = _norm_mod(x_ref[0], g_ref[...], sh_ref[0], sc_ref[0]).astype(BF16)
    n_pair = aw // LANES

    def head_norm(p, g):
        sq = (p * p).astype(BF16)
        ms = jnp.concatenate([_dot(sq[:, c:c + MXU_TILE], ones_ref[...])
                              for c in range(0, aw, MXU_TILE)], axis=1) * (1.0 / HEAD_DIM)
        return p * lax.rsqrt(ms + EPS) * g

    q = head_norm(_dot(hb, w_ref[:, 0:aw]), gq_ref[...]).astype(BF16)
    for hp in range(n_pair):
        q_ref[0, hp] = q[:, hp * LANES:(hp + 1) * LANES]
    k = head_norm(_dot(hb, w_ref[:, aw:2 * aw]), gk_ref[...]).astype(BF16)
    for hp in range(n_pair):
        k_ref[0, hp] = k[:, hp * LANES:(hp + 1) * LANES]
    v = _dot(hb, w_ref[:, 2 * aw:3 * aw]).astype(BF16)
    for hp in range(n_pair):
        v_ref[0, hp] = v[:, hp * LANES:(hp + 1) * LANES]
    u_ref[...] = _dot(hb, w_ref[:, 3 * aw:]).astype(BF16)


def _layer_block(stacked, li, index_args=2):
    zeros = (0,) * (stacked.ndim - 1)
    if index_args == 1:
        return pl.BlockSpec((None,) + stacked.shape[1:], lambda i: (li,) + zeros)
    return pl.BlockSpec((None,) + stacked.shape[1:], lambda bi, i: (li,) + zeros,
                        pipeline_mode=pl.Buffered(1))


def _in_proj(x, sh, sc, g, w_in_bf, li, gq, gk, ones_blk, aw, sw):
    b, l, d = x.shape
    tm = min(PROJ_TM, l)
    n_pair = aw // LANES
    qkv_shape = jax.ShapeDtypeStruct((b, n_pair, l, LANES), BF16)
    qkv_spec = pl.BlockSpec((1, n_pair, tm, LANES), lambda bi, i: (bi, 0, i, 0))
    vec = lambda n: pl.BlockSpec((1, n), lambda bi, i: (0, 0))
    bvec = pl.BlockSpec((1, 1, d), lambda bi, i: (bi, 0, 0))
    return pl.pallas_call(
        functools.partial(_in_proj_kernel, aw=aw),
        grid=(b, l // tm),
        in_specs=[
            pl.BlockSpec((1, tm, d), lambda bi, i: (bi, i, 0)),
            bvec, bvec, vec(d),
            _layer_block(w_in_bf, li),
            vec(aw), vec(aw),
            pl.BlockSpec(ones_blk.shape, lambda bi, i: (0, 0)),
        ],
        out_specs=[qkv_spec, qkv_spec, qkv_spec,
                   pl.BlockSpec((tm, sw), lambda bi, i: (i, bi))],
        out_shape=[qkv_shape, qkv_shape, qkv_shape,
                   jax.ShapeDtypeStruct((l, b * sw), BF16)],
        compiler_params=_cparams(("parallel", "parallel")),
        name="in_proj",
    )(x, sh, sc, g, w_in_bf, gq, gk, ones_blk)


def _attn_kernel(q_ref, k_ref, v_ref, o_ref, acc_ref, carry_ref, next_ref, partial_ref,
                 bound_ref):
    t = ATT_T
    n_sub = q_ref.shape[2] // t
    base = pl.program_id(2) * n_sub
    lane = lax.broadcasted_iota(jnp.int32, (1, LANES), 1)
    first = lane < HEAD_DIM
    row = lax.broadcasted_iota(jnp.int32, (t, t), 0)
    col = lax.broadcasted_iota(jnp.int32, (t, t), 1)
    later = jnp.where(row > col, 1.0, 0.0).astype(BF16)
    strict = col < row

    def split_heads(ref, j):
        blk = ref[0, 0, pl.ds(pl.multiple_of(j * t, t), t), :]
        zero = jnp.zeros_like(blk)
        return jnp.where(first, blk, zero), jnp.where(first, zero, blk)

    def pad_rows(x):
        if x.shape[0] == t:
            return x
        return jnp.concatenate([x, jnp.zeros((t - x.shape[0],) + x.shape[1:], x.dtype)], axis=0)

    def blocks(items, carry_in=None, row_keep=None):
        zz = [lax.dot_general(q2[:rows], jnp.concatenate(split_heads(k_ref, j), axis=0),
                              (((1,), (1,)), ((), ())),
                              preferred_element_type=F32) for q2, j, _, rows in items]
        logb, hilo, sums = [], [], []
        for (_, _, depth, _), zz_i in zip(items, zz):
            for h in range(2):
                z = zz_i[:, h * t:(h + 1) * t]
                neg_abs = lax.bitcast_convert_type(
                    lax.bitcast_convert_type(z, jnp.uint32) | jnp.uint32(0x80000000), F32)
                s = jnp.maximum(z, 0.0) + jnp.log2(1.0 + jnp.exp2(neg_abs))
                sm = jnp.where(strict, s, 0.0) if depth == 0 else s
                if row_keep is not None:
                    sm = sm * row_keep
                hi = sm.astype(BF16)
                lo = (sm - hi.astype(F32)).astype(BF16)
                logb.append(z - s)
                hilo.append((hi, lo))
                sums.append(jnp.sum(sm, axis=-1, keepdims=True))
        within = [_dot(hi, later) + _dot(lo, later) for hi, lo in hilo]
        ws, carries = [], []
        for n, (_, _, depth, rows) in enumerate(items):
            carry = carries[-1] if depth > 0 else carry_in
            w_heads, after = [], []
            for h in range(2):
                arg = logb[2 * n + h] - within[2 * n + h]
                if carry is not None:
                    arg = arg - carry[h][:rows]
                w = jnp.exp2(arg)
                if depth == 0:
                    w = jnp.where(strict, w, 0.0)
                if row_keep is not None:
                    w = w * row_keep
                w_heads.append(w.astype(BF16))
                added = pad_rows(sums[2 * n + h])
                after.append(added if carry is None else carry[h] + added)
            ws.append(w_heads)
            carries.append(after)
        pvs = []
        for (w0, w1), (_, j, _, _) in zip(ws, items):
            v0, v1 = split_heads(v_ref, j)
            pvs.append(pad_rows(_dot(w0, v0) + _dot(w1, v1)))
        return pvs, carries

    def first_blocks(depths):
        for g in range(0, n_sub, ATT_GROUP):
            first_blocks_of(depths, range(g, min(g + ATT_GROUP, n_sub)))

    def first_blocks_of(depths, group):
        items, owner = [], []
        for k in group:
            q2 = q_ref[0, 0, k * t:(k + 1) * t, :]
            for d in range(depths[k]):
                partial = depths[k] == ATT_WINDOW and d == ATT_WINDOW - 1
                items.append((q2, base + k - d, d, ATT_TAIL_ROWS if partial else t))
                owner.append(k)
        pvs, carries = blocks(items)
        for k in group:
            mine = [n for n, o in enumerate(owner) if o == k]
            acc = pvs[mine[0]]
            for n in mine[1:]:
                acc = acc + pvs[n]
            carry = carries[mine[-1]]
            acc_ref[k] = acc
            for h in range(2):
                carry_ref[k, h] = carry[h]
            last = base + k - depths[k] + 1
            partial = depths[k] == ATT_WINDOW
            next_ref[k] = last if partial else last - 1
            partial_ref[k] = last if partial else -1
            bound_ref[k] = jnp.min(jnp.minimum(carry[0], carry[1]))

    @pl.when(base >= ATT_WINDOW - 1)
    def _():
        first_blocks([ATT_WINDOW] * n_sub)

    @pl.when(base < ATT_WINDOW - 1)
    def _():
        first_blocks([ATT_WINDOW if k >= ATT_WINDOW - 1 else 1 for k in range(n_sub)])

    tail_rows = lax.broadcasted_iota(jnp.int32, (t, 1), 0) < ATT_TAIL_ROWS

    def finish(k, _):
        rows = pl.ds(pl.multiple_of(k * t, t), t)
        q2 = q_ref[0, 0, rows, :]
        partial_j = partial_ref[k]

        def cond(state):
            j, bound = state
            return jnp.logical_and(j >= 0, bound < UNDERFLOW_BOUND)

        def body(state):
            j, _ = state
            carry = [carry_ref[k, 0], carry_ref[k, 1]]
            keep = jnp.where(jnp.logical_and(j == partial_j, tail_rows), 0.0, 1.0)
            (pv,), (carry,) = blocks([(q2, j, -1, t)], carry, keep)
            acc_ref[k] += pv
            for h in range(2):
                carry_ref[k, h] = carry[h]
            return j - 1, jnp.min(jnp.minimum(carry[0], carry[1]))

        lax.while_loop(cond, body, (next_ref[k], bound_ref[k]))
        o_ref[0, rows, :] = acc_ref[k].astype(o_ref.dtype)
        return 0

    lax.fori_loop(0, n_sub, finish, 0)


def _attn(q, k, v):
    b, n_pair, l, _ = q.shape
    t = ATT_T
    qs = min(ATT_QS, l)
    kv_spec = pl.BlockSpec((1, 1, l, LANES), lambda bi, hp, i: (bi, hp, 0, 0))
    return pl.pallas_call(
        _attn_kernel,
        grid=(b, n_pair, l // qs),
        in_specs=[pl.BlockSpec((1, 1, qs, LANES), lambda bi, hp, i: (bi, hp, i, 0)),
                  kv_spec, kv_spec],
        out_specs=pl.BlockSpec((1, qs, LANES), lambda bi, hp, i: (bi, i, hp)),
        out_shape=jax.ShapeDtypeStruct((b, l, n_pair * LANES), BF16),
        scratch_shapes=[pltpu.VMEM((qs // t, t, LANES), F32),
                        pltpu.VMEM((qs // t, 2, t, 1), F32),
                        pltpu.SMEM((qs // t,), jnp.int32),
                        pltpu.SMEM((qs // t,), jnp.int32),
                        pltpu.SMEM((qs // t,), F32)],
        compiler_params=_cparams(("parallel", "parallel", "parallel")),
        name="attn",
    )(q, k, v)


def _ssm_kernel(u_ref, wb_ref, are_ref, aim_ref, wcre_ref, wcim_ref, d_ref,
                wglu_ref, bglu_ref, g_ref, o_ref, sre_ref, sim_ref, st_ref, y_ref,
                *, n_oct, batch):
    n_tok = u_ref.shape[0]
    sw = u_ref.shape[1] // batch
    oct_states = sre_ref.shape[2]

    @pl.when(pl.program_id(0) == 0)
    def _():
        st_ref[...] = jnp.zeros_like(st_ref)

    for bi in range(batch):
        for o in range(n_oct):
            c0 = bi * sw + o * LANES
            y_ref[o, pl.ds(bi, n_tok, stride=batch), :] = u_ref[:, c0:c0 + LANES].astype(F32)
    def input_proj(o):
        bu = _dot(y_ref[o].astype(BF16), wb_ref[o])
        sre_ref[o] = bu[:, :oct_states]
        sim_ref[o] = bu[:, oct_states:]

    def recurrence(o):
        cols = slice(o * oct_states, (o + 1) * oct_states)
        are = jnp.broadcast_to(are_ref[:, cols], (batch, oct_states))
        aim = jnp.broadcast_to(aim_ref[:, cols], (batch, oct_states))
        s_re, s_im = st_ref[0, :, cols], st_ref[1, :, cols]
        for t in range(n_tok):
            r = slice(t * batch, (t + 1) * batch)
            s_re, s_im = (are * s_re - aim * s_im + sre_ref[o, r, :],
                          are * s_im + aim * s_re + sim_ref[o, r, :])
            sre_ref[o, r, :] = s_re
            sim_ref[o, r, :] = s_im
        st_ref[0, :, cols] = s_re
        st_ref[1, :, cols] = s_im

    def output_proj(o):
        lanes = slice(o * LANES, (o + 1) * LANES)
        y = (_dot(sre_ref[o].astype(BF16), wcre_ref[o])
             + _dot(sim_ref[o].astype(BF16), wcim_ref[o])
             + d_ref[:, lanes] * y_ref[o])
        y = jax.nn.gelu(y)
        gate = jax.nn.sigmoid(_dot(y.astype(BF16), wglu_ref[o]) + bglu_ref[:, lanes])
        out = y * gate
        y_ref[o] = out
        return jnp.sum(out * out, axis=-1, keepdims=True)

    sumsq = 0.0
    input_proj(0)
    for o in range(n_oct):
        if o + 1 < n_oct:
            input_proj(o + 1)
        recurrence(o)
        if o >= 1:
            sumsq = sumsq + output_proj(o - 1)
    sumsq = sumsq + output_proj(n_oct - 1)
    inv = lax.rsqrt(sumsq * (1.0 / sw) + EPS)
    for o in range(n_oct):
        y_ref[o] = y_ref[o] * inv * g_ref[:, o * LANES:(o + 1) * LANES]
    for bi in range(batch):
        for o in range(n_oct):
            c0 = bi * sw + o * LANES
            o_ref[:, c0:c0 + LANES] = (
                y_ref[o, pl.ds(bi, n_tok, stride=batch), :].astype(o_ref.dtype))


def _ssm(u2, prm, li, batch):
    l, width = u2.shape
    sw = width // batch
    n_oct = sw // LANES
    n_state = prm["are"].shape[-1]
    rows = SSM_T * batch
    full = lambda a: _layer_block(a, li, index_args=1)
    names = ["wb", "are", "aim", "wcre", "wcim", "d", "wglu", "bglu", "g"]
    return pl.pallas_call(
        functools.partial(_ssm_kernel, n_oct=n_oct, batch=batch),
        grid=(l // SSM_T,),
        in_specs=[pl.BlockSpec((SSM_T, width), lambda i: (i, 0))]
                 + [full(prm[n]) for n in names],
        out_specs=pl.BlockSpec((SSM_T, width), lambda i: (i, 0)),
        out_shape=jax.ShapeDtypeStruct((l, width), BF16),
        scratch_shapes=[pltpu.VMEM((n_oct, rows, n_state // n_oct), F32),
                        pltpu.VMEM((n_oct, rows, n_state // n_oct), F32),
                        pltpu.VMEM((2, batch, n_state), F32),
                        pltpu.VMEM((n_oct, rows, LANES), F32)],
        compiler_params=_cparams(("arbitrary",)),
        name="ssm",
    )(u2, *[prm[n] for n in names])


def _ssm_params(a_re, a_im, log_dt, b_re, b_im, c_re, c_im, d_skip, glu_w, glu_b, out_g):
    g, p, h = b_re.shape
    per_oct = LANES // h
    n_oct = g // per_oct
    dt = jnp.exp(log_dt)[:, None]
    mag = jnp.exp(dt * a_re)
    abar_re = mag * jnp.cos(dt * a_im)
    abar_im = mag * jnp.sin(dt * a_im)
    em_re = abar_re - 1.0
    em_im = abar_im
    den = a_re * a_re + a_im * a_im
    f_re = (em_re * a_re + em_im * a_im) / den
    f_im = (em_im * a_re - em_re * a_im) / den
    bb_re = f_re[..., None] * b_re - f_im[..., None] * b_im
    bb_im = f_re[..., None] * b_im + f_im[..., None] * b_re
    eye = jnp.eye(per_oct, dtype=F32)

    def in_blockdiag(bb):
        bb = bb.reshape(n_oct, per_oct, p, h)
        m = jnp.einsum("ogph,gk->oghkp", bb, eye)
        return m.reshape(n_oct, per_oct * h, per_oct * p)

    def out_blockdiag(c):
        c = c.reshape(n_oct, per_oct, h, p)
        m = jnp.einsum("oghp,gk->ogpkh", c, eye)
        return m.reshape(n_oct, per_oct * p, per_oct * h)

    wb = jnp.concatenate([in_blockdiag(bb_re), in_blockdiag(bb_im)], axis=-1).astype(BF16)
    wglu = jnp.einsum("oghk,gf->oghfk", glu_w.reshape(n_oct, per_oct, h, h), eye)
    wglu = wglu.reshape(n_oct, per_oct * h, per_oct * h).astype(BF16)
    return {
        "wb": wb,
        "are": abar_re.reshape(1, g * p), "aim": abar_im.reshape(1, g * p),
        "wcre": out_blockdiag(c_re).astype(BF16),
        "wcim": out_blockdiag(-c_im).astype(BF16),
        "d": d_skip.reshape(1, g * h), "wglu": wglu,
        "bglu": glu_b.reshape(1, g * h), "g": out_g.reshape(1, g * h),
    }


def _mix_ffn_kernel(x_ref, xh_ref, oa_ref, oah_ref, os_ref, osh_ref, ga_ref, wa_ref, ws_ref,
                    g1_ref, sh_ref, sc_ref, g2_ref, g_ref, wup_ref, cw_ref, cb_ref, wdn_ref,
                    o_ref, perm_ref, h_ref, up_ref, a_ref, *, d_ff):
    tm, d = x_ref.shape[1], x_ref.shape[2]
    n_grp = tm // SUBLANES
    n_slab = d // LANES
    n_chunk = d_ff // FFN_CK
    g, sh, sc = g_ref[...], sh_ref[0], sc_ref[0]

    def natural_rows(s):
        return slice(s * n_grp, (s + 1) * n_grp)

    def interleaved_rows(s):
        return pl.ds(s, n_grp, stride=SUBLANES)

    oa = jnp.concatenate([oa_ref[0], oah_ref[0]], axis=0).astype(F32)
    inv = lax.rsqrt(jnp.mean(oa * oa, axis=-1, keepdims=True) + EPS)
    mixed = (_dot((oa * inv * ga_ref[...]).astype(BF16), wa_ref[...])
             + _dot(jnp.concatenate([os_ref[...], osh_ref[...]], axis=0), ws_ref[...]))
    o_ref[0] = x_ref[0] + g1_ref[0] * mixed[:tm]
    x1_halo = xh_ref[0] + g1_ref[0] * mixed[tm:]

    hn = _norm_mod(o_ref[0], g, sh, sc)
    for slab in range(n_slab):
        lanes = slice(slab * LANES, (slab + 1) * LANES)
        for s in range(SUBLANES):
            perm_ref[slab, interleaved_rows(s), :] = hn[natural_rows(s), lanes]
    for slab in range(n_slab):
        h_ref[:tm, slab * LANES:(slab + 1) * LANES] = perm_ref[slab].astype(BF16)
    keep = (pl.program_id(1) > 0).astype(F32)
    h_ref[tm:, :] = (_norm_mod(x1_halo, g, sh, sc) * keep).astype(BF16)
    sublane = lax.broadcasted_iota(jnp.int32, (SUBLANES, FFN_CK), 0)

    def chunk(start):
        if not isinstance(start, int):
            start = pl.multiple_of(start, FFN_CK)
        return pl.ds(start, FFN_CK)

    def cols(c, part):
        return chunk(part * d_ff + c * FFN_CK)

    def wrapped(group, before):
        return jnp.where(sublane == 0, before, pltpu.roll(group, 1, axis=0))

    def up_proj(c, slot):
        for part in range(2):
            r = _dot(h_ref[...], wup_ref[:, cols(c, part)])
            pad = (CONV_WIDTH - 1) * SUBLANES
            up_ref[slot, part, pad:, :] = r[:tm]
            for back in range(1, CONV_WIDTH):
                up_ref[slot, part, pad - back * SUBLANES:pad - (back - 1) * SUBLANES, :] = wrapped(
                    r[tm - back * SUBLANES:tm - (back - 1) * SUBLANES],
                    r[tm + FFN_HALO - back:tm + FFN_HALO - back + 1])

    def conv(c, slot, part):
        out = cb_ref[:, cols(c, part)]
        for i in range(CONV_WIDTH):
            off = i * SUBLANES
            out = out + up_ref[slot, part, off:off + tm, :] * cw_ref[i:i + 1, cols(c, part)]
        return out

    def activate(c, slot):
        a_ref[:, chunk(c * FFN_CK)] = (
            jax.nn.gelu(conv(c, slot, 1)) * conv(c, slot, 0)).astype(BF16)

    for c in range(n_chunk):
        up_proj(c, c % 2)
        activate(c, c % 2)
    y = _dot(a_ref[...], wdn_ref[...])
    for slab in range(n_slab):
        perm_ref[slab] = y[:, slab * LANES:(slab + 1) * LANES]
    for slab in range(n_slab):
        lanes = slice(slab * LANES, (slab + 1) * LANES)
        for s in range(SUBLANES):
            o_ref[0, natural_rows(s), lanes] += (
                g2_ref[0, :, lanes] * perm_ref[slab, interleaved_rows(s), :])


def _mix_ffn(x, o_attn, o_ssm2, ga, w_out, li, g1, sh, sc, g2, g, wup, cw, cb, wdn):
    b, l, d = x.shape
    aw = o_attn.shape[-1]
    sw = o_ssm2.shape[-1] // b
    d_ff = wdn.shape[1]
    assert aw == sw
    w_half = lambda half: pl.BlockSpec((None, aw, d), lambda bi, i: (li, half, 0),
                                       pipeline_mode=pl.Buffered(1))
    tm = min(FFN_TM, l)
    halo_per_tile = tm // FFN_HALO
    halo = lambda i: jnp.maximum(i * halo_per_tile - 1, 0)
    bvec = pl.BlockSpec((1, 1, d), lambda bi, i: (bi, 0, 0))
    const = lambda a: pl.BlockSpec(a.shape, lambda bi, i: (0,) * a.ndim,
                                   pipeline_mode=pl.Buffered(1))
    return pl.pallas_call(
        functools.partial(_mix_ffn_kernel, d_ff=d_ff),
        grid=(b, l // tm),
        in_specs=[
            pl.BlockSpec((1, tm, d), lambda bi, i: (bi, i, 0)),
            pl.BlockSpec((1, FFN_HALO, d), lambda bi, i: (bi, halo(i), 0)),
            pl.BlockSpec((1, tm, aw), lambda bi, i: (bi, i, 0)),
            pl.BlockSpec((1, FFN_HALO, aw), lambda bi, i: (bi, halo(i), 0)),
            pl.BlockSpec((tm, sw), lambda bi, i: (i, bi)),
            pl.BlockSpec((FFN_HALO, sw), lambda bi, i: (halo(i), bi)),
            const(ga), w_half(0), w_half(1), bvec,
            bvec, bvec, bvec, const(g), _layer_block(wup, li), const(cw), const(cb),
            _layer_block(wdn, li),
        ],
        out_specs=pl.BlockSpec((1, tm, d), lambda bi, i: (bi, i, 0)),
        out_shape=jax.ShapeDtypeStruct(x.shape, F32),
        scratch_shapes=[pltpu.VMEM((d // LANES, tm, LANES), F32),
                        pltpu.VMEM((tm + FFN_HALO, d), BF16),
                        pltpu.VMEM((2, 2, tm + (CONV_WIDTH - 1) * SUBLANES, FFN_CK), F32),
                        pltpu.VMEM((tm, d_ff), BF16)],
        compiler_params=_cparams(("parallel", "parallel")),
        name="mix_ffn",
    )(x, x, o_attn, o_attn, o_ssm2, o_ssm2, ga, w_out, w_out, g1, sh, sc, g2, g, wup, cw, cb,
      wdn)


def kernel(x, c, ada_w, ada_b, norm1_g, w_in, q_norm_g, k_norm_g, ssm_a_re, ssm_a_im,
           ssm_log_dt, ssm_b_re, ssm_b_im, ssm_c_re, ssm_c_im, ssm_d, glu_w, glu_b,
           attn_out_g, ssm_out_g, w_out, norm2_g, ffn_w_up, ffn_conv_w, ffn_conv_b,
           ffn_w_down):
    b, l, d = x.shape
    depth = ada_w.shape[0]
    aw = attn_out_g.shape[-1]
    sw = ssm_out_g.shape[-1]
    n_heads = aw // HEAD_DIM
    assert b == SUBLANES and aw % LANES == 0 and sw % LANES == 0
    assert l % ATT_T == 0 and l % SSM_T == 0

    mod = _ada_mod(c, ada_w, ada_b)
    head = jnp.arange(aw) // HEAD_DIM
    ones_blk = (head[:MXU_TILE, None] == head[None, :MXU_TILE]).astype(BF16)
    inv_sqrt = LOG2_E / math.sqrt(HEAD_DIM)

    w_in_bf, w_out_bf = w_in.astype(BF16), w_out.astype(BF16)
    w_up_bf, w_down_bf = ffn_w_up.astype(BF16), ffn_w_down.astype(BF16)
    prm = jax.vmap(_ssm_params)(ssm_a_re, ssm_a_im, ssm_log_dt, ssm_b_re, ssm_b_im, ssm_c_re,
                                ssm_c_im, ssm_d, glu_w, glu_b, ssm_out_g)
    gq_all = jnp.tile(q_norm_g * inv_sqrt, (1, n_heads))
    gk_all = jnp.tile(k_norm_g, (1, n_heads))

    for li in range(depth):
        m = mod[li].reshape(b, N_MOD, 1, d)
        sh1, sc1, g1, sh2, sc2, g2 = (m[:, i] for i in range(N_MOD))
        q, k, v, u2 = _in_proj(x, sh1, sc1, norm1_g[li].reshape(1, d), w_in_bf, li,
                               gq_all[li].reshape(1, aw), gk_all[li].reshape(1, aw),
                               ones_blk, aw, sw)
        o_attn = _attn(q, k, v)
        o_ssm = _ssm(u2, prm, li, b)
        x = _mix_ffn(x, o_attn, o_ssm, attn_out_g[li].reshape(1, aw), w_out_bf, li, g1,
                     sh2, sc2, g2, norm2_g[li].reshape(1, d), w_up_bf,
                     ffn_conv_w[li], ffn_conv_b[li].reshape(1, -1), w_down_bf)
    return x
```

```python
import functools
import math

import jax
import jax.numpy as jnp
from jax import lax
from jax.experimental import pallas as pl
from jax.experimental.pallas import tpu as pltpu

F32 = jnp.float32
BF16 = jnp.bfloat16

EPS = 1e-6
HEAD_DIM = 64
SSM_GROUP = 16
SSM_STATE = 64
CONV_WIDTH = 3
N_MOD = 6

LANES = 128
SUBLANES = 8
MXU_TILE = 256
VMEM_LIMIT_BYTES = 56 * 1024 * 1024

ADA_TN = 1536
PROJ_TM = 512
ATT_T = 128
ATT_QS = 2048
ATT_WINDOW = 3
ATT_GROUP = 16
SSM_T = 128
FFN_TM = 512
FFN_CK = 256
FFN_HALO = 16

UNDERFLOW_BOUND = 152.0
LOG2_E = 1.4426950408889634


def _cparams(sem):
    return pltpu.CompilerParams(dimension_semantics=sem,
                                vmem_limit_bytes=VMEM_LIMIT_BYTES)


def _dot(a, b):
    return jnp.dot(a, b, preferred_element_type=F32)


def _ada_kernel(c_ref, w_ref, b_ref, o_ref):
    c = c_ref[...]
    c_act = c * jax.nn.sigmoid(c)
    c_hi = c_act.astype(BF16)
    c_lo = (c_act - c_hi.astype(F32)).astype(BF16)
    w = w_ref[0]
    w_hi = w.astype(BF16)
    w_lo = (w - w_hi.astype(F32)).astype(BF16)
    acc = _dot(c_hi, w_hi) + (_dot(c_hi, w_lo) + _dot(c_lo, w_hi))
    o_ref[0] = acc + b_ref[0]


def _ada_mod(c, ada_w, ada_b):
    depth, d, n = ada_w.shape
    b = c.shape[0]
    return pl.pallas_call(
        _ada_kernel,
        grid=(depth, n // ADA_TN),
        in_specs=[
            pl.BlockSpec((b, d), lambda l, j: (0, 0)),
            pl.BlockSpec((1, d, ADA_TN), lambda l, j: (l, 0, j)),
            pl.BlockSpec((1, 1, ADA_TN), lambda l, j: (l, 0, j)),
        ],
        out_specs=pl.BlockSpec((1, b, ADA_TN), lambda l, j: (l, 0, j)),
        out_shape=jax.ShapeDtypeStruct((depth, b, n), F32),
        compiler_params=_cparams(("parallel", "parallel")),
        name="ada_mod",
    )(c, ada_w, ada_b.reshape(depth, 1, n))


def _norm_mod(xf, g, sh, sc):
    inv = lax.rsqrt(jnp.mean(xf * xf, axis=-1, keepdims=True) + EPS)
    return xf * inv * (g * (1.0 + sc)) + sh


def _in_proj_kernel(x_ref, sh_ref, sc_ref, g_ref, w_ref, gq_ref, gk_ref, ones_ref,
                    q_ref, k_ref, v_ref, u_ref, *, aw):
    hb = _norm_mod(x_ref[0], g_ref[...], sh_ref[0], sc_ref[0]).astype(BF16)
    n_pair = aw // LANES

    def head_norm(p, g):
        sq = (p * p).astype(BF16)
        ms = jnp.concatenate([_dot(sq[:, c:c + MXU_TILE], ones_ref[...])
                              for c in range(0, aw, MXU_TILE)], axis=1) * (1.0 / HEAD_DIM)
        return p * lax.rsqrt(ms + EPS) * g

    q = head_norm(_dot(hb, w_ref[:, 0:aw]), gq_ref[...]).astype(BF16)
    for hp in range(n_pair):
        q_ref[0, hp] = q[:, hp * LANES:(hp + 1) * LANES]
    k = head_norm(_dot(hb, w_ref[:, aw:2 * aw]), gk_ref[...]).astype(BF16)
    for hp in range(n_pair):
        k_ref[0, hp] = k[:, hp * LANES:(hp + 1) * LANES]
    v = _dot(hb, w_ref[:, 2 * aw:3 * aw]).astype(BF16)
    for hp in range(n_pair):
        v_ref[0, hp] = v[:, hp * LANES:(hp + 1) * LANES]
    u_ref[...] = _dot(hb, w_ref[:, 3 * aw:]).astype(BF16)


def _layer_block(stacked, li, index_args=2):
    zeros = (0,) * (stacked.ndim - 1)
    if index_args == 1:
        return pl.BlockSpec((None,) + stacked.shape[1:], lambda i: (li,) + zeros)
    return pl.BlockSpec((None,) + stacked.shape[1:], lambda bi, i: (li,) + zeros,
                        pipeline_mode=pl.Buffered(1))


def _in_proj(x, sh, sc, g, w_in_bf, li, gq, gk, ones_blk, aw, sw):
    b, l, d = x.shape
    tm = min(PROJ_TM, l)
    n_pair = aw // LANES
    qkv_shape = jax.ShapeDtypeStruct((b, n_pair, l, LANES), BF16)
    qkv_spec = pl.BlockSpec((1, n_pair, tm, LANES), lambda bi, i: (bi, 0, i, 0))
    vec = lambda n: pl.BlockSpec((1, n), lambda bi, i: (0, 0))
    bvec = pl.BlockSpec((1, 1, d), lambda bi, i: (bi, 0, 0))
    return pl.pallas_call(
        functools.partial(_in_proj_kernel, aw=aw),
        grid=(b, l // tm),
        in_specs=[
            pl.BlockSpec((1, tm, d), lambda bi, i: (bi, i, 0)),
            bvec, bvec, vec(d),
            _layer_block(w_in_bf, li),
            vec(aw), vec(aw),
            pl.BlockSpec(ones_blk.shape, lambda bi, i: (0, 0)),
        ],
        out_specs=[qkv_spec, qkv_spec, qkv_spec,
                   pl.BlockSpec((tm, sw), lambda bi, i: (i, bi))],
        out_shape=[qkv_shape, qkv_shape, qkv_shape,
                   jax.ShapeDtypeStruct((l, b * sw), BF16)],
        compiler_params=_cparams(("parallel", "parallel")),
        name="in_proj",
    )(x, sh, sc, g, w_in_bf, gq, gk, ones_blk)


def _attn_kernel(q_ref, k_ref, v_ref, o_ref, acc_ref, carry_ref, next_ref, bound_ref):
    t = ATT_T
    n_sub = q_ref.shape[2] // t
    base = pl.program_id(2) * n_sub
    lane = lax.broadcasted_iota(jnp.int32, (1, LANES), 1)
    first = lane < HEAD_DIM
    row = lax.broadcasted_iota(jnp.int32, (t, t), 0)
    col = lax.broadcasted_iota(jnp.int32, (t, t), 1)
    later = jnp.where(row > col, 1.0, 0.0).astype(BF16)
    strict = col < row

    def split_heads(ref, j):
        blk = ref[0, 0, pl.ds(pl.multiple_of(j * t, t), t), :]
        zero = jnp.zeros_like(blk)
        return jnp.where(first, blk, zero), jnp.where(first, zero, blk)

    def blocks(items, carry_in=None):
        zz = [lax.dot_general(q2, jnp.concatenate(split_heads(k_ref, j), axis=0),
                              (((1,), (1,)), ((), ())),
                              preferred_element_type=F32) for q2, j, _ in items]
        logb, s_bf, sums = [], [], []
        for (_, _, depth), zz_i in zip(items, zz):
            for h in range(2):
                z = zz_i[:, h * t:(h + 1) * t]
                neg_abs = lax.bitcast_convert_type(
                    lax.bitcast_convert_type(z, jnp.uint32) | jnp.uint32(0x80000000), F32)
                s = jnp.maximum(z, 0.0) + jnp.log2(1.0 + jnp.exp2(neg_abs))
                sm = jnp.where(strict, s, 0.0) if depth == 0 else s
                logb.append(z - s)
                s_bf.append(sm.astype(BF16))
                sums.append(jnp.sum(sm, axis=-1, keepdims=True))
        within = [_dot(x, later) for x in s_bf]
        ws, carries = [], []
        for n, (_, _, depth) in enumerate(items):
            carry = carries[-1] if depth > 0 else carry_in
            w_heads, after = [], []
            for h in range(2):
                arg = logb[2 * n + h] - within[2 * n + h]
                if carry is not None:
                    arg = arg - carry[h]
                w = jnp.exp2(arg)
                if depth == 0:
                    w = jnp.where(strict, w, 0.0)
                w_heads.append(w.astype(BF16))
                after.append(sums[2 * n + h] if carry is None else carry[h] + sums[2 * n + h])
            ws.append(w_heads)
            carries.append(after)
        pvs = []
        for (w0, w1), (_, j, _) in zip(ws, items):
            v0, v1 = split_heads(v_ref, j)
            pvs.append(_dot(w0, v0) + _dot(w1, v1))
        return pvs, carries

    def first_blocks(depths):
        for g in range(0, n_sub, ATT_GROUP):
            first_blocks_of(depths, range(g, min(g + ATT_GROUP, n_sub)))

    def first_blocks_of(depths, group):
        items, owner = [], []
        for k in group:
            q2 = q_ref[0, 0, k * t:(k + 1) * t, :]
            for d in range(depths[k]):
                items.append((q2, base + k - d, d))
                owner.append(k)
        pvs, carries = blocks(items)
        for k in group:
            mine = [n for n, o in enumerate(owner) if o == k]
            acc = pvs[mine[0]]
            for n in mine[1:]:
                acc = acc + pvs[n]
            carry = carries[mine[-1]]
            acc_ref[k] = acc
            for h in range(2):
                carry_ref[k, h] = carry[h]
            next_ref[k] = base + k - depths[k]
            bound_ref[k] = jnp.min(jnp.minimum(carry[0], carry[1]))

    @pl.when(base >= ATT_WINDOW - 1)
    def _():
        first_blocks([ATT_WINDOW] * n_sub)

    @pl.when(base < ATT_WINDOW - 1)
    def _():
        first_blocks([ATT_WINDOW if k >= ATT_WINDOW - 1 else 1 for k in range(n_sub)])

    def finish(k, _):
        rows = pl.ds(pl.multiple_of(k * t, t), t)
        q2 = q_ref[0, 0, rows, :]

        def cond(state):
            j, bound = state
            return jnp.logical_and(j >= 0, bound < UNDERFLOW_BOUND)

        def body(state):
            j, _ = state
            carry = [carry_ref[k, 0], carry_ref[k, 1]]
            (pv,), (carry,) = blocks([(q2, j, -1)], carry)
            acc_ref[k] += pv
            for h in range(2):
                carry_ref[k, h] = carry[h]
            return j - 1, jnp.min(jnp.minimum(carry[0], carry[1]))

        lax.while_loop(cond, body, (next_ref[k], bound_ref[k]))
        o_ref[0, rows, :] = acc_ref[k].astype(o_ref.dtype)
        return 0

    lax.fori_loop(0, n_sub, finish, 0)


def _attn(q, k, v):
    b, n_pair, l, _ = q.shape
    t = ATT_T
    qs = min(ATT_QS, l)
    kv_spec = pl.BlockSpec((1, 1, l, LANES), lambda bi, hp, i: (bi, hp, 0, 0))
    return pl.pallas_call(
        _attn_kernel,
        grid=(b, n_pair, l // qs),
        in_specs=[pl.BlockSpec((1, 1, qs, LANES), lambda bi, hp, i: (bi, hp, i, 0)),
                  kv_spec, kv_spec],
        out_specs=pl.BlockSpec((1, qs, LANES), lambda bi, hp, i: (bi, i, hp)),
        out_shape=jax.ShapeDtypeStruct((b, l, n_pair * LANES), BF16),
        scratch_shapes=[pltpu.VMEM((qs // t, t, LANES), F32),
                        pltpu.VMEM((qs // t, 2, t, 1), F32),
                        pltpu.SMEM((qs // t,), jnp.int32),
                        pltpu.SMEM((qs // t,), F32)],
        compiler_params=_cparams(("parallel", "parallel", "parallel")),
        name="attn",
    )(q, k, v)


def _ssm_kernel(u_ref, wb_ref, are_ref, aim_ref, wcre_ref, wcim_ref, d_ref,
                wglu_ref, bglu_ref, g_ref, o_ref, sre_ref, sim_ref, st_ref, y_ref,
                *, n_oct, batch):
    n_tok = u_ref.shape[0]
    sw = u_ref.shape[1] // batch
    oct_states = sre_ref.shape[2]

    @pl.when(pl.program_id(0) == 0)
    def _():
        st_ref[...] = jnp.zeros_like(st_ref)

    for bi in range(batch):
        for o in range(n_oct):
            c0 = bi * sw + o * LANES
            y_ref[o, pl.ds(bi, n_tok, stride=batch), :] = u_ref[:, c0:c0 + LANES].astype(F32)
    def input_proj(o):
        bu = _dot(y_ref[o].astype(BF16), wb_ref[o])
        sre_ref[o] = bu[:, :oct_states]
        sim_ref[o] = bu[:, oct_states:]

    def recurrence(o):
        cols = slice(o * oct_states, (o + 1) * oct_states)
        are = jnp.broadcast_to(are_ref[:, cols], (batch, oct_states))
        aim = jnp.broadcast_to(aim_ref[:, cols], (batch, oct_states))
        s_re, s_im = st_ref[0, :, cols], st_ref[1, :, cols]
        for t in range(n_tok):
            r = slice(t * batch, (t + 1) * batch)
            s_re, s_im = (are * s_re - aim * s_im + sre_ref[o, r, :],
                          are * s_im + aim * s_re + sim_ref[o, r, :])
            sre_ref[o, r, :] = s_re
            sim_ref[o, r, :] = s_im
        st_ref[0, :, cols] = s_re
        st_ref[1, :, cols] = s_im

    def output_proj(o):
        lanes = slice(o * LANES, (o + 1) * LANES)
        y = (_dot(sre_ref[o].astype(BF16), wcre_ref[o])
             + _dot(sim_ref[o].astype(BF16), wcim_ref[o])
             + d_ref[:, lanes] * y_ref[o])
        y = jax.nn.gelu(y)
        gate = jax.nn.sigmoid(_dot(y.astype(BF16), wglu_ref[o]) + bglu_ref[:, lanes])
        out = y * gate
        y_ref[o] = out
        return jnp.sum(out * out, axis=-1, keepdims=True)

    sumsq = 0.0
    input_proj(0)
    for o in range(n_oct):
        if o + 1 < n_oct:
            input_proj(o + 1)
        recurrence(o)
        if o >= 1:
            sumsq = sumsq + output_proj(o - 1)
    sumsq = sumsq + output_proj(n_oct - 1)
    inv = lax.rsqrt(sumsq * (1.0 / sw) + EPS)
    for o in range(n_oct):
        y_ref[o] = y_ref[o] * inv * g_ref[:, o * LANES:(o + 1) * LANES]
    for bi in range(batch):
        for o in range(n_oct):
            c0 = bi * sw + o * LANES
            o_ref[:, c0:c0 + LANES] = (
                y_ref[o, pl.ds(bi, n_tok, stride=batch), :].astype(o_ref.dtype))


def _ssm(u2, prm, li, batch):
    l, width = u2.shape
    sw = width // batch
    n_oct = sw // LANES
    n_state = prm["are"].shape[-1]
    rows = SSM_T * batch
    full = lambda a: _layer_block(a, li, index_args=1)
    names = ["wb", "are", "aim", "wcre", "wcim", "d", "wglu", "bglu", "g"]
    return pl.pallas_call(
        functools.partial(_ssm_kernel, n_oct=n_oct, batch=batch),
        grid=(l // SSM_T,),
        in_specs=[pl.BlockSpec((SSM_T, width), lambda i: (i, 0))]
                 + [full(prm[n]) for n in names],
        out_specs=pl.BlockSpec((SSM_T, width), lambda i: (i, 0)),
        out_shape=jax.ShapeDtypeStruct((l, width), BF16),
        scratch_shapes=[pltpu.VMEM((n_oct, rows, n_state // n_oct), F32),
                        pltpu.VMEM((n_oct, rows, n_state // n_oct), F32),
                        pltpu.VMEM((2, batch, n_state), F32),
                        pltpu.VMEM((n_oct, rows, LANES), F32)],
        compiler_params=_cparams(("arbitrary",)),
        name="ssm",
    )(u2, *[prm[n] for n in names])


def _ssm_params(a_re, a_im, log_dt, b_re, b_im, c_re, c_im, d_skip, glu_w, glu_b, out_g):
    g, p, h = b_re.shape
    per_oct = LANES // h
    n_oct = g // per_oct
    dt = jnp.exp(log_dt)[:, None]
    mag = jnp.exp(dt * a_re)
    abar_re = mag * jnp.cos(dt * a_im)
    abar_im = mag * jnp.sin(dt * a_im)
    em_re = abar_re - 1.0
    em_im = abar_im
    den = a_re * a_re + a_im * a_im
    f_re = (em_re * a_re + em_im * a_im) / den
    f_im = (em_im * a_re - em_re * a_im) / den
    bb_re = f_re[..., None] * b_re - f_im[..., None] * b_im
    bb_im = f_re[..., None] * b_im + f_im[..., None] * b_re
    eye = jnp.eye(per_oct, dtype=F32)

    def in_blockdiag(bb):
        bb = bb.reshape(n_oct, per_oct, p, h)
        m = jnp.einsum("ogph,gk->oghkp", bb, eye)
        return m.reshape(n_oct, per_oct * h, per_oct * p)

    def out_blockdiag(c):
        c = c.reshape(n_oct, per_oct, h, p)
        m = jnp.einsum("oghp,gk->ogpkh", c, eye)
        return m.reshape(n_oct, per_oct * p, per_oct * h)

    wb = jnp.concatenate([in_blockdiag(bb_re), in_blockdiag(bb_im)], axis=-1).astype(BF16)
    wglu = jnp.einsum("oghk,gf->oghfk", glu_w.reshape(n_oct, per_oct, h, h), eye)
    wglu = wglu.reshape(n_oct, per_oct * h, per_oct * h).astype(BF16)
    return {
        "wb": wb,
        "are": abar_re.reshape(1, g * p), "aim": abar_im.reshape(1, g * p),
        "wcre": out_blockdiag(c_re).astype(BF16),
        "wcim": out_blockdiag(-c_im).astype(BF16),
        "d": d_skip.reshape(1, g * h), "wglu": wglu,
        "bglu": glu_b.reshape(1, g * h), "g": out_g.reshape(1, g * h),
    }


def _mix_ffn_kernel(x_ref, xh_ref, oa_ref, oah_ref, os_ref, osh_ref, ga_ref, wa_ref, ws_ref,
                    g1_ref, sh_ref, sc_ref, g2_ref, g_ref, wup_ref, cw_ref, cb_ref, wdn_ref,
                    o_ref, perm_ref, h_ref, up_ref, a_ref, *, d_ff):
    tm, d = x_ref.shape[1], x_ref.shape[2]
    n_grp = tm // SUBLANES
    n_slab = d // LANES
    n_chunk = d_ff // FFN_CK
    g, sh, sc = g_ref[...], sh_ref[0], sc_ref[0]

    def natural_rows(s):
        return slice(s * n_grp, (s + 1) * n_grp)

    def interleaved_rows(s):
        return pl.ds(s, n_grp, stride=SUBLANES)

    oa = jnp.concatenate([oa_ref[0], oah_ref[0]], axis=0).astype(F32)
    inv = lax.rsqrt(jnp.mean(oa * oa, axis=-1, keepdims=True) + EPS)
    mixed = (_dot((oa * inv * ga_ref[...]).astype(BF16), wa_ref[...])
             + _dot(jnp.concatenate([os_ref[...], osh_ref[...]], axis=0), ws_ref[...]))
    o_ref[0] = x_ref[0] + g1_ref[0] * mixed[:tm]
    x1_halo = xh_ref[0] + g1_ref[0] * mixed[tm:]

    hn = _norm_mod(o_ref[0], g, sh, sc)
    for slab in range(n_slab):
        lanes = slice(slab * LANES, (slab + 1) * LANES)
        for s in range(SUBLANES):
            perm_ref[slab, interleaved_rows(s), :] = hn[natural_rows(s), lanes]
    for slab in range(n_slab):
        h_ref[:tm, slab * LANES:(slab + 1) * LANES] = perm_ref[slab].astype(BF16)
    keep = (pl.program_id(1) > 0).astype(F32)
    h_ref[tm:, :] = (_norm_mod(x1_halo, g, sh, sc) * keep).astype(BF16)
    sublane = lax.broadcasted_iota(jnp.int32, (SUBLANES, FFN_CK), 0)

    def chunk(start):
        if not isinstance(start, int):
            start = pl.multiple_of(start, FFN_CK)
        return pl.ds(start, FFN_CK)

    def cols(c, part):
        return chunk(part * d_ff + c * FFN_CK)

    def wrapped(group, before):
        return jnp.where(sublane == 0, before, pltpu.roll(group, 1, axis=0))

    def up_proj(c, slot):
        for part in range(2):
            r = _dot(h_ref[...], wup_ref[:, cols(c, part)])
            pad = (CONV_WIDTH - 1) * SUBLANES
            up_ref[slot, part, pad:, :] = r[:tm]
            for back in range(1, CONV_WIDTH):
                up_ref[slot, part, pad - back * SUBLANES:pad - (back - 1) * SUBLANES, :] = wrapped(
                    r[tm - back * SUBLANES:tm - (back - 1) * SUBLANES],
                    r[tm + FFN_HALO - back:tm + FFN_HALO - back + 1])

    def conv(c, slot, part):
        out = cb_ref[:, cols(c, part)]
        for i in range(CONV_WIDTH):
            off = i * SUBLANES
            out = out + up_ref[slot, part, off:off + tm, :] * cw_ref[i:i + 1, cols(c, part)]
        return out

    def activate(c, slot):
        a_ref[:, chunk(c * FFN_CK)] = (
            jax.nn.gelu(conv(c, slot, 1)) * conv(c, slot, 0)).astype(BF16)

    for c in range(n_chunk):
        up_proj(c, c % 2)
        activate(c, c % 2)
    y = _dot(a_ref[...], wdn_ref[...])
    for slab in range(n_slab):
        perm_ref[slab] = y[:, slab * LANES:(slab + 1) * LANES]
    for slab in range(n_slab):
        lanes = slice(slab * LANES, (slab + 1) * LANES)
        for s in range(SUBLANES):
            o_ref[0, natural_rows(s), lanes] += (
                g2_ref[0, :, lanes] * perm_ref[slab, interleaved_rows(s), :])


def _mix_ffn(x, o_attn, o_ssm2, ga, w_out, li, g1, sh, sc, g2, g, wup, cw, cb, wdn):
    b, l, d = x.shape
    aw = o_attn.shape[-1]
    sw = o_ssm2.shape[-1] // b
    d_ff = wdn.shape[1]
    assert aw == sw
    w_half = lambda half: pl.BlockSpec((None, aw, d), lambda bi, i: (li, half, 0),
                                       pipeline_mode=pl.Buffered(1))
    tm = min(FFN_TM, l)
    halo_per_tile = tm // FFN_HALO
    halo = lambda i: jnp.maximum(i * halo_per_tile - 1, 0)
    bvec = pl.BlockSpec((1, 1, d), lambda bi, i: (bi, 0, 0))
    const = lambda a: pl.BlockSpec(a.shape, lambda bi, i: (0,) * a.ndim,
                                   pipeline_mode=pl.Buffered(1))
    return pl.pallas_call(
        functools.partial(_mix_ffn_kernel, d_ff=d_ff),
        grid=(b, l // tm),
        in_specs=[
            pl.BlockSpec((1, tm, d), lambda bi, i: (bi, i, 0)),
            pl.BlockSpec((1, FFN_HALO, d), lambda bi, i: (bi, halo(i), 0)),
            pl.BlockSpec((1, tm, aw), lambda bi, i: (bi, i, 0)),
            pl.BlockSpec((1, FFN_HALO, aw), lambda bi, i: (bi, halo(i), 0)),
            pl.BlockSpec((tm, sw), lambda bi, i: (i, bi)),
            pl.BlockSpec((FFN_HALO, sw), lambda bi, i: (halo(i), bi)),
            const(ga), w_half(0), w_half(1), bvec,
            bvec, bvec, bvec, const(g), _layer_block(wup, li), const(cw), const(cb),
            _layer_block(wdn, li),
        ],
        out_specs=pl.BlockSpec((1, tm, d), lambda bi, i: (bi, i, 0)),
        out_shape=jax.ShapeDtypeStruct(x.shape, F32),
        scratch_shapes=[pltpu.VMEM((d // LANES, tm, LANES), F32),
                        pltpu.VMEM((tm + FFN_HALO, d), BF16),
                        pltpu.VMEM((2, 2, tm + (CONV_WIDTH - 1) * SUBLANES, FFN_CK), F32),
                        pltpu.VMEM((tm, d_ff), BF16)],
        compiler_params=_cparams(("parallel", "parallel")),
        name="mix_ffn",
    )(x, x, o_attn, o_attn, o_ssm2, o_ssm2, ga, w_out, w_out, g1, sh, sc, g2, g, wup, cw, cb,
      wdn)


def kernel(x, c, ada_w, ada_b, norm1_g, w_in, q_norm_g, k_norm_g, ssm_a_re, ssm_a_im,
           ssm_log_dt, ssm_b_re, ssm_b_im, ssm_c_re, ssm_c_im, ssm_d, glu_w, glu_b,
           attn_out_g, ssm_out_g, w_out, norm2_g, ffn_w_up, ffn_conv_w, ffn_conv_b,
           ffn_w_down):
    b, l, d = x.shape
    depth = ada_w.shape[0]
    aw = attn_out_g.shape[-1]
    sw = ssm_out_g.shape[-1]
    n_heads = aw // HEAD_DIM
    assert b == SUBLANES and aw % LANES == 0 and sw % LANES == 0
    assert l % ATT_T == 0 and l % SSM_T == 0

    mod = _ada_mod(c, ada_w, ada_b)
    head = jnp.arange(aw) // HEAD_DIM
    ones_blk = (head[:MXU_TILE, None] == head[None, :MXU_TILE]).astype(BF16)
    inv_sqrt = LOG2_E / math.sqrt(HEAD_DIM)

    w_in_bf, w_out_bf = w_in.astype(BF16), w_out.astype(BF16)
    w_up_bf, w_down_bf = ffn_w_up.astype(BF16), ffn_w_down.astype(BF16)
    prm = jax.vmap(_ssm_params)(ssm_a_re, ssm_a_im, ssm_log_dt, ssm_b_re, ssm_b_im, ssm_c_re,
                                ssm_c_im, ssm_d, glu_w, glu_b, ssm_out_g)
    gq_all = jnp.tile(q_norm_g * inv_sqrt, (1, n_heads))
    gk_all = jnp.tile(k_norm_g, (1, n_heads))

    for li in range(depth):
        m = mod[li].reshape(b, N_MOD, 1, d)
        sh1, sc1, g1, sh2, sc2, g2 = (m[:, i] for i in range(N_MOD))
        q, k, v, u2 = _in_proj(x, sh1, sc1, norm1_g[li].reshape(1, d), w_in_bf, li,
                               gq_all[li].reshape(1, aw), gk_all[li].reshape(1, aw),
                               ones_blk, aw, sw)
        o_attn = _attn(q, k, v)
        o_ssm = _ssm(u2, prm, li, b)
        x = _mix_ffn(x, o_attn, o_ssm, attn_out_g[li].reshape(1, aw), w_out_bf, li, g1,
                     sh2, sc2, g2, norm2_g[li].reshape(1, d), w_up_bf,
                     ffn_conv_w[li], ffn_conv_b[li].reshape(1, -1), w_down_bf)
    return x
```

```python
import functools
import math

import jax
import jax.numpy as jnp
from jax import lax
from jax.experimental import pallas as pl
from jax.experimental.pallas import tpu as pltpu

F32 = jnp.float32
BF16 = jnp.bfloat16

EPS = 1e-6
HEAD_DIM = 64
SSM_GROUP = 16
SSM_STATE = 64
CONV_WIDTH = 3
N_MOD = 6

LANES = 128
SUBLANES = 8
MXU_TILE = 256
VMEM_LIMIT_BYTES = 56 * 1024 * 1024

ADA_TN = 1536
PROJ_TM = 512
ATT_T = 128
ATT_QS = 2048
ATT_WINDOW = 3
ATT_GROUP = 16
SSM_T = 128
FFN_TM = 512
FFN_CK = 256
FFN_HALO = 16

UNDERFLOW_BOUND = 152.0
LOG2_E = 1.4426950408889634


def _cparams(sem):
    return pltpu.CompilerParams(dimension_semantics=sem,
                                vmem_limit_bytes=VMEM_LIMIT_BYTES)


def _dot(a, b):
    return jnp.dot(a, b, preferred_element_type=F32)


def _ada_kernel(c_ref, w_ref, b_ref, o_ref):
    c = c_ref[...]
    c_act = c * jax.nn.sigmoid(c)
    c_hi = c_act.astype(BF16)
    c_lo = (c_act - c_hi.astype(F32)).astype(BF16)
    w = w_ref[0]
    w_hi = w.astype(BF16)
    w_lo = (w - w_hi.astype(F32)).astype(BF16)
    acc = _dot(c_hi, w_hi) + (_dot(c_hi, w_lo) + _dot(c_lo, w_hi))
    o_ref[0] = acc + b_ref[0]


def _ada_mod(c, ada_w, ada_b):
    depth, d, n = ada_w.shape
    b = c.shape[0]
    return pl.pallas_call(
        _ada_kernel,
        grid=(depth, n // ADA_TN),
        in_specs=[
            pl.BlockSpec((b, d), lambda l, j: (0, 0)),
            pl.BlockSpec((1, d, ADA_TN), lambda l, j: (l, 0, j)),
            pl.BlockSpec((1, 1, ADA_TN), lambda l, j: (l, 0, j)),
        ],
        out_specs=pl.BlockSpec((1, b, ADA_TN), lambda l, j: (l, 0, j)),
        out_shape=jax.ShapeDtypeStruct((depth, b, n), F32),
        compiler_params=_cparams(("parallel", "parallel")),
        name="ada_mod",
    )(c, ada_w, ada_b.reshape(depth, 1, n))


def _norm_mod(xf, g, sh, sc):
    inv = lax.rsqrt(jnp.mean(xf * xf, axis=-1, keepdims=True) + EPS)
    return xf * inv * (g * (1.0 + sc)) + sh


def _in_proj_kernel(x_ref, sh_ref, sc_ref, g_ref, w_ref, gq_ref, gk_ref, ones_ref,
                    q_ref, k_ref, v_ref, u_ref, *, aw):
    hb = _norm_mod(x_ref[0], g_ref[...], sh_ref[0], sc_ref[0]).astype(BF16)
    n_pair = aw // LANES

    def head_norm(p, g):
        sq = (p * p).astype(BF16)
        ms = jnp.concatenate([_dot(sq[:, c:c + MXU_TILE], ones_ref[...])
                              for c in range(0, aw, MXU_TILE)], axis=1) * (1.0 / HEAD_DIM)
        return p * lax.rsqrt(ms + EPS) * g

    q = head_norm(_dot(hb, w_ref[:, 0:aw]), gq_ref[...]).astype(BF16)
    for hp in range(n_pair):
        q_ref[0, hp] = q[:, hp * LANES:(hp + 1) * LANES]
    k = head_norm(_dot(hb, w_ref[:, aw:2 * aw]), gk_ref[...]).astype(BF16)
    for hp in range(n_pair):
        k_ref[0, hp] = k[:, hp * LANES:(hp + 1) * LANES]
    v = _dot(hb, w_ref[:, 2 * aw:3 * aw]).astype(BF16)
    for hp in range(n_pair):
        v_ref[0, hp] = v[:, hp * LANES:(hp + 1) * LANES]
    u_ref[...] = _dot(hb, w_ref[:, 3 * aw:]).astype(BF16)


def _layer_block(stacked, li, index_args=2):
    zeros = (0,) * (stacked.ndim - 1)
    if index_args == 1:
        return pl.BlockSpec((None,) + stacked.shape[1:], lambda i: (li,) + zeros)
    return pl.BlockSpec((None,) + stacked.shape[1:], lambda bi, i: (li,) + zeros,
                        pipeline_mode=pl.Buffered(1))


def _in_proj(x, sh, sc, g, w_in_bf, li, gq, gk, ones_blk, aw, sw):
    b, l, d = x.shape
    tm = min(PROJ_TM, l)
    n_pair = aw // LANES
    qkv_shape = jax.ShapeDtypeStruct((b, n_pair, l, LANES), BF16)
    qkv_spec = pl.BlockSpec((1, n_pair, tm, LANES), lambda bi, i: (bi, 0, i, 0))
    vec = lambda n: pl.BlockSpec((1, n), lambda bi, i: (0, 0))
    bvec = pl.BlockSpec((1, 1, d), lambda bi, i: (bi, 0, 0))
    return pl.pallas_call(
        functools.partial(_in_proj_kernel, aw=aw),
        grid=(b, l // tm),
        in_specs=[
            pl.BlockSpec((1, tm, d), lambda bi, i: (bi, i, 0)),
            bvec, bvec, vec(d),
            _layer_block(w_in_bf, li),
            vec(aw), vec(aw),
            pl.BlockSpec(ones_blk.shape, lambda bi, i: (0, 0)),
        ],
        out_specs=[qkv_spec, qkv_spec, qkv_spec,
                   pl.BlockSpec((tm, sw), lambda bi, i: (i, bi))],
        out_shape=[qkv_shape, qkv_shape, qkv_shape,
                   jax.ShapeDtypeStruct((l, b * sw), BF16)],
        compiler_params=_cparams(("parallel", "parallel")),
        name="in_proj",
    )(x, sh, sc, g, w_in_bf, gq, gk, ones_blk)


def _attn_kernel(q_ref, k_ref, v_ref, o_ref, acc_ref, carry_ref, next_ref, bound_ref):
    t = ATT_T
    n_sub = q_ref.shape[2] // t
    base = pl.program_id(2) * n_sub
    lane = lax.broadcasted_iota(jnp.int32, (1, LANES), 1)
    first = lane < HEAD_DIM
    row = lax.broadcasted_iota(jnp.int32, (t, t), 0)
    col = lax.broadcasted_iota(jnp.int32, (t, t), 1)
    later = jnp.where(row > col, 1.0, 0.0).astype(BF16)
    strict = col < row

    def split_heads(ref, j):
        blk = ref[0, 0, pl.ds(pl.multiple_of(j * t, t), t), :]
        zero = jnp.zeros_like(blk)
        return jnp.where(first, blk, zero), jnp.where(first, zero, blk)

    def blocks(items, carry_in=None):
        zz = [lax.dot_general(q2, jnp.concatenate(split_heads(k_ref, j), axis=0),
                              (((1,), (1,)), ((), ())),
                              preferred_element_type=F32) for q2, j, _ in items]
        logb, s_bf, sums = [], [], []
        for (_, _, depth), zz_i in zip(items, zz):
            for h in range(2):
                z = zz_i[:, h * t:(h + 1) * t]
                neg_abs = lax.bitcast_convert_type(
                    lax.bitcast_convert_type(z, jnp.uint32) | jnp.uint32(0x80000000), F32)
                s = jnp.maximum(z, 0.0) + jnp.log2(1.0 + jnp.exp2(neg_abs))
                sm = jnp.where(strict, s, 0.0) if depth == 0 else s
                logb.append(z - s)
                s_bf.append(sm.astype(BF16))
                sums.append(jnp.sum(sm, axis=-1, keepdims=True))
        within = [_dot(x, later) for x in s_bf]
        ws, carries = [], []
        for n, (_, _, depth) in enumerate(items):
            carry = carries[-1] if depth > 0 else carry_in
            w_heads, after = [], []
            for h in range(2):
                arg = logb[2 * n + h] - within[2 * n + h]
                if carry is not None:
                    arg = arg - carry[h]
                w = jnp.exp2(arg)
                if depth == 0:
                    w = jnp.where(strict, w, 0.0)
                w_heads.append(w.astype(BF16))
                after.append(sums[2 * n + h] if carry is None else carry[h] + sums[2 * n + h])
            ws.append(w_heads)
            carries.append(after)
        pvs = []
        for (w0, w1), (_, j, _) in zip(ws, items):
            v0, v1 = split_heads(v_ref, j)
            pvs.append(_dot(w0, v0) + _dot(w1, v1))
        return pvs, carries

    def first_blocks(depths):
        for g in range(0, n_sub, ATT_GROUP):
            first_blocks_of(depths, range(g, min(g + ATT_GROUP, n_sub)))

    def first_blocks_of(depths, group):
        items, owner = [], []
        for k in group:
            q2 = q_ref[0, 0, k * t:(k + 1) * t, :]
            for d in range(depths[k]):
                items.append((q2, base + k - d, d))
                owner.append(k)
        pvs, carries = blocks(items)
        for k in group:
            mine = [n for n, o in enumerate(owner) if o == k]
            acc = pvs[mine[0]]
            for n in mine[1:]:
                acc = acc + pvs[n]
            carry = carries[mine[-1]]
            acc_ref[k] = acc
            for h in range(2):
                carry_ref[k, h] = carry[h]
            next_ref[k] = base + k - depths[k]
            bound_ref[k] = jnp.min(jnp.minimum(carry[0], carry[1]))

    @pl.when(base >= ATT_WINDOW - 1)
    def _():
        first_blocks([ATT_WINDOW] * n_sub)

    @pl.when(base < ATT_WINDOW - 1)
    def _():
        first_blocks([ATT_WINDOW if k >= ATT_WINDOW - 1 else 1 for k in range(n_sub)])

    def finish(k, _):
        rows = pl.ds(pl.multiple_of(k * t, t), t)
        q2 = q_ref[0, 0, rows, :]

        def cond(state):
            j, bound = state
            return jnp.logical_and(j >= 0, bound < UNDERFLOW_BOUND)

        def body(state):
            j, _ = state
            carry = [carry_ref[k, 0], carry_ref[k, 1]]
            (pv,), (carry,) = blocks([(q2, j, -1)], carry)
            acc_ref[k] += pv
            for h in range(2):
                carry_ref[k, h] = carry[h]
            return j - 1, jnp.min(jnp.minimum(carry[0], carry[1]))

        lax.while_loop(cond, body, (next_ref[k], bound_ref[k]))
        o_ref[0, rows, :] = acc_ref[k].astype(o_ref.dtype)
        return 0

    lax.fori_loop(0, n_sub, finish, 0)


def _attn(q, k, v):
    b, n_pair, l, _ = q.shape
    t = ATT_T
    qs = min(ATT_QS, l)
    kv_spec = pl.BlockSpec((1, 1, l, LANES), lambda bi, hp, i: (bi, hp, 0, 0))
    return pl.pallas_call(
        _attn_kernel,
        grid=(b, n_pair, l // qs),
        in_specs=[pl.BlockSpec((1, 1, qs, LANES), lambda bi, hp, i: (bi, hp, i, 0)),
                  kv_spec, kv_spec],
        out_specs=pl.BlockSpec((1, qs, LANES), lambda bi, hp, i: (bi, i, hp)),
        out_shape=jax.ShapeDtypeStruct((b, l, n_pair * LANES), BF16),
        scratch_shapes=[pltpu.VMEM((qs // t, t, LANES), F32),
                        pltpu.VMEM((qs // t, 2, t, 1), F32),
                        pltpu.SMEM((qs // t,), jnp.int32),
                        pltpu.SMEM((qs // t,), F32)],
        compiler_params=_cparams(("parallel", "parallel", "parallel")),
        name="attn",
    )(q, k, v)


def _ssm_kernel(u_ref, wb_ref, are_ref, aim_ref, wcre_ref, wcim_ref, wft_ref, d_ref,
                wglu_ref, bglu_ref, g_ref, o_ref, sre_ref, sim_ref, st_ref, ulast_ref, y_ref,
                *, n_oct, batch):
    n_tok = u_ref.shape[0]
    n_pair = n_tok // 2
    rows = n_pair * batch
    sw = u_ref.shape[1] // batch
    oct_states = sre_ref.shape[2]

    @pl.when(pl.program_id(0) == 0)
    def _():
        st_ref[...] = jnp.zeros_like(st_ref)
        ulast_ref[...] = jnp.zeros_like(ulast_ref)

    def split_tokens(x):
        x3 = x.reshape(n_pair, 2 * batch, LANES)
        return x3[:, :batch].reshape(rows, LANES), x3[:, batch:].reshape(rows, LANES)

    def merge_tokens(even, odd):
        return jnp.concatenate([even.reshape(n_pair, batch, LANES),
                                odd.reshape(n_pair, batch, LANES)], axis=1).reshape(-1, LANES)

    for bi in range(batch):
        for o in range(n_oct):
            c0 = bi * sw + o * LANES
            y_ref[o, pl.ds(bi, n_tok, stride=batch), :] = u_ref[:, c0:c0 + LANES].astype(F32)
    def input_proj(o):
        even, odd = split_tokens(y_ref[o])
        prev_odd = jnp.concatenate([ulast_ref[o], odd[:rows - batch]], axis=0)
        ulast_ref[o] = odd[rows - batch:]
        bu = _dot(jnp.concatenate([prev_odd, even], axis=1).astype(BF16), wb_ref[o])
        sre_ref[o] = bu[:, :oct_states]
        sim_ref[o] = bu[:, oct_states:]

    def recurrence(o):
        cols = slice(o * oct_states, (o + 1) * oct_states)
        are = jnp.broadcast_to(are_ref[:, cols], (batch, oct_states))
        aim = jnp.broadcast_to(aim_ref[:, cols], (batch, oct_states))
        s_re, s_im = st_ref[0, :, cols], st_ref[1, :, cols]
        for t in range(n_pair):
            r = slice(t * batch, (t + 1) * batch)
            s_re, s_im = (are * s_re - aim * s_im + sre_ref[o, r, :],
                          are * s_im + aim * s_re + sim_ref[o, r, :])
            sre_ref[o, r, :] = s_re
            sim_ref[o, r, :] = s_im
        st_ref[0, :, cols] = s_re
        st_ref[1, :, cols] = s_im

    def output_proj(o):
        lanes = slice(o * LANES, (o + 1) * LANES)
        even, odd = split_tokens(y_ref[o])
        d = d_ref[:, lanes]
        y2 = (_dot(sre_ref[o].astype(BF16), wcre_ref[o])
              + _dot(sim_ref[o].astype(BF16), wcim_ref[o]))
        y = jnp.concatenate(
            [y2[:, :LANES] + d * even,
             y2[:, LANES:] + _dot(odd.astype(BF16), wft_ref[o]) + d * odd], axis=1)
        y = jax.nn.gelu(y)
        bias = jnp.concatenate([bglu_ref[:, lanes]] * 2, axis=1)
        gate = jax.nn.sigmoid(_dot(y.astype(BF16), wglu_ref[o]) + bias)
        out = y * gate
        y_ref[o] = merge_tokens(out[:, :LANES], out[:, LANES:])

    input_proj(0)
    for o in range(n_oct):
        if o + 1 < n_oct:
            input_proj(o + 1)
        recurrence(o)
        if o >= 1:
            output_proj(o - 1)
    output_proj(n_oct - 1)
    sumsq = 0.0
    for o in range(n_oct):
        out = y_ref[o]
        sumsq = sumsq + jnp.sum(out * out, axis=-1, keepdims=True)
    inv = lax.rsqrt(sumsq * (1.0 / sw) + EPS)
    for o in range(n_oct):
        y_ref[o] = y_ref[o] * inv * g_ref[:, o * LANES:(o + 1) * LANES]
    for bi in range(batch):
        for o in range(n_oct):
            c0 = bi * sw + o * LANES
            o_ref[:, c0:c0 + LANES] = (
                y_ref[o, pl.ds(bi, n_tok, stride=batch), :].astype(o_ref.dtype))


def _ssm(u2, prm, li, batch):
    l, width = u2.shape
    sw = width // batch
    n_oct = sw // LANES
    n_state = prm["are"].shape[-1]
    rows = SSM_T * batch
    pair_rows = rows // 2
    full = lambda a: _layer_block(a, li, index_args=1)
    names = ["wb", "are", "aim", "wcre", "wcim", "wft", "d", "wglu", "bglu", "g"]
    return pl.pallas_call(
        functools.partial(_ssm_kernel, n_oct=n_oct, batch=batch),
        grid=(l // SSM_T,),
        in_specs=[pl.BlockSpec((SSM_T, width), lambda i: (i, 0))]
                 + [full(prm[n]) for n in names],
        out_specs=pl.BlockSpec((SSM_T, width), lambda i: (i, 0)),
        out_shape=jax.ShapeDtypeStruct((l, width), BF16),
        scratch_shapes=[pltpu.VMEM((n_oct, pair_rows, n_state // n_oct), F32),
                        pltpu.VMEM((n_oct, pair_rows, n_state // n_oct), F32),
                        pltpu.VMEM((2, batch, n_state), F32),
                        pltpu.VMEM((n_oct, batch, LANES), F32),
                        pltpu.VMEM((n_oct, rows, LANES), F32)],
        compiler_params=_cparams(("arbitrary",)),
        name="ssm",
    )(u2, *[prm[n] for n in names])


def _ssm_params(a_re, a_im, log_dt, b_re, b_im, c_re, c_im, d_skip, glu_w, glu_b, out_g):
    g, p, h = b_re.shape
    per_oct = LANES // h
    n_oct = g // per_oct
    dt = jnp.exp(log_dt)[:, None]
    mag = jnp.exp(dt * a_re)
    abar_re = mag * jnp.cos(dt * a_im)
    abar_im = mag * jnp.sin(dt * a_im)
    em_re = abar_re - 1.0
    em_im = abar_im
    den = a_re * a_re + a_im * a_im
    f_re = (em_re * a_re + em_im * a_im) / den
    f_im = (em_im * a_re - em_re * a_im) / den
    bb_re = f_re[..., None] * b_re - f_im[..., None] * b_im
    bb_im = f_re[..., None] * b_im + f_im[..., None] * b_re
    eye = jnp.eye(per_oct, dtype=F32)

    def in_blockdiag(bb):
        bb = bb.reshape(n_oct, per_oct, p, h)
        m = jnp.einsum("ogph,gk->oghkp", bb, eye)
        return m.reshape(n_oct, per_oct * h, per_oct * p)

    def out_blockdiag(c):
        c = c.reshape(n_oct, per_oct, h, p)
        m = jnp.einsum("oghp,gk->ogpkh", c, eye)
        return m.reshape(n_oct, per_oct * p, per_oct * h)

    def chan_blockdiag(m):
        m = jnp.einsum("oghk,gf->oghfk", m.reshape(n_oct, per_oct, h, h), eye)
        return m.reshape(n_oct, per_oct * h, per_oct * h)

    are_c, aim_c = abar_re[..., None], abar_im[..., None]
    ab_re, ab_im = are_c * bb_re - aim_c * bb_im, are_c * bb_im + aim_c * bb_re
    are_r, aim_r = abar_re[:, None, :], abar_im[:, None, :]
    ca_re, ca_im = c_re * are_r - c_im * aim_r, c_re * aim_r + c_im * are_r
    feed = (jnp.einsum("gop,gpi->gio", c_re, bb_re) - jnp.einsum("gop,gpi->gio", c_im, bb_im))
    wb = jnp.concatenate(
        [jnp.concatenate([in_blockdiag(ab_re), in_blockdiag(ab_im)], axis=-1),
         jnp.concatenate([in_blockdiag(bb_re), in_blockdiag(bb_im)], axis=-1)], axis=1)
    glu1 = chan_blockdiag(glu_w)
    zero = jnp.zeros_like(glu1)
    wglu = jnp.concatenate([jnp.concatenate([glu1, zero], axis=-1),
                            jnp.concatenate([zero, glu1], axis=-1)], axis=1)
    return {
        "wb": wb.astype(BF16),
        "are": (abar_re * abar_re - abar_im * abar_im).reshape(1, g * p),
        "aim": (2.0 * abar_re * abar_im).reshape(1, g * p),
        "wcre": jnp.concatenate([out_blockdiag(c_re), out_blockdiag(ca_re)], -1).astype(BF16),
        "wcim": jnp.concatenate([out_blockdiag(-c_im), out_blockdiag(-ca_im)], -1).astype(BF16),
        "wft": chan_blockdiag(feed).astype(BF16),
        "d": d_skip.reshape(1, g * h), "wglu": wglu.astype(BF16),
        "bglu": glu_b.reshape(1, g * h), "g": out_g.reshape(1, g * h),
    }


def _mix_ffn_kernel(x_ref, xh_ref, oa_ref, oah_ref, os_ref, osh_ref, ga_ref, wa_ref, ws_ref,
                    g1_ref, sh_ref, sc_ref, g2_ref, g_ref, wup_ref, cw_ref, cb_ref, wdn_ref,
                    o_ref, perm_ref, h_ref, up_ref, a_ref, *, d_ff):
    tm, d = x_ref.shape[1], x_ref.shape[2]
    n_grp = tm // SUBLANES
    n_slab = d // LANES
    n_chunk = d_ff // FFN_CK
    g, sh, sc = g_ref[...], sh_ref[0], sc_ref[0]

    def natural_rows(s):
        return slice(s * n_grp, (s + 1) * n_grp)

    def interleaved_rows(s):
        return pl.ds(s, n_grp, stride=SUBLANES)

    oa = jnp.concatenate([oa_ref[0], oah_ref[0]], axis=0).astype(F32)
    inv = lax.rsqrt(jnp.mean(oa * oa, axis=-1, keepdims=True) + EPS)
    mixed = (_dot((oa * inv * ga_ref[...]).astype(BF16), wa_ref[...])
             + _dot(jnp.concatenate([os_ref[...], osh_ref[...]], axis=0), ws_ref[...]))
    o_ref[0] = x_ref[0] + g1_ref[0] * mixed[:tm]
    x1_halo = xh_ref[0] + g1_ref[0] * mixed[tm:]

    hn = _norm_mod(o_ref[0], g, sh, sc)
    for slab in range(n_slab):
        lanes = slice(slab * LANES, (slab + 1) * LANES)
        for s in range(SUBLANES):
            perm_ref[slab, interleaved_rows(s), :] = hn[natural_rows(s), lanes]
    for slab in range(n_slab):
        h_ref[:tm, slab * LANES:(slab + 1) * LANES] = perm_ref[slab].astype(BF16)
    keep = (pl.program_id(1) > 0).astype(F32)
    h_ref[tm:, :] = (_norm_mod(x1_halo, g, sh, sc) * keep).astype(BF16)
    sublane = lax.broadcasted_iota(jnp.int32, (SUBLANES, FFN_CK), 0)

    def chunk(start):
        if not isinstance(start, int):
            start = pl.multiple_of(start, FFN_CK)
        return pl.ds(start, FFN_CK)

    def cols(c, part):
        return chunk(part * d_ff + c * FFN_CK)

    def wrapped(group, before):
        return jnp.where(sublane == 0, before, pltpu.roll(group, 1, axis=0))

    def up_proj(c, slot):
        for part in range(2):
            r = _dot(h_ref[...], wup_ref[:, cols(c, part)])
            pad = (CONV_WIDTH - 1) * SUBLANES
            up_ref[slot, part, pad:, :] = r[:tm]
            for back in range(1, CONV_WIDTH):
                up_ref[slot, part, pad - back * SUBLANES:pad - (back - 1) * SUBLANES, :] = wrapped(
                    r[tm - back * SUBLANES:tm - (back - 1) * SUBLANES],
                    r[tm + FFN_HALO - back:tm + FFN_HALO - back + 1])

    def conv(c, slot, part):
        out = cb_ref[:, cols(c, part)]
        for i in range(CONV_WIDTH):
            off = i * SUBLANES
            out = out + up_ref[slot, part, off:off + tm, :] * cw_ref[i:i + 1, cols(c, part)]
        return out

    def activate(c, slot):
        a_ref[:, chunk(c * FFN_CK)] = (
            jax.nn.gelu(conv(c, slot, 1)) * conv(c, slot, 0)).astype(BF16)

    for c in range(n_chunk):
        up_proj(c, c % 2)
        activate(c, c % 2)
    y = _dot(a_ref[...], wdn_ref[...])
    for slab in range(n_slab):
        perm_ref[slab] = y[:, slab * LANES:(slab + 1) * LANES]
    for slab in range(n_slab):
        lanes = slice(slab * LANES, (slab + 1) * LANES)
        for s in range(SUBLANES):
            o_ref[0, natural_rows(s), lanes] += (
                g2_ref[0, :, lanes] * perm_ref[slab, interleaved_rows(s), :])


def _mix_ffn(x, o_attn, o_ssm2, ga, w_out, li, g1, sh, sc, g2, g, wup, cw, cb, wdn):
    b, l, d = x.shape
    aw = o_attn.shape[-1]
    sw = o_ssm2.shape[-1] // b
    d_ff = wdn.shape[1]
    assert aw == sw
    w_half = lambda half: pl.BlockSpec((None, aw, d), lambda bi, i: (li, half, 0),
                                       pipeline_mode=pl.Buffered(1))
    tm = min(FFN_TM, l)
    halo_per_tile = tm // FFN_HALO
    halo = lambda i: jnp.maximum(i * halo_per_tile - 1, 0)
    bvec = pl.BlockSpec((1, 1, d), lambda bi, i: (bi, 0, 0))
    const = lambda a: pl.BlockSpec(a.shape, lambda bi, i: (0,) * a.ndim,
                                   pipeline_mode=pl.Buffered(1))
    return pl.pallas_call(
        functools.partial(_mix_ffn_kernel, d_ff=d_ff),
        grid=(b, l // tm),
        in_specs=[
            pl.BlockSpec((1, tm, d), lambda bi, i: (bi, i, 0)),
            pl.BlockSpec((1, FFN_HALO, d), lambda bi, i: (bi, halo(i), 0)),
            pl.BlockSpec((1, tm, aw), lambda bi, i: (bi, i, 0)),
            pl.BlockSpec((1, FFN_HALO, aw), lambda bi, i: (bi, halo(i), 0)),
            pl.BlockSpec((tm, sw), lambda bi, i: (i, bi)),
            pl.BlockSpec((FFN_HALO, sw), lambda bi, i: (halo(i), bi)),
            const(ga), w_half(0), w_half(1), bvec,
            bvec, bvec, bvec, const(g), _layer_block(wup, li), const(cw), const(cb),
            _layer_block(wdn, li),
        ],
        out_specs=pl.BlockSpec((1, tm, d), lambda bi, i: (bi, i, 0)),
        out_shape=jax.ShapeDtypeStruct(x.shape, F32),
        scratch_shapes=[pltpu.VMEM((d // LANES, tm, LANES), F32),
                        pltpu.VMEM((tm + FFN_HALO, d), BF16),
                        pltpu.VMEM((2, 2, tm + (CONV_WIDTH - 1) * SUBLANES, FFN_CK), F32),
                        pltpu.VMEM((tm, d_ff), BF16)],
        compiler_params=_cparams(("parallel", "parallel")),
        name="mix_ffn",
    )(x, x, o_attn, o_attn, o_ssm2, o_ssm2, ga, w_out, w_out, g1, sh, sc, g2, g, wup, cw, cb,
      wdn)


def kernel(x, c, ada_w, ada_b, norm1_g, w_in, q_norm_g, k_norm_g, ssm_a_re, ssm_a_im,
           ssm_log_dt, ssm_b_re, ssm_b_im, ssm_c_re, ssm_c_im, ssm_d, glu_w, glu_b,
           attn_out_g, ssm_out_g, w_out, norm2_g, ffn_w_up, ffn_conv_w, ffn_conv_b,
           ffn_w_down):
    b, l, d = x.shape
    depth = ada_w.shape[0]
    aw = attn_out_g.shape[-1]
    sw = ssm_out_g.shape[-1]
    n_heads = aw // HEAD_DIM
    assert b == SUBLANES and aw % LANES == 0 and sw % LANES == 0
    assert l % ATT_T == 0 and l % SSM_T == 0

    mod = _ada_mod(c, ada_w, ada_b)
    head = jnp.arange(aw) // HEAD_DIM
    ones_blk = (head[:MXU_TILE, None] == head[None, :MXU_TILE]).astype(BF16)
    inv_sqrt = LOG2_E / math.sqrt(HEAD_DIM)

    w_in_bf, w_out_bf = w_in.astype(BF16), w_out.astype(BF16)
    w_up_bf, w_down_bf = ffn_w_up.astype(BF16), ffn_w_down.astype(BF16)
    prm = jax.vmap(_ssm_params)(ssm_a_re, ssm_a_im, ssm_log_dt, ssm_b_re, ssm_b_im, ssm_c_re,
                                ssm_c_im, ssm_d, glu_w, glu_b, ssm_out_g)
    gq_all = jnp.tile(q_norm_g * inv_sqrt, (1, n_heads))
    gk_all = jnp.tile(k_norm_g, (1, n_heads))

    for li in range(depth):
        m = mod[li].reshape(b, N_MOD, 1, d)
        sh1, sc1, g1, sh2, sc2, g2 = (m[:, i] for i in range(N_MOD))
        q, k, v, u2 = _in_proj(x, sh1, sc1, norm1_g[li].reshape(1, d), w_in_bf, li,
                               gq_all[li].reshape(1, aw), gk_all[li].reshape(1, aw),
                               ones_blk, aw, sw)
        o_attn = _attn(q, k, v)
        o_ssm = _ssm(u2, prm, li, b)
        x = _mix_ffn(x, o_attn, o_ssm, attn_out_g[li].reshape(1, aw), w_out_bf, li, g1,
                     sh2, sc2, g2, norm2_g[li].reshape(1, d), w_up_bf,
                     ffn_conv_w[li], ffn_conv_b[li].reshape(1, -1), w_down_bf)
    return x
```

```python
import functools
import math

import jax
import jax.numpy as jnp
from jax import lax
from jax.experimental import pallas as pl
from jax.experimental.pallas import tpu as pltpu

F32 = jnp.float32
BF16 = jnp.bfloat16

EPS = 1e-6
HEAD_DIM = 64
SSM_GROUP = 16
SSM_STATE = 64
CONV_WIDTH = 3
N_MOD = 6

LANES = 128
SUBLANES = 8
MXU_TILE = 256
VMEM_LIMIT_BYTES = 56 * 1024 * 1024

ADA_TN = 1536
PROJ_TM = 512
ATT_T = 128
ATT_QS = 2048
ATT_WINDOW = 3
ATT_GROUP = 16
SSM_T = 128
FFN_TM = 512
FFN_CK = 256

UNDERFLOW_BOUND = 152.0
LOG2_E = 1.4426950408889634


def _cparams(sem):
    return pltpu.CompilerParams(dimension_semantics=sem,
                                vmem_limit_bytes=VMEM_LIMIT_BYTES)


def _dot(a, b):
    return jnp.dot(a, b, preferred_element_type=F32)


def _ada_kernel(c_ref, w_ref, b_ref, o_ref):
    c = c_ref[...]
    c_act = c * jax.nn.sigmoid(c)
    c_hi = c_act.astype(BF16)
    c_lo = (c_act - c_hi.astype(F32)).astype(BF16)
    w = w_ref[0]
    w_hi = w.astype(BF16)
    w_lo = (w - w_hi.astype(F32)).astype(BF16)
    acc = _dot(c_hi, w_hi) + (_dot(c_hi, w_lo) + _dot(c_lo, w_hi))
    o_ref[0] = acc + b_ref[0]


def _ada_mod(c, ada_w, ada_b):
    depth, d, n = ada_w.shape
    b = c.shape[0]
    return pl.pallas_call(
        _ada_kernel,
        grid=(depth, n // ADA_TN),
        in_specs=[
            pl.BlockSpec((b, d), lambda l, j: (0, 0)),
            pl.BlockSpec((1, d, ADA_TN), lambda l, j: (l, 0, j)),
            pl.BlockSpec((1, 1, ADA_TN), lambda l, j: (l, 0, j)),
        ],
        out_specs=pl.BlockSpec((1, b, ADA_TN), lambda l, j: (l, 0, j)),
        out_shape=jax.ShapeDtypeStruct((depth, b, n), F32),
        compiler_params=_cparams(("parallel", "parallel")),
        name="ada_mod",
    )(c, ada_w, ada_b.reshape(depth, 1, n))


def _norm_mod(xf, g, sh, sc):
    inv = lax.rsqrt(jnp.mean(xf * xf, axis=-1, keepdims=True) + EPS)
    return xf * inv * (g * (1.0 + sc)) + sh


def _in_proj_kernel(x_ref, sh_ref, sc_ref, g_ref, w_ref, gq_ref, gk_ref, ones_ref,
                    q_ref, k_ref, v_ref, u_ref, *, aw):
    hb = _norm_mod(x_ref[0], g_ref[...], sh_ref[0], sc_ref[0]).astype(BF16)
    n_pair = aw // LANES

    def head_norm(p, g):
        sq = (p * p).astype(BF16)
        ms = jnp.concatenate([_dot(sq[:, c:c + MXU_TILE], ones_ref[...])
                              for c in range(0, aw, MXU_TILE)], axis=1) * (1.0 / HEAD_DIM)
        return p * lax.rsqrt(ms + EPS) * g

    q = head_norm(_dot(hb, w_ref[:, 0:aw]), gq_ref[...]).astype(BF16)
    for hp in range(n_pair):
        q_ref[0, hp] = q[:, hp * LANES:(hp + 1) * LANES]
    k = head_norm(_dot(hb, w_ref[:, aw:2 * aw]), gk_ref[...]).astype(BF16)
    for hp in range(n_pair):
        k_ref[0, hp] = k[:, hp * LANES:(hp + 1) * LANES]
    v = _dot(hb, w_ref[:, 2 * aw:3 * aw]).astype(BF16)
    for hp in range(n_pair):
        v_ref[0, hp] = v[:, hp * LANES:(hp + 1) * LANES]
    u_ref[...] = _dot(hb, w_ref[:, 3 * aw:]).astype(BF16)


def _layer_block(stacked, li, index_args=2):
    zeros = (0,) * (stacked.ndim - 1)
    if index_args == 1:
        return pl.BlockSpec((None,) + stacked.shape[1:], lambda i: (li,) + zeros)
    return pl.BlockSpec((None,) + stacked.shape[1:], lambda bi, i: (li,) + zeros,
                        pipeline_mode=pl.Buffered(1))


def _in_proj(x, sh, sc, g, w_in_bf, li, gq, gk, ones_blk, aw, sw):
    b, l, d = x.shape
    tm = min(PROJ_TM, l)
    n_pair = aw // LANES
    qkv_shape = jax.ShapeDtypeStruct((b, n_pair, l, LANES), BF16)
    qkv_spec = pl.BlockSpec((1, n_pair, tm, LANES), lambda bi, i: (bi, 0, i, 0))
    vec = lambda n: pl.BlockSpec((1, n), lambda bi, i: (0, 0))
    bvec = pl.BlockSpec((1, 1, d), lambda bi, i: (bi, 0, 0))
    return pl.pallas_call(
        functools.partial(_in_proj_kernel, aw=aw),
        grid=(b, l // tm),
        in_specs=[
            pl.BlockSpec((1, tm, d), lambda bi, i: (bi, i, 0)),
            bvec, bvec, vec(d),
            _layer_block(w_in_bf, li),
            vec(aw), vec(aw),
            pl.BlockSpec(ones_blk.shape, lambda bi, i: (0, 0)),
        ],
        out_specs=[qkv_spec, qkv_spec, qkv_spec,
                   pl.BlockSpec((tm, sw), lambda bi, i: (i, bi))],
        out_shape=[qkv_shape, qkv_shape, qkv_shape,
                   jax.ShapeDtypeStruct((l, b * sw), BF16)],
        compiler_params=_cparams(("parallel", "parallel")),
        name="in_proj",
    )(x, sh, sc, g, w_in_bf, gq, gk, ones_blk)


def _attn_kernel(q_ref, k_ref, v_ref, o_ref, acc_ref, carry_ref, next_ref, bound_ref):
    t = ATT_T
    n_sub = q_ref.shape[2] // t
    base = pl.program_id(2) * n_sub
    lane = lax.broadcasted_iota(jnp.int32, (1, LANES), 1)
    first = lane < HEAD_DIM
    row = lax.broadcasted_iota(jnp.int32, (t, t), 0)
    col = lax.broadcasted_iota(jnp.int32, (t, t), 1)
    later = jnp.where(row > col, 1.0, 0.0).astype(BF16)
    strict = col < row

    def split_heads(ref, j):
        blk = ref[0, 0, pl.ds(pl.multiple_of(j * t, t), t), :]
        zero = jnp.zeros_like(blk)
        return jnp.where(first, blk, zero), jnp.where(first, zero, blk)

    def blocks(items, carry_in=None):
        zz = [lax.dot_general(q2, jnp.concatenate(split_heads(k_ref, j), axis=0),
                              (((1,), (1,)), ((), ())),
                              preferred_element_type=F32) for q2, j, _ in items]
        logb, s_bf, sums = [], [], []
        for (_, _, depth), zz_i in zip(items, zz):
            for h in range(2):
                z = zz_i[:, h * t:(h + 1) * t]
                neg_abs = lax.bitcast_convert_type(
                    lax.bitcast_convert_type(z, jnp.uint32) | jnp.uint32(0x80000000), F32)
                s = jnp.maximum(z, 0.0) + jnp.log2(1.0 + jnp.exp2(neg_abs))
                sm = jnp.where(strict, s, 0.0) if depth == 0 else s
                logb.append(z - s)
                s_bf.append(sm.astype(BF16))
                sums.append(jnp.sum(sm, axis=-1, keepdims=True))
        within = [_dot(x, later) for x in s_bf]
        ws, carries = [], []
        for n, (_, _, depth) in enumerate(items):
            carry = carries[-1] if depth > 0 else carry_in
            w_heads, after = [], []
            for h in range(2):
                arg = logb[2 * n + h] - within[2 * n + h]
                if carry is not None:
                    arg = arg - carry[h]
                w = jnp.exp2(arg)
                if depth == 0:
                    w = jnp.where(strict, w, 0.0)
                w_heads.append(w.astype(BF16))
                after.append(sums[2 * n + h] if carry is None else carry[h] + sums[2 * n + h])
            ws.append(w_heads)
            carries.append(after)
        pvs = []
        for (w0, w1), (_, j, _) in zip(ws, items):
            v0, v1 = split_heads(v_ref, j)
            pvs.append(_dot(w0, v0) + _dot(w1, v1))
        return pvs, carries

    def first_blocks(depths):
        for g in range(0, n_sub, ATT_GROUP):
            first_blocks_of(depths, range(g, min(g + ATT_GROUP, n_sub)))

    def first_blocks_of(depths, group):
        items, owner = [], []
        for k in group:
            q2 = q_ref[0, 0, k * t:(k + 1) * t, :]
            for d in range(depths[k]):
                items.append((q2, base + k - d, d))
                owner.append(k)
        pvs, carries = blocks(items)
        for k in group:
            mine = [n for n, o in enumerate(owner) if o == k]
            acc = pvs[mine[0]]
            for n in mine[1:]:
                acc = acc + pvs[n]
            carry = carries[mine[-1]]
            acc_ref[k] = acc
            for h in range(2):
                carry_ref[k, h] = carry[h]
            next_ref[k] = base + k - depths[k]
            bound_ref[k] = jnp.min(jnp.minimum(carry[0], carry[1]))

    @pl.when(base >= ATT_WINDOW - 1)
    def _():
        first_blocks([ATT_WINDOW] * n_sub)

    @pl.when(base < ATT_WINDOW - 1)
    def _():
        first_blocks([ATT_WINDOW if k >= ATT_WINDOW - 1 else 1 for k in range(n_sub)])

    def finish(k, _):
        rows = pl.ds(pl.multiple_of(k * t, t), t)
        q2 = q_ref[0, 0, rows, :]

        def cond(state):
            j, bound = state
            return jnp.logical_and(j >= 0, bound < UNDERFLOW_BOUND)

        def body(state):
            j, _ = state
            carry = [carry_ref[k, 0], carry_ref[k, 1]]
            (pv,), (carry,) = blocks([(q2, j, -1)], carry)
            acc_ref[k] += pv
            for h in range(2):
                carry_ref[k, h] = carry[h]
            return j - 1, jnp.min(jnp.minimum(carry[0], carry[1]))

        lax.while_loop(cond, body, (next_ref[k], bound_ref[k]))
        o_ref[0, rows, :] = acc_ref[k].astype(o_ref.dtype)
        return 0

    lax.fori_loop(0, n_sub, finish, 0)


def _attn(q, k, v):
    b, n_pair, l, _ = q.shape
    t = ATT_T
    qs = min(ATT_QS, l)
    kv_spec = pl.BlockSpec((1, 1, l, LANES), lambda bi, hp, i: (bi, hp, 0, 0))
    return pl.pallas_call(
        _attn_kernel,
        grid=(b, n_pair, l // qs),
        in_specs=[pl.BlockSpec((1, 1, qs, LANES), lambda bi, hp, i: (bi, hp, i, 0)),
                  kv_spec, kv_spec],
        out_specs=pl.BlockSpec((1, qs, LANES), lambda bi, hp, i: (bi, i, hp)),
        out_shape=jax.ShapeDtypeStruct((b, l, n_pair * LANES), BF16),
        scratch_shapes=[pltpu.VMEM((qs // t, t, LANES), F32),
                        pltpu.VMEM((qs // t, 2, t, 1), F32),
                        pltpu.SMEM((qs // t,), jnp.int32),
                        pltpu.SMEM((qs // t,), F32)],
        compiler_params=_cparams(("parallel", "parallel", "parallel")),
        name="attn",
    )(q, k, v)


def _ssm_kernel(u_ref, wb_ref, are_ref, aim_ref, wcre_ref, wcim_ref, wft_ref, d_ref,
                wglu_ref, bglu_ref, g_ref, o_ref, sre_ref, sim_ref, st_ref, ulast_ref, y_ref,
                *, n_oct, batch):
    n_tok = u_ref.shape[0]
    n_pair = n_tok // 2
    rows = n_pair * batch
    sw = u_ref.shape[1] // batch
    oct_states = sre_ref.shape[2]

    @pl.when(pl.program_id(0) == 0)
    def _():
        st_ref[...] = jnp.zeros_like(st_ref)
        ulast_ref[...] = jnp.zeros_like(ulast_ref)

    def split_tokens(x):
        x3 = x.reshape(n_pair, 2 * batch, LANES)
        return x3[:, :batch].reshape(rows, LANES), x3[:, batch:].reshape(rows, LANES)

    def merge_tokens(even, odd):
        return jnp.concatenate([even.reshape(n_pair, batch, LANES),
                                odd.reshape(n_pair, batch, LANES)], axis=1).reshape(-1, LANES)

    for bi in range(batch):
        for o in range(n_oct):
            c0 = bi * sw + o * LANES
            y_ref[o, pl.ds(bi, n_tok, stride=batch), :] = u_ref[:, c0:c0 + LANES].astype(F32)
    def input_proj(o):
        even, odd = split_tokens(y_ref[o])
        prev_odd = jnp.concatenate([ulast_ref[o], odd[:rows - batch]], axis=0)
        ulast_ref[o] = odd[rows - batch:]
        bu = _dot(jnp.concatenate([prev_odd, even], axis=1).astype(BF16), wb_ref[o])
        sre_ref[o] = bu[:, :oct_states]
        sim_ref[o] = bu[:, oct_states:]

    def recurrence(o):
        cols = slice(o * oct_states, (o + 1) * oct_states)
        are = jnp.broadcast_to(are_ref[:, cols], (batch, oct_states))
        aim = jnp.broadcast_to(aim_ref[:, cols], (batch, oct_states))
        s_re, s_im = st_ref[0, :, cols], st_ref[1, :, cols]
        for t in range(n_pair):
            r = slice(t * batch, (t + 1) * batch)
            s_re, s_im = (are * s_re - aim * s_im + sre_ref[o, r, :],
                          are * s_im + aim * s_re + sim_ref[o, r, :])
            sre_ref[o, r, :] = s_re
            sim_ref[o, r, :] = s_im
        st_ref[0, :, cols] = s_re
        st_ref[1, :, cols] = s_im

    def output_proj(o):
        lanes = slice(o * LANES, (o + 1) * LANES)
        even, odd = split_tokens(y_ref[o])
        d = d_ref[:, lanes]
        y2 = (_dot(sre_ref[o].astype(BF16), wcre_ref[o])
              + _dot(sim_ref[o].astype(BF16), wcim_ref[o]))
        y = jnp.concatenate(
            [y2[:, :LANES] + d * even,
             y2[:, LANES:] + _dot(odd.astype(BF16), wft_ref[o]) + d * odd], axis=1)
        y = jax.nn.gelu(y)
        bias = jnp.concatenate([bglu_ref[:, lanes]] * 2, axis=1)
        gate = jax.nn.sigmoid(_dot(y.astype(BF16), wglu_ref[o]) + bias)
        out = y * gate
        y_ref[o] = merge_tokens(out[:, :LANES], out[:, LANES:])

    input_proj(0)
    for o in range(n_oct):
        if o + 1 < n_oct:
            input_proj(o + 1)
        recurrence(o)
        if o >= 1:
            output_proj(o - 1)
    output_proj(n_oct - 1)
    sumsq = 0.0
    for o in range(n_oct):
        out = y_ref[o]
        sumsq = sumsq + jnp.sum(out * out, axis=-1, keepdims=True)
    inv = lax.rsqrt(sumsq * (1.0 / sw) + EPS)
    for o in range(n_oct):
        y_ref[o] = y_ref[o] * inv * g_ref[:, o * LANES:(o + 1) * LANES]
    for bi in range(batch):
        for o in range(n_oct):
            c0 = bi * sw + o * LANES
            o_ref[:, c0:c0 + LANES] = (
                y_ref[o, pl.ds(bi, n_tok, stride=batch), :].astype(o_ref.dtype))


def _ssm(u2, prm, li, batch):
    l, width = u2.shape
    sw = width // batch
    n_oct = sw // LANES
    n_state = prm["are"].shape[-1]
    rows = SSM_T * batch
    pair_rows = rows // 2
    full = lambda a: _layer_block(a, li, index_args=1)
    names = ["wb", "are", "aim", "wcre", "wcim", "wft", "d", "wglu", "bglu", "g"]
    return pl.pallas_call(
        functools.partial(_ssm_kernel, n_oct=n_oct, batch=batch),
        grid=(l // SSM_T,),
        in_specs=[pl.BlockSpec((SSM_T, width), lambda i: (i, 0))]
                 + [full(prm[n]) for n in names],
        out_specs=pl.BlockSpec((SSM_T, width), lambda i: (i, 0)),
        out_shape=jax.ShapeDtypeStruct((l, width), BF16),
        scratch_shapes=[pltpu.VMEM((n_oct, pair_rows, n_state // n_oct), F32),
                        pltpu.VMEM((n_oct, pair_rows, n_state // n_oct), F32),
                        pltpu.VMEM((2, batch, n_state), F32),
                        pltpu.VMEM((n_oct, batch, LANES), F32),
                        pltpu.VMEM((n_oct, rows, LANES), F32)],
        compiler_params=_cparams(("arbitrary",)),
        name="ssm",
    )(u2, *[prm[n] for n in names])


def _ssm_params(a_re, a_im, log_dt, b_re, b_im, c_re, c_im, d_skip, glu_w, glu_b, out_g):
    g, p, h = b_re.shape
    per_oct = LANES // h
    n_oct = g // per_oct
    dt = jnp.exp(log_dt)[:, None]
    mag = jnp.exp(dt * a_re)
    abar_re = mag * jnp.cos(dt * a_im)
    abar_im = mag * jnp.sin(dt * a_im)
    em_re = abar_re - 1.0
    em_im = abar_im
    den = a_re * a_re + a_im * a_im
    f_re = (em_re * a_re + em_im * a_im) / den
    f_im = (em_im * a_re - em_re * a_im) / den
    bb_re = f_re[..., None] * b_re - f_im[..., None] * b_im
    bb_im = f_re[..., None] * b_im + f_im[..., None] * b_re
    eye = jnp.eye(per_oct, dtype=F32)

    def in_blockdiag(bb):
        bb = bb.reshape(n_oct, per_oct, p, h)
        m = jnp.einsum("ogph,gk->oghkp", bb, eye)
        return m.reshape(n_oct, per_oct * h, per_oct * p)

    def out_blockdiag(c):
        c = c.reshape(n_oct, per_oct, h, p)
        m = jnp.einsum("oghp,gk->ogpkh", c, eye)
        return m.reshape(n_oct, per_oct * p, per_oct * h)

    def chan_blockdiag(m):
        m = jnp.einsum("oghk,gf->oghfk", m.reshape(n_oct, per_oct, h, h), eye)
        return m.reshape(n_oct, per_oct * h, per_oct * h)

    are_c, aim_c = abar_re[..., None], abar_im[..., None]
    ab_re, ab_im = are_c * bb_re - aim_c * bb_im, are_c * bb_im + aim_c * bb_re
    are_r, aim_r = abar_re[:, None, :], abar_im[:, None, :]
    ca_re, ca_im = c_re * are_r - c_im * aim_r, c_re * aim_r + c_im * are_r
    feed = (jnp.einsum("gop,gpi->gio", c_re, bb_re) - jnp.einsum("gop,gpi->gio", c_im, bb_im))
    wb = jnp.concatenate(
        [jnp.concatenate([in_blockdiag(ab_re), in_blockdiag(ab_im)], axis=-1),
         jnp.concatenate([in_blockdiag(bb_re), in_blockdiag(bb_im)], axis=-1)], axis=1)
    glu1 = chan_blockdiag(glu_w)
    zero = jnp.zeros_like(glu1)
    wglu = jnp.concatenate([jnp.concatenate([glu1, zero], axis=-1),
                            jnp.concatenate([zero, glu1], axis=-1)], axis=1)
    return {
        "wb": wb.astype(BF16),
        "are": (abar_re * abar_re - abar_im * abar_im).reshape(1, g * p),
        "aim": (2.0 * abar_re * abar_im).reshape(1, g * p),
        "wcre": jnp.concatenate([out_blockdiag(c_re), out_blockdiag(ca_re)], -1).astype(BF16),
        "wcim": jnp.concatenate([out_blockdiag(-c_im), out_blockdiag(-ca_im)], -1).astype(BF16),
        "wft": chan_blockdiag(feed).astype(BF16),
        "d": d_skip.reshape(1, g * h), "wglu": wglu.astype(BF16),
        "bglu": glu_b.reshape(1, g * h), "g": out_g.reshape(1, g * h),
    }


def _mix_ffn_kernel(x_ref, oa_ref, os_ref, ga_ref, wa_ref, ws_ref, g1_ref, sh_ref, sc_ref,
                    g2_ref, g_ref, wup_ref, cw_ref, cb_ref, wdn_ref,
                    o_ref, perm_ref, h_ref, up_ref, hist_ref, a_ref, *, d_ff):
    tm, d = x_ref.shape[1], x_ref.shape[2]
    n_grp = tm // SUBLANES
    n_slab = d // LANES
    n_chunk = d_ff // FFN_CK
    g, sh, sc = g_ref[...], sh_ref[0], sc_ref[0]

    def natural_rows(s):
        return slice(s * n_grp, (s + 1) * n_grp)

    def interleaved_rows(s):
        return pl.ds(s, n_grp, stride=SUBLANES)

    oa = oa_ref[0].astype(F32)
    inv = lax.rsqrt(jnp.mean(oa * oa, axis=-1, keepdims=True) + EPS)
    mixed = (_dot((oa * inv * ga_ref[...]).astype(BF16), wa_ref[...])
             + _dot(os_ref[...], ws_ref[...]))
    o_ref[0] = x_ref[0] + g1_ref[0] * mixed

    hn = _norm_mod(o_ref[0], g, sh, sc)
    for slab in range(n_slab):
        lanes = slice(slab * LANES, (slab + 1) * LANES)
        for s in range(SUBLANES):
            perm_ref[slab, interleaved_rows(s), :] = hn[natural_rows(s), lanes]
    for slab in range(n_slab):
        h_ref[:, slab * LANES:(slab + 1) * LANES] = perm_ref[slab].astype(BF16)
    @pl.when(pl.program_id(1) == 0)
    def _():
        hist_ref[...] = jnp.zeros_like(hist_ref)

    sublane = lax.broadcasted_iota(jnp.int32, (SUBLANES, FFN_CK), 0)
    pad = (CONV_WIDTH - 1) * SUBLANES

    def chunk(start):
        if not isinstance(start, int):
            start = pl.multiple_of(start, FFN_CK)
        return pl.ds(start, FFN_CK)

    def cols(c, part):
        return chunk(part * d_ff + c * FFN_CK)

    def wrapped(group, before):
        return jnp.where(sublane == 0, before, pltpu.roll(group, 1, axis=0))

    def up_proj(c, slot):
        for part in range(2):
            r = _dot(h_ref[...], wup_ref[:, cols(c, part)])
            up_ref[slot, part, pad:, :] = r
            for back in range(1, CONV_WIDTH):
                row = pad - (back - 1) * SUBLANES - 1
                up_ref[slot, part, pad - back * SUBLANES:pad - (back - 1) * SUBLANES, :] = wrapped(
                    r[tm - back * SUBLANES:tm - (back - 1) * SUBLANES],
                    hist_ref[c, part, row:row + 1, :])
            hist_ref[c, part] = r[tm - pad:]

    def conv(c, slot, part):
        out = cb_ref[:, cols(c, part)]
        for i in range(CONV_WIDTH):
            off = i * SUBLANES
            out = out + up_ref[slot, part, off:off + tm, :] * cw_ref[i:i + 1, cols(c, part)]
        return out

    def activate(c, slot):
        a_ref[:, chunk(c * FFN_CK)] = (
            jax.nn.gelu(conv(c, slot, 1)) * conv(c, slot, 0)).astype(BF16)

    for c in range(n_chunk):
        up_proj(c, c % 2)
        activate(c, c % 2)
    y = _dot(a_ref[...], wdn_ref[...])
    for slab in range(n_slab):
        perm_ref[slab] = y[:, slab * LANES:(slab + 1) * LANES]
    for slab in range(n_slab):
        lanes = slice(slab * LANES, (slab + 1) * LANES)
        for s in range(SUBLANES):
            o_ref[0, natural_rows(s), lanes] += (
                g2_ref[0, :, lanes] * perm_ref[slab, interleaved_rows(s), :])


def _mix_ffn(x, o_attn, o_ssm2, ga, w_out, li, g1, sh, sc, g2, g, wup, cw, cb, wdn):
    b, l, d = x.shape
    aw = o_attn.shape[-1]
    sw = o_ssm2.shape[-1] // b
    d_ff = wdn.shape[1]
    assert aw == sw
    w_half = lambda half: pl.BlockSpec((None, aw, d), lambda bi, i: (li, half, 0),
                                       pipeline_mode=pl.Buffered(1))
    tm = min(FFN_TM, l)
    bvec = pl.BlockSpec((1, 1, d), lambda bi, i: (bi, 0, 0))
    const = lambda a: pl.BlockSpec(a.shape, lambda bi, i: (0,) * a.ndim,
                                   pipeline_mode=pl.Buffered(1))
    return pl.pallas_call(
        functools.partial(_mix_ffn_kernel, d_ff=d_ff),
        grid=(b, l // tm),
        in_specs=[
            pl.BlockSpec((1, tm, d), lambda bi, i: (bi, i, 0)),
            pl.BlockSpec((1, tm, aw), lambda bi, i: (bi, i, 0)),
            pl.BlockSpec((tm, sw), lambda bi, i: (i, bi)),
            const(ga), w_half(0), w_half(1), bvec,
            bvec, bvec, bvec, const(g), _layer_block(wup, li), const(cw), const(cb),
            _layer_block(wdn, li),
        ],
        out_specs=pl.BlockSpec((1, tm, d), lambda bi, i: (bi, i, 0)),
        out_shape=jax.ShapeDtypeStruct(x.shape, F32),
        scratch_shapes=[pltpu.VMEM((d // LANES, tm, LANES), F32),
                        pltpu.VMEM((tm, d), BF16),
                        pltpu.VMEM((2, 2, tm + (CONV_WIDTH - 1) * SUBLANES, FFN_CK), F32),
                        pltpu.VMEM((d_ff // FFN_CK, 2, (CONV_WIDTH - 1) * SUBLANES, FFN_CK), F32),
                        pltpu.VMEM((tm, d_ff), BF16)],
        compiler_params=_cparams(("arbitrary", "arbitrary")),
        name="mix_ffn",
    )(x, o_attn, o_ssm2, ga, w_out, w_out, g1, sh, sc, g2, g, wup, cw, cb, wdn)


def kernel(x, c, ada_w, ada_b, norm1_g, w_in, q_norm_g, k_norm_g, ssm_a_re, ssm_a_im,
           ssm_log_dt, ssm_b_re, ssm_b_im, ssm_c_re, ssm_c_im, ssm_d, glu_w, glu_b,
           attn_out_g, ssm_out_g, w_out, norm2_g, ffn_w_up, ffn_conv_w, ffn_conv_b,
           ffn_w_down):
    b, l, d = x.shape
    depth = ada_w.shape[0]
    aw = attn_out_g.shape[-1]
    sw = ssm_out_g.shape[-1]
    n_heads = aw // HEAD_DIM
    assert b == SUBLANES and aw % LANES == 0 and sw % LANES == 0
    assert l % ATT_T == 0 and l % SSM_T == 0

    mod = _ada_mod(c, ada_w, ada_b)
    head = jnp.arange(aw) // HEAD_DIM
    ones_blk = (head[:MXU_TILE, None] == head[None, :MXU_TILE]).astype(BF16)
    inv_sqrt = LOG2_E / math.sqrt(HEAD_DIM)

    w_in_bf, w_out_bf = w_in.astype(BF16), w_out.astype(BF16)
    w_up_bf, w_down_bf = ffn_w_up.astype(BF16), ffn_w_down.astype(BF16)
    prm = jax.vmap(_ssm_params)(ssm_a_re, ssm_a_im, ssm_log_dt, ssm_b_re, ssm_b_im, ssm_c_re,
                                ssm_c_im, ssm_d, glu_w, glu_b, ssm_out_g)
    gq_all = jnp.tile(q_norm_g * inv_sqrt, (1, n_heads))
    gk_all = jnp.tile(k_norm_g, (1, n_heads))

    for li in range(depth):
        m = mod[li].reshape(b, N_MOD, 1, d)
        sh1, sc1, g1, sh2, sc2, g2 = (m[:, i] for i in range(N_MOD))
        q, k, v, u2 = _in_proj(x, sh1, sc1, norm1_g[li].reshape(1, d), w_in_bf, li,
                               gq_all[li].reshape(1, aw), gk_all[li].reshape(1, aw),
                               ones_blk, aw, sw)
        o_attn = _attn(q, k, v)
        o_ssm = _ssm(u2, prm, li, b)
        x = _mix_ffn(x, o_attn, o_ssm, attn_out_g[li].reshape(1, aw), w_out_bf, li, g1,
                     sh2, sc2, g2, norm2_g[li].reshape(1, d), w_up_bf,
                     ffn_conv_w[li], ffn_conv_b[li].reshape(1, -1), w_down_bf)
    return x
```

```python
import functools
import math

import jax
import jax.numpy as jnp
from jax import lax
from jax.experimental import pallas as pl
from jax.experimental.pallas import tpu as pltpu

F32 = jnp.float32
BF16 = jnp.bfloat16

EPS = 1e-6
HEAD_DIM = 64
SSM_GROUP = 16
SSM_STATE = 64
CONV_WIDTH = 3
N_MOD = 6

LANES = 128
SUBLANES = 8
MXU_TILE = 256
VMEM_LIMIT_BYTES = 56 * 1024 * 1024

ADA_TN = 1536
PROJ_TM = 512
ATT_T = 128
ATT_QS = 2048
ATT_WINDOW = 3
ATT_GROUP = 16
SSM_T = 128
SSM_STAGE_GROUPS = 2
FFN_TM = 512
FFN_CK = 256
FFN_HALO = 16

UNDERFLOW_BOUND = 152.0
LOG2_E = 1.4426950408889634


def _cparams(sem):
    return pltpu.CompilerParams(dimension_semantics=sem,
                                vmem_limit_bytes=VMEM_LIMIT_BYTES)


def _dot(a, b):
    return jnp.dot(a, b, preferred_element_type=F32)


def _ada_kernel(c_ref, w_ref, b_ref, o_ref):
    c = c_ref[...]
    c_act = c * jax.nn.sigmoid(c)
    c_hi = c_act.astype(BF16)
    c_lo = (c_act - c_hi.astype(F32)).astype(BF16)
    w = w_ref[0]
    w_hi = w.astype(BF16)
    w_lo = (w - w_hi.astype(F32)).astype(BF16)
    acc = _dot(c_hi, w_hi) + (_dot(c_hi, w_lo) + _dot(c_lo, w_hi))
    o_ref[0] = acc + b_ref[0]


def _ada_mod(c, ada_w, ada_b):
    depth, d, n = ada_w.shape
    b = c.shape[0]
    return pl.pallas_call(
        _ada_kernel,
        grid=(depth, n // ADA_TN),
        in_specs=[
            pl.BlockSpec((b, d), lambda l, j: (0, 0)),
            pl.BlockSpec((1, d, ADA_TN), lambda l, j: (l, 0, j)),
            pl.BlockSpec((1, 1, ADA_TN), lambda l, j: (l, 0, j)),
        ],
        out_specs=pl.BlockSpec((1, b, ADA_TN), lambda l, j: (l, 0, j)),
        out_shape=jax.ShapeDtypeStruct((depth, b, n), F32),
        compiler_params=_cparams(("parallel", "parallel")),
        name="ada_mod",
    )(c, ada_w, ada_b.reshape(depth, 1, n))


def _norm_mod(xf, g, sh, sc):
    inv = lax.rsqrt(jnp.mean(xf * xf, axis=-1, keepdims=True) + EPS)
    return xf * inv * (g * (1.0 + sc)) + sh


def _in_proj_kernel(x_ref, sh_ref, sc_ref, g_ref, w_ref, gq_ref, gk_ref, ones_ref,
                    q_ref, k_ref, v_ref, u_ref, *, aw):
    hb = _norm_mod(x_ref[0], g_ref[...], sh_ref[0], sc_ref[0]).astype(BF16)
    n_pair = aw // LANES

    def head_norm(p, g):
        sq = (p * p).astype(BF16)
        ms = jnp.concatenate([_dot(sq[:, c:c + MXU_TILE], ones_ref[...])
                              for c in range(0, aw, MXU_TILE)], axis=1) * (1.0 / HEAD_DIM)
        return p * lax.rsqrt(ms + EPS) * g

    q = head_norm(_dot(hb, w_ref[:, 0:aw]), gq_ref[...]).astype(BF16)
    for hp in range(n_pair):
        q_ref[0, hp] = q[:, hp * LANES:(hp + 1) * LANES]
    k = head_norm(_dot(hb, w_ref[:, aw:2 * aw]), gk_ref[...]).astype(BF16)
    for hp in range(n_pair):
        k_ref[0, hp] = k[:, hp * LANES:(hp + 1) * LANES]
    v = _dot(hb, w_ref[:, 2 * aw:3 * aw]).astype(BF16)
    for hp in range(n_pair):
        v_ref[0, hp] = v[:, hp * LANES:(hp + 1) * LANES]
    u_ref[...] = _dot(hb, w_ref[:, 3 * aw:]).astype(BF16)


def _layer_block(stacked, li, index_args=2):
    zeros = (0,) * (stacked.ndim - 1)
    if index_args == 1:
        return pl.BlockSpec((None,) + stacked.shape[1:], lambda i: (li,) + zeros)
    return pl.BlockSpec((None,) + stacked.shape[1:], lambda bi, i: (li,) + zeros,
                        pipeline_mode=pl.Buffered(1))


def _in_proj(x, sh, sc, g, w_in_bf, li, gq, gk, ones_blk, aw, sw):
    b, l, d = x.shape
    tm = min(PROJ_TM, l)
    n_pair = aw // LANES
    qkv_shape = jax.ShapeDtypeStruct((b, n_pair, l, LANES), BF16)
    qkv_spec = pl.BlockSpec((1, n_pair, tm, LANES), lambda bi, i: (bi, 0, i, 0))
    vec = lambda n: pl.BlockSpec((1, n), lambda bi, i: (0, 0))
    bvec = pl.BlockSpec((1, 1, d), lambda bi, i: (bi, 0, 0))
    return pl.pallas_call(
        functools.partial(_in_proj_kernel, aw=aw),
        grid=(b, l // tm),
        in_specs=[
            pl.BlockSpec((1, tm, d), lambda bi, i: (bi, i, 0)),
            bvec, bvec, vec(d),
            _layer_block(w_in_bf, li),
            vec(aw), vec(aw),
            pl.BlockSpec(ones_blk.shape, lambda bi, i: (0, 0)),
        ],
        out_specs=[qkv_spec, qkv_spec, qkv_spec,
                   pl.BlockSpec((tm, sw), lambda bi, i: (i, bi))],
        out_shape=[qkv_shape, qkv_shape, qkv_shape,
                   jax.ShapeDtypeStruct((l, b * sw), BF16)],
        compiler_params=_cparams(("parallel", "parallel")),
        name="in_proj",
    )(x, sh, sc, g, w_in_bf, gq, gk, ones_blk)


def _attn_kernel(q_ref, k_ref, v_ref, o_ref, acc_ref, carry_ref, next_ref, bound_ref):
    t = ATT_T
    n_sub = q_ref.shape[2] // t
    base = pl.program_id(2) * n_sub
    lane = lax.broadcasted_iota(jnp.int32, (1, LANES), 1)
    first = lane < HEAD_DIM
    row = lax.broadcasted_iota(jnp.int32, (t, t), 0)
    col = lax.broadcasted_iota(jnp.int32, (t, t), 1)
    later = jnp.where(row > col, 1.0, 0.0).astype(BF16)
    strict = col < row

    def split_heads(ref, j):
        blk = ref[0, 0, pl.ds(pl.multiple_of(j * t, t), t), :]
        zero = jnp.zeros_like(blk)
        return jnp.where(first, blk, zero), jnp.where(first, zero, blk)

    def blocks(items, carry_in=None):
        zz = [lax.dot_general(q2, jnp.concatenate(split_heads(k_ref, j), axis=0),
                              (((1,), (1,)), ((), ())),
                              preferred_element_type=F32) for q2, j, _ in items]
        logb, s_bf, sums = [], [], []
        for (_, _, depth), zz_i in zip(items, zz):
            for h in range(2):
                z = zz_i[:, h * t:(h + 1) * t]
                neg_abs = lax.bitcast_convert_type(
                    lax.bitcast_convert_type(z, jnp.uint32) | jnp.uint32(0x80000000), F32)
                s = jnp.maximum(z, 0.0) + jnp.log2(1.0 + jnp.exp2(neg_abs))
                sm = jnp.where(strict, s, 0.0) if depth == 0 else s
                logb.append(z - s)
                s_bf.append(sm.astype(BF16))
                sums.append(jnp.sum(sm, axis=-1, keepdims=True))
        within = [_dot(x, later) for x in s_bf]
        ws, carries = [], []
        for n, (_, _, depth) in enumerate(items):
            carry = carries[-1] if depth > 0 else carry_in
            w_heads, after = [], []
            for h in range(2):
                arg = logb[2 * n + h] - within[2 * n + h]
                if carry is not None:
                    arg = arg - carry[h]
                w = jnp.exp2(arg)
                if depth == 0:
                    w = jnp.where(strict, w, 0.0)
                w_heads.append(w.astype(BF16))
                after.append(sums[2 * n + h] if carry is None else carry[h] + sums[2 * n + h])
            ws.append(w_heads)
            carries.append(after)
        pvs = []
        for (w0, w1), (_, j, _) in zip(ws, items):
            v0, v1 = split_heads(v_ref, j)
            pvs.append(_dot(w0, v0) + _dot(w1, v1))
        return pvs, carries

    def first_blocks(depths):
        for g in range(0, n_sub, ATT_GROUP):
            first_blocks_of(depths, range(g, min(g + ATT_GROUP, n_sub)))

    def first_blocks_of(depths, group):
        items, owner = [], []
        for k in group:
            q2 = q_ref[0, 0, k * t:(k + 1) * t, :]
            for d in range(depths[k]):
                items.append((q2, base + k - d, d))
                owner.append(k)
        pvs, carries = blocks(items)
        for k in group:
            mine = [n for n, o in enumerate(owner) if o == k]
            acc = pvs[mine[0]]
            for n in mine[1:]:
                acc = acc + pvs[n]
            carry = carries[mine[-1]]
            acc_ref[k] = acc
            for h in range(2):
                carry_ref[k, h] = carry[h]
            next_ref[k] = base + k - depths[k]
            bound_ref[k] = jnp.min(jnp.minimum(carry[0], carry[1]))

    @pl.when(base >= ATT_WINDOW - 1)
    def _():
        first_blocks([ATT_WINDOW] * n_sub)

    @pl.when(base < ATT_WINDOW - 1)
    def _():
        first_blocks([ATT_WINDOW if k >= ATT_WINDOW - 1 else 1 for k in range(n_sub)])

    def finish(k, _):
        rows = pl.ds(pl.multiple_of(k * t, t), t)
        q2 = q_ref[0, 0, rows, :]

        def cond(state):
            j, bound = state
            return jnp.logical_and(j >= 0, bound < UNDERFLOW_BOUND)

        def body(state):
            j, _ = state
            carry = [carry_ref[k, 0], carry_ref[k, 1]]
            (pv,), (carry,) = blocks([(q2, j, -1)], carry)
            acc_ref[k] += pv
            for h in range(2):
                carry_ref[k, h] = carry[h]
            return j - 1, jnp.min(jnp.minimum(carry[0], carry[1]))

        lax.while_loop(cond, body, (next_ref[k], bound_ref[k]))
        o_ref[0, rows, :] = acc_ref[k].astype(o_ref.dtype)
        return 0

    lax.fori_loop(0, n_sub, finish, 0)


def _attn(q, k, v):
    b, n_pair, l, _ = q.shape
    t = ATT_T
    qs = min(ATT_QS, l)
    kv_spec = pl.BlockSpec((1, 1, l, LANES), lambda bi, hp, i: (bi, hp, 0, 0))
    return pl.pallas_call(
        _attn_kernel,
        grid=(b, n_pair, l // qs),
        in_specs=[pl.BlockSpec((1, 1, qs, LANES), lambda bi, hp, i: (bi, hp, i, 0)),
                  kv_spec, kv_spec],
        out_specs=pl.BlockSpec((1, qs, LANES), lambda bi, hp, i: (bi, i, hp)),
        out_shape=jax.ShapeDtypeStruct((b, l, n_pair * LANES), BF16),
        scratch_shapes=[pltpu.VMEM((qs // t, t, LANES), F32),
                        pltpu.VMEM((qs // t, 2, t, 1), F32),
                        pltpu.SMEM((qs // t,), jnp.int32),
                        pltpu.SMEM((qs // t,), F32)],
        compiler_params=_cparams(("parallel", "parallel", "parallel")),
        name="attn",
    )(q, k, v)


def _ssm_kernel(u_ref, wb_ref, are_ref, aim_ref, wcre_ref, wcim_ref, wft_ref, d_ref,
                wglu_ref, bglu_ref, g_ref, o_ref, sre_ref, sim_ref, st_ref, ulast_ref, y_ref,
                *, n_oct, batch):
    n_tok = u_ref.shape[0]
    n_pair = n_tok // 2
    rows = n_pair * batch
    sw = u_ref.shape[1] // batch
    oct_states = sre_ref.shape[2]

    @pl.when(pl.program_id(0) == 0)
    def _():
        st_ref[...] = jnp.zeros_like(st_ref)
        ulast_ref[...] = jnp.zeros_like(ulast_ref)

    def split_tokens(x):
        x3 = x.reshape(n_pair, 2 * batch, LANES)
        return x3[:, :batch].reshape(rows, LANES), x3[:, batch:].reshape(rows, LANES)

    def merge_tokens(even, odd):
        return jnp.concatenate([even.reshape(n_pair, batch, LANES),
                                odd.reshape(n_pair, batch, LANES)], axis=1).reshape(-1, LANES)

    for bi in range(batch):
        for o in range(n_oct):
            c0 = bi * sw + o * LANES
            y_ref[o, pl.ds(bi, n_tok, stride=batch), :] = u_ref[:, c0:c0 + LANES].astype(F32)
    def input_proj(o):
        even, odd = split_tokens(y_ref[o])
        prev_odd = jnp.concatenate([ulast_ref[o], odd[:rows - batch]], axis=0)
        ulast_ref[o] = odd[rows - batch:]
        bu = _dot(jnp.concatenate([prev_odd, even], axis=1).astype(BF16), wb_ref[o])
        sre_ref[o] = bu[:, :oct_states]
        sim_ref[o] = bu[:, oct_states:]

    def recurrence(octs):
        cols = [slice(o * oct_states, (o + 1) * oct_states) for o in octs]
        are = [jnp.broadcast_to(are_ref[:, c], (batch, oct_states)) for c in cols]
        aim = [jnp.broadcast_to(aim_ref[:, c], (batch, oct_states)) for c in cols]
        s_re = [st_ref[0, :, c] for c in cols]
        s_im = [st_ref[1, :, c] for c in cols]
        for t in range(n_pair):
            r = slice(t * batch, (t + 1) * batch)
            for n, o in enumerate(octs):
                s_re[n], s_im[n] = (are[n] * s_re[n] - aim[n] * s_im[n] + sre_ref[o, r, :],
                                    are[n] * s_im[n] + aim[n] * s_re[n] + sim_ref[o, r, :])
                sre_ref[o, r, :] = s_re[n]
                sim_ref[o, r, :] = s_im[n]
        for n, c in enumerate(cols):
            st_ref[0, :, c] = s_re[n]
            st_ref[1, :, c] = s_im[n]

    def output_proj(octs):
        lanes = [slice(o * LANES, (o + 1) * LANES) for o in octs]
        tokens = [split_tokens(y_ref[o]) for o in octs]
        y2 = [_dot(sre_ref[o].astype(BF16), wcre_ref[o])
              + _dot(sim_ref[o].astype(BF16), wcim_ref[o]) for o in octs]
        feed = [_dot(odd.astype(BF16), wft_ref[o]) for o, (_, odd) in zip(octs, tokens)]
        ys = []
        for n, (even, odd) in enumerate(tokens):
            d = d_ref[:, lanes[n]]
            ys.append(jax.nn.gelu(jnp.concatenate(
                [y2[n][:, :LANES] + d * even, y2[n][:, LANES:] + feed[n] + d * odd], axis=1)))
        pre = [_dot(y.astype(BF16), wglu_ref[o]) for o, y in zip(octs, ys)]
        for n, o in enumerate(octs):
            bias = jnp.concatenate([bglu_ref[:, lanes[n]]] * 2, axis=1)
            out = ys[n] * jax.nn.sigmoid(pre[n] + bias)
            y_ref[o] = merge_tokens(out[:, :LANES], out[:, LANES:])

    pairs = [list(range(o, min(o + SSM_STAGE_GROUPS, n_oct)))
             for o in range(0, n_oct, SSM_STAGE_GROUPS)]
    for o in pairs[0]:
        input_proj(o)
    for n, octs in enumerate(pairs):
        if n + 1 < len(pairs):
            for o in pairs[n + 1]:
                input_proj(o)
        recurrence(octs)
        if n >= 1:
            output_proj(pairs[n - 1])
    output_proj(pairs[-1])
    sumsq = 0.0
    for o in range(n_oct):
        out = y_ref[o]
        sumsq = sumsq + jnp.sum(out * out, axis=-1, keepdims=True)
    inv = lax.rsqrt(sumsq * (1.0 / sw) + EPS)
    for o in range(n_oct):
        y_ref[o] = y_ref[o] * inv * g_ref[:, o * LANES:(o + 1) * LANES]
    for bi in range(batch):
        for o in range(n_oct):
            c0 = bi * sw + o * LANES
            o_ref[:, c0:c0 + LANES] = (
                y_ref[o, pl.ds(bi, n_tok, stride=batch), :].astype(o_ref.dtype))


def _ssm(u2, prm, li, batch):
    l, width = u2.shape
    sw = width // batch
    n_oct = sw // LANES
    n_state = prm["are"].shape[-1]
    rows = SSM_T * batch
    pair_rows = rows // 2
    full = lambda a: _layer_block(a, li, index_args=1)
    names = ["wb", "are", "aim", "wcre", "wcim", "wft", "d", "wglu", "bglu", "g"]
    return pl.pallas_call(
        functools.partial(_ssm_kernel, n_oct=n_oct, batch=batch),
        grid=(l // SSM_T,),
        in_specs=[pl.BlockSpec((SSM_T, width), lambda i: (i, 0))]
                 + [full(prm[n]) for n in names],
        out_specs=pl.BlockSpec((SSM_T, width), lambda i: (i, 0)),
        out_shape=jax.ShapeDtypeStruct((l, width), BF16),
        scratch_shapes=[pltpu.VMEM((n_oct, pair_rows, n_state // n_oct), F32),
                        pltpu.VMEM((n_oct, pair_rows, n_state // n_oct), F32),
                        pltpu.VMEM((2, batch, n_state), F32),
                        pltpu.VMEM((n_oct, batch, LANES), F32),
                        pltpu.VMEM((n_oct, rows, LANES), F32)],
        compiler_params=_cparams(("arbitrary",)),
        name="ssm",
    )(u2, *[prm[n] for n in names])


def _ssm_params(a_re, a_im, log_dt, b_re, b_im, c_re, c_im, d_skip, glu_w, glu_b, out_g):
    g, p, h = b_re.shape
    per_oct = LANES // h
    n_oct = g // per_oct
    dt = jnp.exp(log_dt)[:, None]
    mag = jnp.exp(dt * a_re)
    abar_re = mag * jnp.cos(dt * a_im)
    abar_im = mag * jnp.sin(dt * a_im)
    em_re = abar_re - 1.0
    em_im = abar_im
    den = a_re * a_re + a_im * a_im
    f_re = (em_re * a_re + em_im * a_im) / den
    f_im = (em_im * a_re - em_re * a_im) / den
    bb_re = f_re[..., None] * b_re - f_im[..., None] * b_im
    bb_im = f_re[..., None] * b_im + f_im[..., None] * b_re
    eye = jnp.eye(per_oct, dtype=F32)

    def in_blockdiag(bb):
        bb = bb.reshape(n_oct, per_oct, p, h)
        m = jnp.einsum("ogph,gk->oghkp", bb, eye)
        return m.reshape(n_oct, per_oct * h, per_oct * p)

    def out_blockdiag(c):
        c = c.reshape(n_oct, per_oct, h, p)
        m = jnp.einsum("oghp,gk->ogpkh", c, eye)
        return m.reshape(n_oct, per_oct * p, per_oct * h)

    def chan_blockdiag(m):
        m = jnp.einsum("oghk,gf->oghfk", m.reshape(n_oct, per_oct, h, h), eye)
        return m.reshape(n_oct, per_oct * h, per_oct * h)

    are_c, aim_c = abar_re[..., None], abar_im[..., None]
    ab_re, ab_im = are_c * bb_re - aim_c * bb_im, are_c * bb_im + aim_c * bb_re
    are_r, aim_r = abar_re[:, None, :], abar_im[:, None, :]
    ca_re, ca_im = c_re * are_r - c_im * aim_r, c_re * aim_r + c_im * are_r
    feed = (jnp.einsum("gop,gpi->gio", c_re, bb_re) - jnp.einsum("gop,gpi->gio", c_im, bb_im))
    wb = jnp.concatenate(
        [jnp.concatenate([in_blockdiag(ab_re), in_blockdiag(ab_im)], axis=-1),
         jnp.concatenate([in_blockdiag(bb_re), in_blockdiag(bb_im)], axis=-1)], axis=1)
    glu1 = chan_blockdiag(glu_w)
    zero = jnp.zeros_like(glu1)
    wglu = jnp.concatenate([jnp.concatenate([glu1, zero], axis=-1),
                            jnp.concatenate([zero, glu1], axis=-1)], axis=1)
    return {
        "wb": wb.astype(BF16),
        "are": (abar_re * abar_re - abar_im * abar_im).reshape(1, g * p),
        "aim": (2.0 * abar_re * abar_im).reshape(1, g * p),
        "wcre": jnp.concatenate([out_blockdiag(c_re), out_blockdiag(ca_re)], -1).astype(BF16),
        "wcim": jnp.concatenate([out_blockdiag(-c_im), out_blockdiag(-ca_im)], -1).astype(BF16),
        "wft": chan_blockdiag(feed).astype(BF16),
        "d": d_skip.reshape(1, g * h), "wglu": wglu.astype(BF16),
        "bglu": glu_b.reshape(1, g * h), "g": out_g.reshape(1, g * h),
    }


def _mix_ffn_kernel(x_ref, xh_ref, oa_ref, oah_ref, os_ref, osh_ref, ga_ref, wa_ref, ws_ref,
                    g1_ref, sh_ref, sc_ref, g2_ref, g_ref, wup_ref, cw_ref, cb_ref, wdn_ref,
                    o_ref, perm_ref, h_ref, up_ref, a_ref, *, d_ff):
    tm, d = x_ref.shape[1], x_ref.shape[2]
    n_grp = tm // SUBLANES
    n_slab = d // LANES
    n_chunk = d_ff // FFN_CK
    g, sh, sc = g_ref[...], sh_ref[0], sc_ref[0]

    def natural_rows(s):
        return slice(s * n_grp, (s + 1) * n_grp)

    def interleaved_rows(s):
        return pl.ds(s, n_grp, stride=SUBLANES)

    oa = jnp.concatenate([oa_ref[0], oah_ref[0]], axis=0).astype(F32)
    inv = lax.rsqrt(jnp.mean(oa * oa, axis=-1, keepdims=True) + EPS)
    mixed = (_dot((oa * inv * ga_ref[...]).astype(BF16), wa_ref[...])
             + _dot(jnp.concatenate([os_ref[...], osh_ref[...]], axis=0), ws_ref[...]))
    o_ref[0] = x_ref[0] + g1_ref[0] * mixed[:tm]
    x1_halo = xh_ref[0] + g1_ref[0] * mixed[tm:]

    hn = _norm_mod(o_ref[0], g, sh, sc)
    for slab in range(n_slab):
        lanes = slice(slab * LANES, (slab + 1) * LANES)
        for s in range(SUBLANES):
            perm_ref[slab, interleaved_rows(s), :] = hn[natural_rows(s), lanes]
    for slab in range(n_slab):
        h_ref[:tm, slab * LANES:(slab + 1) * LANES] = perm_ref[slab].astype(BF16)
    keep = (pl.program_id(1) > 0).astype(F32)
    h_ref[tm:, :] = (_norm_mod(x1_halo, g, sh, sc) * keep).astype(BF16)
    sublane = lax.broadcasted_iota(jnp.int32, (SUBLANES, FFN_CK), 0)

    def chunk(start):
        if not isinstance(start, int):
            start = pl.multiple_of(start, FFN_CK)
        return pl.ds(start, FFN_CK)

    def cols(c, part):
        return chunk(part * d_ff + c * FFN_CK)

    def wrapped(group, before):
        return jnp.where(sublane == 0, before, pltpu.roll(group, 1, axis=0))

    def up_proj(c, slot):
        for part in range(2):
            r = _dot(h_ref[...], wup_ref[:, cols(c, part)])
            pad = (CONV_WIDTH - 1) * SUBLANES
            up_ref[slot, part, pad:, :] = r[:tm]
            for back in range(1, CONV_WIDTH):
                up_ref[slot, part, pad - back * SUBLANES:pad - (back - 1) * SUBLANES, :] = wrapped(
                    r[tm - back * SUBLANES:tm - (back - 1) * SUBLANES],
                    r[tm + FFN_HALO - back:tm + FFN_HALO - back + 1])

    def conv(c, slot, part):
        out = cb_ref[:, cols(c, part)]
        for i in range(CONV_WIDTH):
            off = i * SUBLANES
            out = out + up_ref[slot, part, off:off + tm, :] * cw_ref[i:i + 1, cols(c, part)]
        return out

    def activate(c, slot):
        a_ref[:, chunk(c * FFN_CK)] = (
            jax.nn.gelu(conv(c, slot, 1)) * conv(c, slot, 0)).astype(BF16)

    for c in range(n_chunk):
        up_proj(c, c % 2)
        activate(c, c % 2)
    y = _dot(a_ref[...], wdn_ref[...])
    for slab in range(n_slab):
        perm_ref[slab] = y[:, slab * LANES:(slab + 1) * LANES]
    for slab in range(n_slab):
        lanes = slice(slab * LANES, (slab + 1) * LANES)
        for s in range(SUBLANES):
            o_ref[0, natural_rows(s), lanes] += (
                g2_ref[0, :, lanes] * perm_ref[slab, interleaved_rows(s), :])


def _mix_ffn(x, o_attn, o_ssm2, ga, w_out, li, g1, sh, sc, g2, g, wup, cw, cb, wdn):
    b, l, d = x.shape
    aw = o_attn.shape[-1]
    sw = o_ssm2.shape[-1] // b
    d_ff = wdn.shape[1]
    assert aw == sw
    w_half = lambda half: pl.BlockSpec((None, aw, d), lambda bi, i: (li, half, 0),
                                       pipeline_mode=pl.Buffered(1))
    tm = min(FFN_TM, l)
    halo_per_tile = tm // FFN_HALO
    halo = lambda i: jnp.maximum(i * halo_per_tile - 1, 0)
    bvec = pl.BlockSpec((1, 1, d), lambda bi, i: (bi, 0, 0))
    const = lambda a: pl.BlockSpec(a.shape, lambda bi, i: (0,) * a.ndim,
                                   pipeline_mode=pl.Buffered(1))
    return pl.pallas_call(
        functools.partial(_mix_ffn_kernel, d_ff=d_ff),
        grid=(b, l // tm),
        in_specs=[
            pl.BlockSpec((1, tm, d), lambda bi, i: (bi, i, 0)),
            pl.BlockSpec((1, FFN_HALO, d), lambda bi, i: (bi, halo(i), 0)),
            pl.BlockSpec((1, tm, aw), lambda bi, i: (bi, i, 0)),
            pl.BlockSpec((1, FFN_HALO, aw), lambda bi, i: (bi, halo(i), 0)),
            pl.BlockSpec((tm, sw), lambda bi, i: (i, bi)),
            pl.BlockSpec((FFN_HALO, sw), lambda bi, i: (halo(i), bi)),
            const(ga), w_half(0), w_half(1), bvec,
            bvec, bvec, bvec, const(g), _layer_block(wup, li), const(cw), const(cb),
            _layer_block(wdn, li),
        ],
        out_specs=pl.BlockSpec((1, tm, d), lambda bi, i: (bi, i, 0)),
        out_shape=jax.ShapeDtypeStruct(x.shape, F32),
        scratch_shapes=[pltpu.VMEM((d // LANES, tm, LANES), F32),
                        pltpu.VMEM((tm + FFN_HALO, d), BF16),
                        pltpu.VMEM((2, 2, tm + (CONV_WIDTH - 1) * SUBLANES, FFN_CK), F32),
                        pltpu.VMEM((tm, d_ff), BF16)],
        compiler_params=_cparams(("parallel", "parallel")),
        name="mix_ffn",
    )(x, x, o_attn, o_attn, o_ssm2, o_ssm2, ga, w_out, w_out, g1, sh, sc, g2, g, wup, cw, cb,
      wdn)


def kernel(x, c, ada_w, ada_b, norm1_g, w_in, q_norm_g, k_norm_g, ssm_a_re, ssm_a_im,
           ssm_log_dt, ssm_b_re, ssm_b_im, ssm_c_re, ssm_c_im, ssm_d, glu_w, glu_b,
           attn_out_g, ssm_out_g, w_out, norm2_g, ffn_w_up, ffn_conv_w, ffn_conv_b,
           ffn_w_down):
    b, l, d = x.shape
    depth = ada_w.shape[0]
    aw = attn_out_g.shape[-1]
    sw = ssm_out_g.shape[-1]
    n_heads = aw // HEAD_DIM
    assert b == SUBLANES and aw % LANES == 0 and sw % LANES == 0
    assert l % ATT_T == 0 and l % SSM_T == 0

    mod = _ada_mod(c, ada_w, ada_b)
    head = jnp.arange(aw) // HEAD_DIM
    ones_blk = (head[:MXU_TILE, None] == head[None, :MXU_TILE]).astype(BF16)
    inv_sqrt = LOG2_E / math.sqrt(HEAD_DIM)

    w_in_bf, w_out_bf = w_in.astype(BF16), w_out.astype(BF16)
    w_up_bf, w_down_bf = ffn_w_up.astype(BF16), ffn_w_down.astype(BF16)
    prm = jax.vmap(_ssm_params)(ssm_a_re, ssm_a_im, ssm_log_dt, ssm_b_re, ssm_b_im, ssm_c_re,
                                ssm_c_im, ssm_d, glu_w, glu_b, ssm_out_g)
    gq_all = jnp.tile(q_norm_g * inv_sqrt, (1, n_heads))
    gk_all = jnp.tile(k_norm_g, (1, n_heads))

    for li in range(depth):
        m = mod[li].reshape(b, N_MOD, 1, d)
        sh1, sc1, g1, sh2, sc2, g2 = (m[:, i] for i in range(N_MOD))
        q, k, v, u2 = _in_proj(x, sh1, sc1, norm1_g[li].reshape(1, d), w_in_bf, li,
                               gq_all[li].reshape(1, aw), gk_all[li].reshape(1, aw),
                               ones_blk, aw, sw)
        o_attn = _attn(q, k, v)
        o_ssm = _ssm(u2, prm, li, b)
        x = _mix_ffn(x, o_attn, o_ssm, attn_out_g[li].reshape(1, aw), w_out_bf, li, g1,
                     sh2, sc2, g2, norm2_g[li].reshape(1, d), w_up_bf,
                     ffn_conv_w[li], ffn_conv_b[li].reshape(1, -1), w_down_bf)
    return x
```

```python
import functools
import math

import jax
import jax.numpy as jnp
from jax import lax
from jax.experimental import pallas as pl
from jax.experimental.pallas import tpu as pltpu

F32 = jnp.float32
BF16 = jnp.bfloat16

EPS = 1e-6
HEAD_DIM = 64
SSM_GROUP = 16
SSM_STATE = 64
CONV_WIDTH = 3
N_MOD = 6

LANES = 128
SUBLANES = 8
MXU_TILE = 256
VMEM_LIMIT_BYTES = 56 * 1024 * 1024

ADA_TN = 1536
PROJ_TM = 512
ATT_T = 128
ATT_QS = 4096
ATT_WINDOW = 3
ATT_GROUP = 16
SSM_T = 128
SSM_STAGE_GROUPS = 2
FFN_TM = 512
FFN_CK = 256
FFN_HALO = 16

UNDERFLOW_BOUND = 152.0
LOG2_E = 1.4426950408889634


def _cparams(sem):
    return pltpu.CompilerParams(dimension_semantics=sem,
                                vmem_limit_bytes=VMEM_LIMIT_BYTES)


def _dot(a, b):
    return jnp.dot(a, b, preferred_element_type=F32)


def _ada_kernel(c_ref, w_ref, b_ref, o_ref):
    c = c_ref[...]
    c_act = c * jax.nn.sigmoid(c)
    c_hi = c_act.astype(BF16)
    c_lo = (c_act - c_hi.astype(F32)).astype(BF16)
    w = w_ref[0]
    w_hi = w.astype(BF16)
    w_lo = (w - w_hi.astype(F32)).astype(BF16)
    acc = _dot(c_hi, w_hi) + (_dot(c_hi, w_lo) + _dot(c_lo, w_hi))
    o_ref[0] = acc + b_ref[0]


def _ada_mod(c, ada_w, ada_b):
    depth, d, n = ada_w.shape
    b = c.shape[0]
    return pl.pallas_call(
        _ada_kernel,
        grid=(depth, n // ADA_TN),
        in_specs=[
            pl.BlockSpec((b, d), lambda l, j: (0, 0)),
            pl.BlockSpec((1, d, ADA_TN), lambda l, j: (l, 0, j)),
            pl.BlockSpec((1, 1, ADA_TN), lambda l, j: (l, 0, j)),
        ],
        out_specs=pl.BlockSpec((1, b, ADA_TN), lambda l, j: (l, 0, j)),
        out_shape=jax.ShapeDtypeStruct((depth, b, n), F32),
        compiler_params=_cparams(("parallel", "parallel")),
        name="ada_mod",
    )(c, ada_w, ada_b.reshape(depth, 1, n))


def _norm_mod(xf, g, sh, sc):
    inv = lax.rsqrt(jnp.mean(xf * xf, axis=-1, keepdims=True) + EPS)
    return xf * inv * (g * (1.0 + sc)) + sh


def _in_proj_kernel(x_ref, sh_ref, sc_ref, g_ref, w_ref, gq_ref, gk_ref, ones_ref,
                    q_ref, k_ref, v_ref, u_ref, *, aw):
    hb = _norm_mod(x_ref[0], g_ref[...], sh_ref[0], sc_ref[0]).astype(BF16)
    n_pair = aw // LANES

    def head_norm(p, g):
        sq = (p * p).astype(BF16)
        ms = jnp.concatenate([_dot(sq[:, c:c + MXU_TILE], ones_ref[...])
                              for c in range(0, aw, MXU_TILE)], axis=1) * (1.0 / HEAD_DIM)
        return p * lax.rsqrt(ms + EPS) * g

    q = head_norm(_dot(hb, w_ref[:, 0:aw]), gq_ref[...]).astype(BF16)
    for hp in range(n_pair):
        q_ref[0, hp] = q[:, hp * LANES:(hp + 1) * LANES]
    k = head_norm(_dot(hb, w_ref[:, aw:2 * aw]), gk_ref[...]).astype(BF16)
    for hp in range(n_pair):
        k_ref[0, hp] = k[:, hp * LANES:(hp + 1) * LANES]
    v = _dot(hb, w_ref[:, 2 * aw:3 * aw]).astype(BF16)
    for hp in range(n_pair):
        v_ref[0, hp] = v[:, hp * LANES:(hp + 1) * LANES]
    u_ref[...] = _dot(hb, w_ref[:, 3 * aw:]).astype(BF16)


def _layer_block(stacked, li, index_args=2):
    zeros = (0,) * (stacked.ndim - 1)
    if index_args == 1:
        return pl.BlockSpec((None,) + stacked.shape[1:], lambda i: (li,) + zeros)
    return pl.BlockSpec((None,) + stacked.shape[1:], lambda bi, i: (li,) + zeros,
                        pipeline_mode=pl.Buffered(1))


def _in_proj(x, sh, sc, g, w_in_bf, li, gq, gk, ones_blk, aw, sw):
    b, l, d = x.shape
    tm = min(PROJ_TM, l)
    n_pair = aw // LANES
    qkv_shape = jax.ShapeDtypeStruct((b, n_pair, l, LANES), BF16)
    qkv_spec = pl.BlockSpec((1, n_pair, tm, LANES), lambda bi, i: (bi, 0, i, 0))
    vec = lambda n: pl.BlockSpec((1, n), lambda bi, i: (0, 0))
    bvec = pl.BlockSpec((1, 1, d), lambda bi, i: (bi, 0, 0))
    return pl.pallas_call(
        functools.partial(_in_proj_kernel, aw=aw),
        grid=(b, l // tm),
        in_specs=[
            pl.BlockSpec((1, tm, d), lambda bi, i: (bi, i, 0)),
            bvec, bvec, vec(d),
            _layer_block(w_in_bf, li),
            vec(aw), vec(aw),
            pl.BlockSpec(ones_blk.shape, lambda bi, i: (0, 0)),
        ],
        out_specs=[qkv_spec, qkv_spec, qkv_spec,
                   pl.BlockSpec((tm, sw), lambda bi, i: (i, bi))],
        out_shape=[qkv_shape, qkv_shape, qkv_shape,
                   jax.ShapeDtypeStruct((l, b * sw), BF16)],
        compiler_params=_cparams(("parallel", "parallel")),
        name="in_proj",
    )(x, sh, sc, g, w_in_bf, gq, gk, ones_blk)


def _attn_kernel(q_ref, k_ref, v_ref, o_ref, acc_ref, carry_ref, next_ref, bound_ref):
    t = ATT_T
    n_sub = q_ref.shape[2] // t
    base = pl.program_id(2) * n_sub
    lane = lax.broadcasted_iota(jnp.int32, (1, LANES), 1)
    first = lane < HEAD_DIM
    row = lax.broadcasted_iota(jnp.int32, (t, t), 0)
    col = lax.broadcasted_iota(jnp.int32, (t, t), 1)
    later = jnp.where(row > col, 1.0, 0.0).astype(BF16)
    strict = col < row

    def split_heads(ref, j):
        blk = ref[0, 0, pl.ds(pl.multiple_of(j * t, t), t), :]
        zero = jnp.zeros_like(blk)
        return jnp.where(first, blk, zero), jnp.where(first, zero, blk)

    def blocks(items, carry_in=None):
        zz = [lax.dot_general(q2, jnp.concatenate(split_heads(k_ref, j), axis=0),
                              (((1,), (1,)), ((), ())),
                              preferred_element_type=F32) for q2, j, _ in items]
        logb, s_bf, sums = [], [], []
        for (_, _, depth), zz_i in zip(items, zz):
            for h in range(2):
                z = zz_i[:, h * t:(h + 1) * t]
                neg_abs = lax.bitcast_convert_type(
                    lax.bitcast_convert_type(z, jnp.uint32) | jnp.uint32(0x80000000), F32)
                s = jnp.maximum(z, 0.0) + jnp.log2(1.0 + jnp.exp2(neg_abs))
                sm = jnp.where(strict, s, 0.0) if depth == 0 else s
                logb.append(z - s)
                s_bf.append(sm.astype(BF16))
                sums.append(jnp.sum(sm, axis=-1, keepdims=True))
        within = [_dot(x, later) for x in s_bf]
        ws, carries = [], []
        for n, (_, _, depth) in enumerate(items):
            carry = carries[-1] if depth > 0 else carry_in
            w_heads, after = [], []
            for h in range(2):
                arg = logb[2 * n + h] - within[2 * n + h]
                if carry is not None:
                    arg = arg - carry[h]
                w = jnp.exp2(arg)
                if depth == 0:
                    w = jnp.where(strict, w, 0.0)
                w_heads.append(w.astype(BF16))
                after.append(sums[2 * n + h] if carry is None else carry[h] + sums[2 * n + h])
            ws.append(w_heads)
            carries.append(after)
        pvs = []
        for (w0, w1), (_, j, _) in zip(ws, items):
            v0, v1 = split_heads(v_ref, j)
            pvs.append(_dot(w0, v0) + _dot(w1, v1))
        return pvs, carries

    def first_blocks(depths):
        for g in range(0, n_sub, ATT_GROUP):
            first_blocks_of(depths, range(g, min(g + ATT_GROUP, n_sub)))

    def first_blocks_of(depths, group):
        items, owner = [], []
        for k in group:
            q2 = q_ref[0, 0, k * t:(k + 1) * t, :]
            for d in range(depths[k]):
                items.append((q2, base + k - d, d))
                owner.append(k)
        pvs, carries = blocks(items)
        for k in group:
            mine = [n for n, o in enumerate(owner) if o == k]
            acc = pvs[mine[0]]
            for n in mine[1:]:
                acc = acc + pvs[n]
            carry = carries[mine[-1]]
            acc_ref[k] = acc
            for h in range(2):
                carry_ref[k, h] = carry[h]
            next_ref[k] = base + k - depths[k]
            bound_ref[k] = jnp.min(jnp.minimum(carry[0], carry[1]))

    @pl.when(base >= ATT_WINDOW - 1)
    def _():
        first_blocks([ATT_WINDOW] * n_sub)

    @pl.when(base < ATT_WINDOW - 1)
    def _():
        first_blocks([ATT_WINDOW if k >= ATT_WINDOW - 1 else 1 for k in range(n_sub)])

    def finish(k, _):
        rows = pl.ds(pl.multiple_of(k * t, t), t)
        q2 = q_ref[0, 0, rows, :]

        def cond(state):
            j, bound = state
            return jnp.logical_and(j >= 0, bound < UNDERFLOW_BOUND)

        def body(state):
            j, _ = state
            carry = [carry_ref[k, 0], carry_ref[k, 1]]
            (pv,), (carry,) = blocks([(q2, j, -1)], carry)
            acc_ref[k] += pv
            for h in range(2):
                carry_ref[k, h] = carry[h]
            return j - 1, jnp.min(jnp.minimum(carry[0], carry[1]))

        lax.while_loop(cond, body, (next_ref[k], bound_ref[k]))
        o_ref[0, rows, :] = acc_ref[k].astype(o_ref.dtype)
        return 0

    lax.fori_loop(0, n_sub, finish, 0)


def _attn(q, k, v):
    b, n_pair, l, _ = q.shape
    t = ATT_T
    qs = min(ATT_QS, l)
    kv_spec = pl.BlockSpec((1, 1, l, LANES), lambda bi, hp, i: (bi, hp, 0, 0))
    return pl.pallas_call(
        _attn_kernel,
        grid=(b, n_pair, l // qs),
        in_specs=[pl.BlockSpec((1, 1, qs, LANES), lambda bi, hp, i: (bi, hp, i, 0)),
                  kv_spec, kv_spec],
        out_specs=pl.BlockSpec((1, qs, LANES), lambda bi, hp, i: (bi, i, hp)),
        out_shape=jax.ShapeDtypeStruct((b, l, n_pair * LANES), BF16),
        scratch_shapes=[pltpu.VMEM((qs // t, t, LANES), F32),
                        pltpu.VMEM((qs // t, 2, t, 1), F32),
                        pltpu.SMEM((qs // t,), jnp.int32),
                        pltpu.SMEM((qs // t,), F32)],
        compiler_params=_cparams(("parallel", "parallel", "parallel")),
        name="attn",
    )(q, k, v)


def _ssm_kernel(u_ref, wb_ref, are_ref, aim_ref, wcre_ref, wcim_ref, wft_ref, d_ref,
                wglu_ref, bglu_ref, g_ref, o_ref, sre_ref, sim_ref, st_ref, ulast_ref, y_ref,
                *, n_oct, batch):
    n_tok = u_ref.shape[0]
    n_pair = n_tok // 2
    rows = n_pair * batch
    sw = u_ref.shape[1] // batch
    oct_states = sre_ref.shape[2]

    @pl.when(pl.program_id(0) == 0)
    def _():
        st_ref[...] = jnp.zeros_like(st_ref)
        ulast_ref[...] = jnp.zeros_like(ulast_ref)

    def split_tokens(x):
        x3 = x.reshape(n_pair, 2 * batch, LANES)
        return x3[:, :batch].reshape(rows, LANES), x3[:, batch:].reshape(rows, LANES)

    def merge_tokens(even, odd):
        return jnp.concatenate([even.reshape(n_pair, batch, LANES),
                                odd.reshape(n_pair, batch, LANES)], axis=1).reshape(-1, LANES)

    for bi in range(batch):
        for o in range(n_oct):
            c0 = bi * sw + o * LANES
            y_ref[o, pl.ds(bi, n_tok, stride=batch), :] = u_ref[:, c0:c0 + LANES].astype(F32)
    def input_proj(o):
        even, odd = split_tokens(y_ref[o])
        prev_odd = jnp.concatenate([ulast_ref[o], odd[:rows - batch]], axis=0)
        ulast_ref[o] = odd[rows - batch:]
        bu = _dot(jnp.concatenate([prev_odd, even], axis=1).astype(BF16), wb_ref[o])
        sre_ref[o] = bu[:, :oct_states]
        sim_ref[o] = bu[:, oct_states:]

    def recurrence(octs):
        cols = [slice(o * oct_states, (o + 1) * oct_states) for o in octs]
        are = [jnp.broadcast_to(are_ref[:, c], (batch, oct_states)) for c in cols]
        aim = [jnp.broadcast_to(aim_ref[:, c], (batch, oct_states)) for c in cols]
        s_re = [st_ref[0, :, c] for c in cols]
        s_im = [st_ref[1, :, c] for c in cols]
        for t in range(n_pair):
            r = slice(t * batch, (t + 1) * batch)
            for n, o in enumerate(octs):
                s_re[n], s_im[n] = (are[n] * s_re[n] - aim[n] * s_im[n] + sre_ref[o, r, :],
                                    are[n] * s_im[n] + aim[n] * s_re[n] + sim_ref[o, r, :])
                sre_ref[o, r, :] = s_re[n]
                sim_ref[o, r, :] = s_im[n]
        for n, c in enumerate(cols):
            st_ref[0, :, c] = s_re[n]
            st_ref[1, :, c] = s_im[n]

    def output_proj(octs):
        lanes = [slice(o * LANES, (o + 1) * LANES) for o in octs]
        tokens = [split_tokens(y_ref[o]) for o in octs]
        y2 = [_dot(sre_ref[o].astype(BF16), wcre_ref[o])
              + _dot(sim_ref[o].astype(BF16), wcim_ref[o]) for o in octs]
        feed = [_dot(odd.astype(BF16), wft_ref[o]) for o, (_, odd) in zip(octs, tokens)]
        ys = []
        for n, (even, odd) in enumerate(tokens):
            d = d_ref[:, lanes[n]]
            ys.append(jax.nn.gelu(jnp.concatenate(
                [y2[n][:, :LANES] + d * even, y2[n][:, LANES:] + feed[n] + d * odd], axis=1)))
        pre = [_dot(y.astype(BF16), wglu_ref[o]) for o, y in zip(octs, ys)]
        for n, o in enumerate(octs):
            bias = jnp.concatenate([bglu_ref[:, lanes[n]]] * 2, axis=1)
            out = ys[n] * jax.nn.sigmoid(pre[n] + bias)
            y_ref[o] = merge_tokens(out[:, :LANES], out[:, LANES:])

    pairs = [list(range(o, min(o + SSM_STAGE_GROUPS, n_oct)))
             for o in range(0, n_oct, SSM_STAGE_GROUPS)]
    for o in pairs[0]:
        input_proj(o)
    for n, octs in enumerate(pairs):
        if n + 1 < len(pairs):
            for o in pairs[n + 1]:
                input_proj(o)
        recurrence(octs)
        if n >= 1:
            output_proj(pairs[n - 1])
    output_proj(pairs[-1])
    sumsq = 0.0
    for o in range(n_oct):
        out = y_ref[o]
        sumsq = sumsq + jnp.sum(out * out, axis=-1, keepdims=True)
    inv = lax.rsqrt(sumsq * (1.0 / sw) + EPS)
    for o in range(n_oct):
        y_ref[o] = y_ref[o] * inv * g_ref[:, o * LANES:(o + 1) * LANES]
    for bi in range(batch):
        for o in range(n_oct):
            c0 = bi * sw + o * LANES
            o_ref[:, c0:c0 + LANES] = (
                y_ref[o, pl.ds(bi, n_tok, stride=batch), :].astype(o_ref.dtype))


def _ssm(u2, prm, li, batch):
    l, width = u2.shape
    sw = width // batch
    n_oct = sw // LANES
    n_state = prm["are"].shape[-1]
    rows = SSM_T * batch
    pair_rows = rows // 2
    full = lambda a: _layer_block(a, li, index_args=1)
    names = ["wb", "are", "aim", "wcre", "wcim", "wft", "d", "wglu", "bglu", "g"]
    return pl.pallas_call(
        functools.partial(_ssm_kernel, n_oct=n_oct, batch=batch),
        grid=(l // SSM_T,),
        in_specs=[pl.BlockSpec((SSM_T, width), lambda i: (i, 0))]
                 + [full(prm[n]) for n in names],
        out_specs=pl.BlockSpec((SSM_T, width), lambda i: (i, 0)),
        out_shape=jax.ShapeDtypeStruct((l, width), BF16),
        scratch_shapes=[pltpu.VMEM((n_oct, pair_rows, n_state // n_oct), F32),
                        pltpu.VMEM((n_oct, pair_rows, n_state // n_oct), F32),
                        pltpu.VMEM((2, batch, n_state), F32),
                        pltpu.VMEM((n_oct, batch, LANES), F32),
                        pltpu.VMEM((n_oct, rows, LANES), F32)],
        compiler_params=_cparams(("arbitrary",)),
        name="ssm",
    )(u2, *[prm[n] for n in names])


def _ssm_params(a_re, a_im, log_dt, b_re, b_im, c_re, c_im, d_skip, glu_w, glu_b, out_g):
    g, p, h = b_re.shape
    per_oct = LANES // h
    n_oct = g // per_oct
    dt = jnp.exp(log_dt)[:, None]
    mag = jnp.exp(dt * a_re)
    abar_re = mag * jnp.cos(dt * a_im)
    abar_im = mag * jnp.sin(dt * a_im)
    em_re = abar_re - 1.0
    em_im = abar_im
    den = a_re * a_re + a_im * a_im
    f_re = (em_re * a_re + em_im * a_im) / den
    f_im = (em_im * a_re - em_re * a_im) / den
    bb_re = f_re[..., None] * b_re - f_im[..., None] * b_im
    bb_im = f_re[..., None] * b_im + f_im[..., None] * b_re
    eye = jnp.eye(per_oct, dtype=F32)

    def in_blockdiag(bb):
        bb = bb.reshape(n_oct, per_oct, p, h)
        m = jnp.einsum("ogph,gk->oghkp", bb, eye)
        return m.reshape(n_oct, per_oct * h, per_oct * p)

    def out_blockdiag(c):
        c = c.reshape(n_oct, per_oct, h, p)
        m = jnp.einsum("oghp,gk->ogpkh", c, eye)
        return m.reshape(n_oct, per_oct * p, per_oct * h)

    def chan_blockdiag(m):
        m = jnp.einsum("oghk,gf->oghfk", m.reshape(n_oct, per_oct, h, h), eye)
        return m.reshape(n_oct, per_oct * h, per_oct * h)

    are_c, aim_c = abar_re[..., None], abar_im[..., None]
    ab_re, ab_im = are_c * bb_re - aim_c * bb_im, are_c * bb_im + aim_c * bb_re
    are_r, aim_r = abar_re[:, None, :], abar_im[:, None, :]
    ca_re, ca_im = c_re * are_r - c_im * aim_r, c_re * aim_r + c_im * are_r
    feed = (jnp.einsum("gop,gpi->gio", c_re, bb_re) - jnp.einsum("gop,gpi->gio", c_im, bb_im))
    wb = jnp.concatenate(
        [jnp.concatenate([in_blockdiag(ab_re), in_blockdiag(ab_im)], axis=-1),
         jnp.concatenate([in_blockdiag(bb_re), in_blockdiag(bb_im)], axis=-1)], axis=1)
    glu1 = chan_blockdiag(glu_w)
    zero = jnp.zeros_like(glu1)
    wglu = jnp.concatenate([jnp.concatenate([glu1, zero], axis=-1),
                            jnp.concatenate([zero, glu1], axis=-1)], axis=1)
    return {
        "wb": wb.astype(BF16),
        "are": (abar_re * abar_re - abar_im * abar_im).reshape(1, g * p),
        "aim": (2.0 * abar_re * abar_im).reshape(1, g * p),
        "wcre": jnp.concatenate([out_blockdiag(c_re), out_blockdiag(ca_re)], -1).astype(BF16),
        "wcim": jnp.concatenate([out_blockdiag(-c_im), out_blockdiag(-ca_im)], -1).astype(BF16),
        "wft": chan_blockdiag(feed).astype(BF16),
        "d": d_skip.reshape(1, g * h), "wglu": wglu.astype(BF16),
        "bglu": glu_b.reshape(1, g * h), "g": out_g.reshape(1, g * h),
    }


def _mix_ffn_kernel(x_ref, xh_ref, oa_ref, oah_ref, os_ref, osh_ref, ga_ref, wa_ref, ws_ref,
                    g1_ref, sh_ref, sc_ref, g2_ref, g_ref, wup_ref, cw_ref, cb_ref, wdn_ref,
                    o_ref, perm_ref, h_ref, up_ref, a_ref, *, d_ff):
    tm, d = x_ref.shape[1], x_ref.shape[2]
    n_grp = tm // SUBLANES
    n_slab = d // LANES
    n_chunk = d_ff // FFN_CK
    g, sh, sc = g_ref[...], sh_ref[0], sc_ref[0]

    def natural_rows(s):
        return slice(s * n_grp, (s + 1) * n_grp)

    def interleaved_rows(s):
        return pl.ds(s, n_grp, stride=SUBLANES)

    oa = jnp.concatenate([oa_ref[0], oah_ref[0]], axis=0).astype(F32)
    inv = lax.rsqrt(jnp.mean(oa * oa, axis=-1, keepdims=True) + EPS)
    mixed = (_dot((oa * inv * ga_ref[...]).astype(BF16), wa_ref[...])
             + _dot(jnp.concatenate([os_ref[...], osh_ref[...]], axis=0), ws_ref[...]))
    o_ref[0] = x_ref[0] + g1_ref[0] * mixed[:tm]
    x1_halo = xh_ref[0] + g1_ref[0] * mixed[tm:]

    hn = _norm_mod(o_ref[0], g, sh, sc)
    for slab in range(n_slab):
        lanes = slice(slab * LANES, (slab + 1) * LANES)
        for s in range(SUBLANES):
            perm_ref[slab, interleaved_rows(s), :] = hn[natural_rows(s), lanes]
    for slab in range(n_slab):
        h_ref[:tm, slab * LANES:(slab + 1) * LANES] = perm_ref[slab].astype(BF16)
    keep = (pl.program_id(1) > 0).astype(F32)
    h_ref[tm:, :] = (_norm_mod(x1_halo, g, sh, sc) * keep).astype(BF16)
    sublane = lax.broadcasted_iota(jnp.int32, (SUBLANES, FFN_CK), 0)

    def chunk(start):
        if not isinstance(start, int):
            start = pl.multiple_of(start, FFN_CK)
        return pl.ds(start, FFN_CK)

    def cols(c, part):
        return chunk(part * d_ff + c * FFN_CK)

    def wrapped(group, before):
        return jnp.where(sublane == 0, before, pltpu.roll(group, 1, axis=0))

    def up_proj(c, slot):
        for part in range(2):
            r = _dot(h_ref[...], wup_ref[:, cols(c, part)])
            pad = (CONV_WIDTH - 1) * SUBLANES
            up_ref[slot, part, pad:, :] = r[:tm]
            for back in range(1, CONV_WIDTH):
                up_ref[slot, part, pad - back * SUBLANES:pad - (back - 1) * SUBLANES, :] = wrapped(
                    r[tm - back * SUBLANES:tm - (back - 1) * SUBLANES],
                    r[tm + FFN_HALO - back:tm + FFN_HALO - back + 1])

    def conv(c, slot, part):
        out = cb_ref[:, cols(c, part)]
        for i in range(CONV_WIDTH):
            off = i * SUBLANES
            out = out + up_ref[slot, part, off:off + tm, :] * cw_ref[i:i + 1, cols(c, part)]
        return out

    def activate(c, slot):
        a_ref[:, chunk(c * FFN_CK)] = (
            jax.nn.gelu(conv(c, slot, 1)) * conv(c, slot, 0)).astype(BF16)

    for c in range(n_chunk):
        up_proj(c, c % 2)
        activate(c, c % 2)
    y = _dot(a_ref[...], wdn_ref[...])
    for slab in range(n_slab):
        perm_ref[slab] = y[:, slab * LANES:(slab + 1) * LANES]
    for slab in range(n_slab):
        lanes = slice(slab * LANES, (slab + 1) * LANES)
        for s in range(SUBLANES):
            o_ref[0, natural_rows(s), lanes] += (
                g2_ref[0, :, lanes] * perm_ref[slab, interleaved_rows(s), :])


def _mix_ffn(x, o_attn, o_ssm2, ga, w_out, li, g1, sh, sc, g2, g, wup, cw, cb, wdn):
    b, l, d = x.shape
    aw = o_attn.shape[-1]
    sw = o_ssm2.shape[-1] // b
    d_ff = wdn.shape[1]
    assert aw == sw
    w_half = lambda half: pl.BlockSpec((None, aw, d), lambda bi, i: (li, half, 0),
                                       pipeline_mode=pl.Buffered(1))
    tm = min(FFN_TM, l)
    halo_per_tile = tm // FFN_HALO
    halo = lambda i: jnp.maximum(i * halo_per_tile - 1, 0)
    bvec = pl.BlockSpec((1, 1, d), lambda bi, i: (bi, 0, 0))
    const = lambda a: pl.BlockSpec(a.shape, lambda bi, i: (0,) * a.ndim,
                                   pipeline_mode=pl.Buffered(1))
    return pl.pallas_call(
        functools.partial(_mix_ffn_kernel, d_ff=d_ff),
        grid=(b, l // tm),
        in_specs=[
            pl.BlockSpec((1, tm, d), lambda bi, i: (bi, i, 0)),
            pl.BlockSpec((1, FFN_HALO, d), lambda bi, i: (bi, halo(i), 0)),
            pl.BlockSpec((1, tm, aw), lambda bi, i: (bi, i, 0)),
            pl.BlockSpec((1, FFN_HALO, aw), lambda bi, i: (bi, halo(i), 0)),
            pl.BlockSpec((tm, sw), lambda bi, i: (i, bi)),
            pl.BlockSpec((FFN_HALO, sw), lambda bi, i: (halo(i), bi)),
            const(ga), w_half(0), w_half(1), bvec,
            bvec, bvec, bvec, const(g), _layer_block(wup, li), const(cw), const(cb),
            _layer_block(wdn, li),
        ],
        out_specs=pl.BlockSpec((1, tm, d), lambda bi, i: (bi, i, 0)),
        out_shape=jax.ShapeDtypeStruct(x.shape, F32),
        scratch_shapes=[pltpu.VMEM((d // LANES, tm, LANES), F32),
                        pltpu.VMEM((tm + FFN_HALO, d), BF16),
                        pltpu.VMEM((2, 2, tm + (CONV_WIDTH - 1) * SUBLANES, FFN_CK), F32),
                        pltpu.VMEM((tm, d_ff), BF16)],
        compiler_params=_cparams(("parallel", "parallel")),
        name="mix_ffn",
    )(x, x, o_attn, o_attn, o_ssm2, o_ssm2, ga, w_out, w_out, g1, sh, sc, g2, g, wup, cw, cb,
      wdn)


def kernel(x, c, ada_w, ada_b, norm1_g, w_in, q_norm_g, k_norm_g, ssm_a_re, ssm_a_im,
           ssm_log_dt, ssm_b_re, ssm_b_im, ssm_c_re, ssm_c_im, ssm_d, glu_w, glu_b,
           attn_out_g, ssm_out_g, w_out, norm2_g, ffn_w_up, ffn_conv_w, ffn_conv_b,
           ffn_w_down):
    b, l, d = x.shape
    depth = ada_w.shape[0]
    aw = attn_out_g.shape[-1]
    sw = ssm_out_g.shape[-1]
    n_heads = aw // HEAD_DIM
    assert b == SUBLANES and aw % LANES == 0 and sw % LANES == 0
    assert l % ATT_T == 0 and l % SSM_T == 0

    mod = _ada_mod(c, ada_w, ada_b)
    head = jnp.arange(aw) // HEAD_DIM
    ones_blk = (head[:MXU_TILE, None] == head[None, :MXU_TILE]).astype(BF16)
    inv_sqrt = LOG2_E / math.sqrt(HEAD_DIM)

    w_in_bf, w_out_bf = w_in.astype(BF16), w_out.astype(BF16)
    w_up_bf, w_down_bf = ffn_w_up.astype(BF16), ffn_w_down.astype(BF16)
    prm = jax.vmap(_ssm_params)(ssm_a_re, ssm_a_im, ssm_log_dt, ssm_b_re, ssm_b_im, ssm_c_re,
                                ssm_c_im, ssm_d, glu_w, glu_b, ssm_out_g)
    gq_all = jnp.tile(q_norm_g * inv_sqrt, (1, n_heads))
    gk_all = jnp.tile(k_norm_g, (1, n_heads))

    for li in range(depth):
        m = mod[li].reshape(b, N_MOD, 1, d)
        sh1, sc1, g1, sh2, sc2, g2 = (m[:, i] for i in range(N_MOD))
        q, k, v, u2 = _in_proj(x, sh1, sc1, norm1_g[li].reshape(1, d), w_in_bf, li,
                               gq_all[li].reshape(1, aw), gk_all[li].reshape(1, aw),
                               ones_blk, aw, sw)
        o_attn = _attn(q, k, v)
        o_ssm = _ssm(u2, prm, li, b)
        x = _mix_ffn(x, o_attn, o_ssm, attn_out_g[li].reshape(1, aw), w_out_bf, li, g1,
                     sh2, sc2, g2, norm2_g[li].reshape(1, d), w_up_bf,
                     ffn_conv_w[li], ffn_conv_b[li].reshape(1, -1), w_down_bf)
    return x
```

```python
import functools
import math

import jax
import jax.numpy as jnp
from jax import lax
from jax.experimental import pallas as pl
from jax.experimental.pallas import tpu as pltpu

F32 = jnp.float32
BF16 = jnp.bfloat16

EPS = 1e-6
HEAD_DIM = 64
SSM_GROUP = 16
SSM_STATE = 64
CONV_WIDTH = 3
N_MOD = 6

LANES = 128
SUBLANES = 8
MXU_TILE = 256
VMEM_LIMIT_BYTES = 56 * 1024 * 1024

ADA_TN = 1536
PROJ_TM = 512
ATT_T = 128
ATT_QS = 2048
ATT_WINDOW = 3
ATT_GROUP = 16
SSM_T = 128
SSM_STAGE_GROUPS = 2
FFN_TM = 512
FFN_CK = 256
FFN_HALO = 16

UNDERFLOW_BOUND = 152.0
LOG2_E = 1.4426950408889634


def _cparams(sem, n_in=0, fuse=()):
    fusion = [i in fuse for i in range(n_in)] if fuse else None
    return pltpu.CompilerParams(dimension_semantics=sem,
                                vmem_limit_bytes=VMEM_LIMIT_BYTES,
                                allow_input_fusion=fusion)


def _dot(a, b):
    return jnp.dot(a, b, preferred_element_type=F32)


def _ada_kernel(c_ref, w_ref, b_ref, o_ref):
    c = c_ref[...]
    c_act = c * jax.nn.sigmoid(c)
    c_hi = c_act.astype(BF16)
    c_lo = (c_act - c_hi.astype(F32)).astype(BF16)
    w = w_ref[0]
    w_hi = w.astype(BF16)
    w_lo = (w - w_hi.astype(F32)).astype(BF16)
    acc = _dot(c_hi, w_hi) + (_dot(c_hi, w_lo) + _dot(c_lo, w_hi))
    o_ref[0] = acc + b_ref[0]


def _ada_mod(c, ada_w, ada_b):
    depth, d, n = ada_w.shape
    b = c.shape[0]
    return pl.pallas_call(
        _ada_kernel,
        grid=(depth, n // ADA_TN),
        in_specs=[
            pl.BlockSpec((b, d), lambda l, j: (0, 0)),
            pl.BlockSpec((1, d, ADA_TN), lambda l, j: (l, 0, j)),
            pl.BlockSpec((1, 1, ADA_TN), lambda l, j: (l, 0, j)),
        ],
        out_specs=pl.BlockSpec((1, b, ADA_TN), lambda l, j: (l, 0, j)),
        out_shape=jax.ShapeDtypeStruct((depth, b, n), F32),
        compiler_params=_cparams(("parallel", "parallel")),
        name="ada_mod",
    )(c, ada_w, ada_b.reshape(depth, 1, n))


def _norm_mod(xf, g, sh, sc):
    inv = lax.rsqrt(jnp.mean(xf * xf, axis=-1, keepdims=True) + EPS)
    return xf * inv * (g * (1.0 + sc)) + sh


def _in_proj_kernel(x_ref, sh_ref, sc_ref, g_ref, w_ref, gq_ref, gk_ref, ones_ref,
                    q_ref, k_ref, v_ref, u_ref, *, aw):
    hb = _norm_mod(x_ref[0], g_ref[...], sh_ref[0], sc_ref[0]).astype(BF16)
    n_pair = aw // LANES

    def head_norm(p, g):
        sq = (p * p).astype(BF16)
        ms = jnp.concatenate([_dot(sq[:, c:c + MXU_TILE], ones_ref[...])
                              for c in range(0, aw, MXU_TILE)], axis=1) * (1.0 / HEAD_DIM)
        return p * lax.rsqrt(ms + EPS) * g

    q = head_norm(_dot(hb, w_ref[:, 0:aw]), gq_ref[...]).astype(BF16)
    for hp in range(n_pair):
        q_ref[0, hp] = q[:, hp * LANES:(hp + 1) * LANES]
    k = head_norm(_dot(hb, w_ref[:, aw:2 * aw]), gk_ref[...]).astype(BF16)
    for hp in range(n_pair):
        k_ref[0, hp] = k[:, hp * LANES:(hp + 1) * LANES]
    v = _dot(hb, w_ref[:, 2 * aw:3 * aw]).astype(BF16)
    for hp in range(n_pair):
        v_ref[0, hp] = v[:, hp * LANES:(hp + 1) * LANES]
    u_ref[...] = _dot(hb, w_ref[:, 3 * aw:]).astype(BF16)


def _layer_block(stacked, li, index_args=2):
    zeros = (0,) * (stacked.ndim - 1)
    if index_args == 1:
        return pl.BlockSpec((None,) + stacked.shape[1:], lambda i: (li,) + zeros)
    return pl.BlockSpec((None,) + stacked.shape[1:], lambda bi, i: (li,) + zeros,
                        pipeline_mode=pl.Buffered(1))


def _in_proj(x, sh, sc, g, w_in_bf, li, gq, gk, ones_blk, aw, sw):
    b, l, d = x.shape
    tm = min(PROJ_TM, l)
    n_pair = aw // LANES
    qkv_shape = jax.ShapeDtypeStruct((b, n_pair, l, LANES), BF16)
    qkv_spec = pl.BlockSpec((1, n_pair, tm, LANES), lambda bi, i: (bi, 0, i, 0))
    vec = lambda n: pl.BlockSpec((1, n), lambda bi, i: (0, 0))
    bvec = pl.BlockSpec((1, 1, d), lambda bi, i: (bi, 0, 0))
    return pl.pallas_call(
        functools.partial(_in_proj_kernel, aw=aw),
        grid=(b, l // tm),
        in_specs=[
            pl.BlockSpec((1, tm, d), lambda bi, i: (bi, i, 0)),
            bvec, bvec, vec(d),
            _layer_block(w_in_bf, li),
            vec(aw), vec(aw),
            pl.BlockSpec(ones_blk.shape, lambda bi, i: (0, 0)),
        ],
        out_specs=[qkv_spec, qkv_spec, qkv_spec,
                   pl.BlockSpec((tm, sw), lambda bi, i: (i, bi))],
        out_shape=[qkv_shape, qkv_shape, qkv_shape,
                   jax.ShapeDtypeStruct((l, b * sw), BF16)],
        compiler_params=_cparams(("parallel", "parallel"), n_in=8, fuse=(4,)),
        name="in_proj",
    )(x, sh, sc, g, w_in_bf, gq, gk, ones_blk)


def _attn_kernel(q_ref, k_ref, v_ref, o_ref, acc_ref, carry_ref, next_ref, bound_ref):
    t = ATT_T
    n_sub = q_ref.shape[2] // t
    base = pl.program_id(2) * n_sub
    lane = lax.broadcasted_iota(jnp.int32, (1, LANES), 1)
    first = lane < HEAD_DIM
    row = lax.broadcasted_iota(jnp.int32, (t, t), 0)
    col = lax.broadcasted_iota(jnp.int32, (t, t), 1)
    later = jnp.where(row > col, 1.0, 0.0).astype(BF16)
    strict = col < row

    def split_heads(ref, j):
        blk = ref[0, 0, pl.ds(pl.multiple_of(j * t, t), t), :]
        zero = jnp.zeros_like(blk)
        return jnp.where(first, blk, zero), jnp.where(first, zero, blk)

    def blocks(items, carry_in=None):
        zz = [lax.dot_general(q2, jnp.concatenate(split_heads(k_ref, j), axis=0),
                              (((1,), (1,)), ((), ())),
                              preferred_element_type=F32) for q2, j, _ in items]
        logb, s_bf, sums = [], [], []
        for (_, _, depth), zz_i in zip(items, zz):
            for h in range(2):
                z = zz_i[:, h * t:(h + 1) * t]
                neg_abs = lax.bitcast_convert_type(
                    lax.bitcast_convert_type(z, jnp.uint32) | jnp.uint32(0x80000000), F32)
                s = jnp.maximum(z, 0.0) + jnp.log2(1.0 + jnp.exp2(neg_abs))
                sm = jnp.where(strict, s, 0.0) if depth == 0 else s
                logb.append(z - s)
                s_bf.append(sm.astype(BF16))
                sums.append(jnp.sum(sm, axis=-1, keepdims=True))
        within = [_dot(x, later) for x in s_bf]
        ws, carries = [], []
        for n, (_, _, depth) in enumerate(items):
            carry = carries[-1] if depth > 0 else carry_in
            w_heads, after = [], []
            for h in range(2):
                arg = logb[2 * n + h] - within[2 * n + h]
                if carry is not None:
                    arg = arg - carry[h]
                w = jnp.exp2(arg)
                if depth == 0:
                    w = jnp.where(strict, w, 0.0)
                w_heads.append(w.astype(BF16))
                after.append(sums[2 * n + h] if carry is None else carry[h] + sums[2 * n + h])
            ws.append(w_heads)
            carries.append(after)
        pvs = []
        for (w0, w1), (_, j, _) in zip(ws, items):
            v0, v1 = split_heads(v_ref, j)
            pvs.append(_dot(w0, v0) + _dot(w1, v1))
        return pvs, carries

    def first_blocks(depths):
        for g in range(0, n_sub, ATT_GROUP):
            first_blocks_of(depths, range(g, min(g + ATT_GROUP, n_sub)))

    def first_blocks_of(depths, group):
        items, owner = [], []
        for k in group:
            q2 = q_ref[0, 0, k * t:(k + 1) * t, :]
            for d in range(depths[k]):
                items.append((q2, base + k - d, d))
                owner.append(k)
        pvs, carries = blocks(items)
        for k in group:
            mine = [n for n, o in enumerate(owner) if o == k]
            acc = pvs[mine[0]]
            for n in mine[1:]:
                acc = acc + pvs[n]
            carry = carries[mine[-1]]
            acc_ref[k] = acc
            for h in range(2):
                carry_ref[k, h] = carry[h]
            next_ref[k] = base + k - depths[k]
            bound_ref[k] = jnp.min(jnp.minimum(carry[0], carry[1]))

    @pl.when(base >= ATT_WINDOW - 1)
    def _():
        first_blocks([ATT_WINDOW] * n_sub)

    @pl.when(base < ATT_WINDOW - 1)
    def _():
        first_blocks([ATT_WINDOW if k >= ATT_WINDOW - 1 else 1 for k in range(n_sub)])

    def finish(k, _):
        rows = pl.ds(pl.multiple_of(k * t, t), t)
        q2 = q_ref[0, 0, rows, :]

        def cond(state):
            j, bound = state
            return jnp.logical_and(j >= 0, bound < UNDERFLOW_BOUND)

        def body(state):
            j, _ = state
            carry = [carry_ref[k, 0], carry_ref[k, 1]]
            (pv,), (carry,) = blocks([(q2, j, -1)], carry)
            acc_ref[k] += pv
            for h in range(2):
                carry_ref[k, h] = carry[h]
            return j - 1, jnp.min(jnp.minimum(carry[0], carry[1]))

        lax.while_loop(cond, body, (next_ref[k], bound_ref[k]))
        o_ref[0, rows, :] = acc_ref[k].astype(o_ref.dtype)
        return 0

    lax.fori_loop(0, n_sub, finish, 0)


def _attn(q, k, v):
    b, n_pair, l, _ = q.shape
    t = ATT_T
    qs = min(ATT_QS, l)
    kv_spec = pl.BlockSpec((1, 1, l, LANES), lambda bi, hp, i: (bi, hp, 0, 0))
    return pl.pallas_call(
        _attn_kernel,
        grid=(b, n_pair, l // qs),
        in_specs=[pl.BlockSpec((1, 1, qs, LANES), lambda bi, hp, i: (bi, hp, i, 0)),
                  kv_spec, kv_spec],
        out_specs=pl.BlockSpec((1, qs, LANES), lambda bi, hp, i: (bi, i, hp)),
        out_shape=jax.ShapeDtypeStruct((b, l, n_pair * LANES), BF16),
        scratch_shapes=[pltpu.VMEM((qs // t, t, LANES), F32),
                        pltpu.VMEM((qs // t, 2, t, 1), F32),
                        pltpu.SMEM((qs // t,), jnp.int32),
                        pltpu.SMEM((qs // t,), F32)],
        compiler_params=_cparams(("parallel", "parallel", "parallel")),
        name="attn",
    )(q, k, v)


def _ssm_kernel(u_ref, wb_ref, are_ref, aim_ref, wcre_ref, wcim_ref, wft_ref, d_ref,
                wglu_ref, bglu_ref, g_ref, o_ref, sre_ref, sim_ref, st_ref, ulast_ref, y_ref,
                *, n_oct, batch):
    n_tok = u_ref.shape[0]
    n_pair = n_tok // 2
    rows = n_pair * batch
    sw = u_ref.shape[1] // batch
    oct_states = sre_ref.shape[2]

    @pl.when(pl.program_id(0) == 0)
    def _():
        st_ref[...] = jnp.zeros_like(st_ref)
        ulast_ref[...] = jnp.zeros_like(ulast_ref)

    def split_tokens(x):
        x3 = x.reshape(n_pair, 2 * batch, LANES)
        return x3[:, :batch].reshape(rows, LANES), x3[:, batch:].reshape(rows, LANES)

    def merge_tokens(even, odd):
        return jnp.concatenate([even.reshape(n_pair, batch, LANES),
                                odd.reshape(n_pair, batch, LANES)], axis=1).reshape(-1, LANES)

    for bi in range(batch):
        for o in range(n_oct):
            c0 = bi * sw + o * LANES
            y_ref[o, pl.ds(bi, n_tok, stride=batch), :] = u_ref[:, c0:c0 + LANES].astype(F32)
    def input_proj(o):
        even, odd = split_tokens(y_ref[o])
        prev_odd = jnp.concatenate([ulast_ref[o], odd[:rows - batch]], axis=0)
        ulast_ref[o] = odd[rows - batch:]
        bu = _dot(jnp.concatenate([prev_odd, even], axis=1).astype(BF16), wb_ref[o])
        sre_ref[o] = bu[:, :oct_states]
        sim_ref[o] = bu[:, oct_states:]

    def recurrence(octs):
        cols = [slice(o * oct_states, (o + 1) * oct_states) for o in octs]
        are = [jnp.broadcast_to(are_ref[:, c], (batch, oct_states)) for c in cols]
        aim = [jnp.broadcast_to(aim_ref[:, c], (batch, oct_states)) for c in cols]
        s_re = [st_ref[0, :, c] for c in cols]
        s_im = [st_ref[1, :, c] for c in cols]
        for t in range(n_pair):
            r = slice(t * batch, (t + 1) * batch)
            for n, o in enumerate(octs):
                s_re[n], s_im[n] = (are[n] * s_re[n] - aim[n] * s_im[n] + sre_ref[o, r, :],
                                    are[n] * s_im[n] + aim[n] * s_re[n] + sim_ref[o, r, :])
                sre_ref[o, r, :] = s_re[n]
                sim_ref[o, r, :] = s_im[n]
        for n, c in enumerate(cols):
            st_ref[0, :, c] = s_re[n]
            st_ref[1, :, c] = s_im[n]

    def output_proj(octs):
        lanes = [slice(o * LANES, (o + 1) * LANES) for o in octs]
        tokens = [split_tokens(y_ref[o]) for o in octs]
        y2 = [_dot(sre_ref[o].astype(BF16), wcre_ref[o])
              + _dot(sim_ref[o].astype(BF16), wcim_ref[o]) for o in octs]
        feed = [_dot(odd.astype(BF16), wft_ref[o]) for o, (_, odd) in zip(octs, tokens)]
        ys = []
        for n, (even, odd) in enumerate(tokens):
            d = d_ref[:, lanes[n]]
            ys.append(jax.nn.gelu(jnp.concatenate(
                [y2[n][:, :LANES] + d * even, y2[n][:, LANES:] + feed[n] + d * odd], axis=1)))
        pre = [_dot(y.astype(BF16), wglu_ref[o]) for o, y in zip(octs, ys)]
        for n, o in enumerate(octs):
            bias = jnp.concatenate([bglu_ref[:, lanes[n]]] * 2, axis=1)
            out = ys[n] * jax.nn.sigmoid(pre[n] + bias)
            y_ref[o] = merge_tokens(out[:, :LANES], out[:, LANES:])

    pairs = [list(range(o, min(o + SSM_STAGE_GROUPS, n_oct)))
             for o in range(0, n_oct, SSM_STAGE_GROUPS)]
    for o in pairs[0]:
        input_proj(o)
    for n, octs in enumerate(pairs):
        if n + 1 < len(pairs):
            for o in pairs[n + 1]:
                input_proj(o)
        recurrence(octs)
        if n >= 1:
            output_proj(pairs[n - 1])
    output_proj(pairs[-1])
    sumsq = 0.0
    for o in range(n_oct):
        out = y_ref[o]
        sumsq = sumsq + jnp.sum(out * out, axis=-1, keepdims=True)
    inv = lax.rsqrt(sumsq * (1.0 / sw) + EPS)
    for o in range(n_oct):
        y_ref[o] = y_ref[o] * inv * g_ref[:, o * LANES:(o + 1) * LANES]
    for bi in range(batch):
        for o in range(n_oct):
            c0 = bi * sw + o * LANES
            o_ref[:, c0:c0 + LANES] = (
                y_ref[o, pl.ds(bi, n_tok, stride=batch), :].astype(o_ref.dtype))


def _ssm(u2, prm, li, batch):
    l, width = u2.shape
    sw = width // batch
    n_oct = sw // LANES
    n_state = prm["are"].shape[-1]
    rows = SSM_T * batch
    pair_rows = rows // 2
    full = lambda a: _layer_block(a, li, index_args=1)
    names = ["wb", "are", "aim", "wcre", "wcim", "wft", "d", "wglu", "bglu", "g"]
    return pl.pallas_call(
        functools.partial(_ssm_kernel, n_oct=n_oct, batch=batch),
        grid=(l // SSM_T,),
        in_specs=[pl.BlockSpec((SSM_T, width), lambda i: (i, 0))]
                 + [full(prm[n]) for n in names],
        out_specs=pl.BlockSpec((SSM_T, width), lambda i: (i, 0)),
        out_shape=jax.ShapeDtypeStruct((l, width), BF16),
        scratch_shapes=[pltpu.VMEM((n_oct, pair_rows, n_state // n_oct), F32),
                        pltpu.VMEM((n_oct, pair_rows, n_state // n_oct), F32),
                        pltpu.VMEM((2, batch, n_state), F32),
                        pltpu.VMEM((n_oct, batch, LANES), F32),
                        pltpu.VMEM((n_oct, rows, LANES), F32)],
        compiler_params=_cparams(("arbitrary",)),
        name="ssm",
    )(u2, *[prm[n] for n in names])


def _ssm_params(a_re, a_im, log_dt, b_re, b_im, c_re, c_im, d_skip, glu_w, glu_b, out_g):
    g, p, h = b_re.shape
    per_oct = LANES // h
    n_oct = g // per_oct
    dt = jnp.exp(log_dt)[:, None]
    mag = jnp.exp(dt * a_re)
    abar_re = mag * jnp.cos(dt * a_im)
    abar_im = mag * jnp.sin(dt * a_im)
    em_re = abar_re - 1.0
    em_im = abar_im
    den = a_re * a_re + a_im * a_im
    f_re = (em_re * a_re + em_im * a_im) / den
    f_im = (em_im * a_re - em_re * a_im) / den
    bb_re = f_re[..., None] * b_re - f_im[..., None] * b_im
    bb_im = f_re[..., None] * b_im + f_im[..., None] * b_re
    eye = jnp.eye(per_oct, dtype=F32)

    def in_blockdiag(bb):
        bb = bb.reshape(n_oct, per_oct, p, h)
        m = jnp.einsum("ogph,gk->oghkp", bb, eye)
        return m.reshape(n_oct, per_oct * h, per_oct * p)

    def out_blockdiag(c):
        c = c.reshape(n_oct, per_oct, h, p)
        m = jnp.einsum("oghp,gk->ogpkh", c, eye)
        return m.reshape(n_oct, per_oct * p, per_oct * h)

    def chan_blockdiag(m):
        m = jnp.einsum("oghk,gf->oghfk", m.reshape(n_oct, per_oct, h, h), eye)
        return m.reshape(n_oct, per_oct * h, per_oct * h)

    are_c, aim_c = abar_re[..., None], abar_im[..., None]
    ab_re, ab_im = are_c * bb_re - aim_c * bb_im, are_c * bb_im + aim_c * bb_re
    are_r, aim_r = abar_re[:, None, :], abar_im[:, None, :]
    ca_re, ca_im = c_re * are_r - c_im * aim_r, c_re * aim_r + c_im * are_r
    feed = (jnp.einsum("gop,gpi->gio", c_re, bb_re) - jnp.einsum("gop,gpi->gio", c_im, bb_im))
    wb = jnp.concatenate(
        [jnp.concatenate([in_blockdiag(ab_re), in_blockdiag(ab_im)], axis=-1),
         jnp.concatenate([in_blockdiag(bb_re), in_blockdiag(bb_im)], axis=-1)], axis=1)
    glu1 = chan_blockdiag(glu_w)
    zero = jnp.zeros_like(glu1)
    wglu = jnp.concatenate([jnp.concatenate([glu1, zero], axis=-1),
                            jnp.concatenate([zero, glu1], axis=-1)], axis=1)
    return {
        "wb": wb.astype(BF16),
        "are": (abar_re * abar_re - abar_im * abar_im).reshape(1, g * p),
        "aim": (2.0 * abar_re * abar_im).reshape(1, g * p),
        "wcre": jnp.concatenate([out_blockdiag(c_re), out_blockdiag(ca_re)], -1).astype(BF16),
        "wcim": jnp.concatenate([out_blockdiag(-c_im), out_blockdiag(-ca_im)], -1).astype(BF16),
        "wft": chan_blockdiag(feed).astype(BF16),
        "d": d_skip.reshape(1, g * h), "wglu": wglu.astype(BF16),
        "bglu": glu_b.reshape(1, g * h), "g": out_g.reshape(1, g * h),
    }


def _mix_ffn_kernel(x_ref, xh_ref, oa_ref, oah_ref, os_ref, osh_ref, ga_ref, wa_ref, ws_ref,
                    g1_ref, sh_ref, sc_ref, g2_ref, g_ref, wup_ref, cw_ref, cb_ref, wdn_ref,
                    o_ref, perm_ref, h_ref, up_ref, a_ref, *, d_ff):
    tm, d = x_ref.shape[1], x_ref.shape[2]
    n_grp = tm // SUBLANES
    n_slab = d // LANES
    n_chunk = d_ff // FFN_CK
    g, sh, sc = g_ref[...], sh_ref[0], sc_ref[0]

    def natural_rows(s):
        return slice(s * n_grp, (s + 1) * n_grp)

    def interleaved_rows(s):
        return pl.ds(s, n_grp, stride=SUBLANES)

    oa = jnp.concatenate([oa_ref[0], oah_ref[0]], axis=0).astype(F32)
    inv = lax.rsqrt(jnp.mean(oa * oa, axis=-1, keepdims=True) + EPS)
    mixed = (_dot((oa * inv * ga_ref[...]).astype(BF16), wa_ref[...])
             + _dot(jnp.concatenate([os_ref[...], osh_ref[...]], axis=0), ws_ref[...]))
    o_ref[0] = x_ref[0] + g1_ref[0] * mixed[:tm]
    x1_halo = xh_ref[0] + g1_ref[0] * mixed[tm:]

    hn = _norm_mod(o_ref[0], g, sh, sc)
    for slab in range(n_slab):
        lanes = slice(slab * LANES, (slab + 1) * LANES)
        for s in range(SUBLANES):
            perm_ref[slab, interleaved_rows(s), :] = hn[natural_rows(s), lanes]
    for slab in range(n_slab):
        h_ref[:tm, slab * LANES:(slab + 1) * LANES] = perm_ref[slab].astype(BF16)
    keep = (pl.program_id(1) > 0).astype(F32)
    h_ref[tm:, :] = (_norm_mod(x1_halo, g, sh, sc) * keep).astype(BF16)
    sublane = lax.broadcasted_iota(jnp.int32, (SUBLANES, FFN_CK), 0)

    def chunk(start):
        if not isinstance(start, int):
            start = pl.multiple_of(start, FFN_CK)
        return pl.ds(start, FFN_CK)

    def cols(c, part):
        return chunk(part * d_ff + c * FFN_CK)

    def wrapped(group, before):
        return jnp.where(sublane == 0, before, pltpu.roll(group, 1, axis=0))

    def up_proj(c, slot):
        for part in range(2):
            r = _dot(h_ref[...], wup_ref[:, cols(c, part)])
            pad = (CONV_WIDTH - 1) * SUBLANES
            up_ref[slot, part, pad:, :] = r[:tm]
            for back in range(1, CONV_WIDTH):
                up_ref[slot, part, pad - back * SUBLANES:pad - (back - 1) * SUBLANES, :] = wrapped(
                    r[tm - back * SUBLANES:tm - (back - 1) * SUBLANES],
                    r[tm + FFN_HALO - back:tm + FFN_HALO - back + 1])

    def conv(c, slot, part):
        out = cb_ref[:, cols(c, part)]
        for i in range(CONV_WIDTH):
            off = i * SUBLANES
            out = out + up_ref[slot, part, off:off + tm, :] * cw_ref[i:i + 1, cols(c, part)]
        return out

    def activate(c, slot):
        a_ref[:, chunk(c * FFN_CK)] = (
            jax.nn.gelu(conv(c, slot, 1)) * conv(c, slot, 0)).astype(BF16)

    for c in range(n_chunk):
        up_proj(c, c % 2)
        activate(c, c % 2)
    y = _dot(a_ref[...], wdn_ref[...])
    for slab in range(n_slab):
        perm_ref[slab] = y[:, slab * LANES:(slab + 1) * LANES]
    for slab in range(n_slab):
        lanes = slice(slab * LANES, (slab + 1) * LANES)
        for s in range(SUBLANES):
            o_ref[0, natural_rows(s), lanes] += (
                g2_ref[0, :, lanes] * perm_ref[slab, interleaved_rows(s), :])


def _mix_ffn(x, o_attn, o_ssm2, ga, w_out, li, g1, sh, sc, g2, g, wup, cw, cb, wdn):
    b, l, d = x.shape
    aw = o_attn.shape[-1]
    sw = o_ssm2.shape[-1] // b
    d_ff = wdn.shape[1]
    assert aw == sw
    w_half = lambda half: pl.BlockSpec((None, aw, d), lambda bi, i: (li, half, 0),
                                       pipeline_mode=pl.Buffered(1))
    tm = min(FFN_TM, l)
    halo_per_tile = tm // FFN_HALO
    halo = lambda i: jnp.maximum(i * halo_per_tile - 1, 0)
    bvec = pl.BlockSpec((1, 1, d), lambda bi, i: (bi, 0, 0))
    const = lambda a: pl.BlockSpec(a.shape, lambda bi, i: (0,) * a.ndim,
                                   pipeline_mode=pl.Buffered(1))
    return pl.pallas_call(
        functools.partial(_mix_ffn_kernel, d_ff=d_ff),
        grid=(b, l // tm),
        in_specs=[
            pl.BlockSpec((1, tm, d), lambda bi, i: (bi, i, 0)),
            pl.BlockSpec((1, FFN_HALO, d), lambda bi, i: (bi, halo(i), 0)),
            pl.BlockSpec((1, tm, aw), lambda bi, i: (bi, i, 0)),
            pl.BlockSpec((1, FFN_HALO, aw), lambda bi, i: (bi, halo(i), 0)),
            pl.BlockSpec((tm, sw), lambda bi, i: (i, bi)),
            pl.BlockSpec((FFN_HALO, sw), lambda bi, i: (halo(i), bi)),
            const(ga), w_half(0), w_half(1), bvec,
            bvec, bvec, bvec, const(g), _layer_block(wup, li), const(cw), const(cb),
            _layer_block(wdn, li),
        ],
        out_specs=pl.BlockSpec((1, tm, d), lambda bi, i: (bi, i, 0)),
        out_shape=jax.ShapeDtypeStruct(x.shape, F32),
        scratch_shapes=[pltpu.VMEM((d // LANES, tm, LANES), F32),
                        pltpu.VMEM((tm + FFN_HALO, d), BF16),
                        pltpu.VMEM((2, 2, tm + (CONV_WIDTH - 1) * SUBLANES, FFN_CK), F32),
                        pltpu.VMEM((tm, d_ff), BF16)],
        compiler_params=_cparams(("parallel", "parallel"), n_in=18, fuse=(7, 8, 14, 17)),
        name="mix_ffn",
    )(x, x, o_attn, o_attn, o_ssm2, o_ssm2, ga, w_out, w_out, g1, sh, sc, g2, g, wup, cw, cb,
      wdn)


def kernel(x, c, ada_w, ada_b, norm1_g, w_in, q_norm_g, k_norm_g, ssm_a_re, ssm_a_im,
           ssm_log_dt, ssm_b_re, ssm_b_im, ssm_c_re, ssm_c_im, ssm_d, glu_w, glu_b,
           attn_out_g, ssm_out_g, w_out, norm2_g, ffn_w_up, ffn_conv_w, ffn_conv_b,
           ffn_w_down):
    b, l, d = x.shape
    depth = ada_w.shape[0]
    aw = attn_out_g.shape[-1]
    sw = ssm_out_g.shape[-1]
    n_heads = aw // HEAD_DIM
    assert b == SUBLANES and aw % LANES == 0 and sw % LANES == 0
    assert l % ATT_T == 0 and l % SSM_T == 0

    mod = _ada_mod(c, ada_w, ada_b)
    head = jnp.arange(aw) // HEAD_DIM
    ones_blk = (head[:MXU_TILE, None] == head[None, :MXU_TILE]).astype(BF16)
    inv_sqrt = LOG2_E / math.sqrt(HEAD_DIM)

    w_in_bf, w_out_bf = w_in.astype(BF16), w_out.astype(BF16)
    w_up_bf, w_down_bf = ffn_w_up.astype(BF16), ffn_w_down.astype(BF16)
    prm = jax.vmap(_ssm_params)(ssm_a_re, ssm_a_im, ssm_log_dt, ssm_b_re, ssm_b_im, ssm_c_re,
                                ssm_c_im, ssm_d, glu_w, glu_b, ssm_out_g)
    gq_all = jnp.tile(q_norm_g * inv_sqrt, (1, n_heads))
    gk_all = jnp.tile(k_norm_g, (1, n_heads))

    for li in range(depth):
        m = mod[li].reshape(b, N_MOD, 1, d)
        sh1, sc1, g1, sh2, sc2, g2 = (m[:, i] for i in range(N_MOD))
        q, k, v, u2 = _in_proj(x, sh1, sc1, norm1_g[li].reshape(1, d), w_in_bf, li,
                               gq_all[li].reshape(1, aw), gk_all[li].reshape(1, aw),
                               ones_blk, aw, sw)
        o_attn = _attn(q, k, v)
        o_ssm = _ssm(u2, prm, li, b)
        x = _mix_ffn(x, o_attn, o_ssm, attn_out_g[li].reshape(1, aw), w_out_bf, li, g1,
                     sh2, sc2, g2, norm2_g[li].reshape(1, d), w_up_bf,
                     ffn_conv_w[li], ffn_conv_b[li].reshape(1, -1), w_down_bf)
    return x
```

```python
import functools
import math

import jax
import jax.numpy as jnp
from jax import lax
from jax.experimental import pallas as pl
from jax.experimental.pallas import tpu as pltpu

F32 = jnp.float32
BF16 = jnp.bfloat16

EPS = 1e-6
HEAD_DIM = 64
SSM_GROUP = 16
SSM_STATE = 64
CONV_WIDTH = 3
N_MOD = 6

LANES = 128
SUBLANES = 8
MXU_TILE = 256
VMEM_LIMIT_BYTES = 56 * 1024 * 1024

ADA_TN = 1536
PROJ_TM = 1024
ATT_T = 128
ATT_QS = 2048
ATT_WINDOW = 3
ATT_GROUP = 16
SSM_T = 128
SSM_STAGE_GROUPS = 2
FFN_TM = 512
FFN_CK = 256
FFN_HALO = 16

UNDERFLOW_BOUND = 152.0
LOG2_E = 1.4426950408889634


def _cparams(sem):
    return pltpu.CompilerParams(dimension_semantics=sem,
                                vmem_limit_bytes=VMEM_LIMIT_BYTES)


def _dot(a, b):
    return jnp.dot(a, b, preferred_element_type=F32)


def _ada_kernel(c_ref, w_ref, b_ref, o_ref):
    c = c_ref[...]
    c_act = c * jax.nn.sigmoid(c)
    c_hi = c_act.astype(BF16)
    c_lo = (c_act - c_hi.astype(F32)).astype(BF16)
    w = w_ref[0]
    w_hi = w.astype(BF16)
    w_lo = (w - w_hi.astype(F32)).astype(BF16)
    acc = _dot(c_hi, w_hi) + (_dot(c_hi, w_lo) + _dot(c_lo, w_hi))
    o_ref[0] = acc + b_ref[0]


def _ada_mod(c, ada_w, ada_b):
    depth, d, n = ada_w.shape
    b = c.shape[0]
    return pl.pallas_call(
        _ada_kernel,
        grid=(depth, n // ADA_TN),
        in_specs=[
            pl.BlockSpec((b, d), lambda l, j: (0, 0)),
            pl.BlockSpec((1, d, ADA_TN), lambda l, j: (l, 0, j)),
            pl.BlockSpec((1, 1, ADA_TN), lambda l, j: (l, 0, j)),
        ],
        out_specs=pl.BlockSpec((1, b, ADA_TN), lambda l, j: (l, 0, j)),
        out_shape=jax.ShapeDtypeStruct((depth, b, n), F32),
        compiler_params=_cparams(("parallel", "parallel")),
        name="ada_mod",
    )(c, ada_w, ada_b.reshape(depth, 1, n))


def _norm_mod(xf, g, sh, sc):
    inv = lax.rsqrt(jnp.mean(xf * xf, axis=-1, keepdims=True) + EPS)
    return xf * inv * (g * (1.0 + sc)) + sh


def _in_proj_kernel(x_ref, sh_ref, sc_ref, g_ref, w_ref, gq_ref, gk_ref, ones_ref,
                    q_ref, k_ref, v_ref, u_ref, *, aw):
    hb = _norm_mod(x_ref[0], g_ref[...], sh_ref[0], sc_ref[0]).astype(BF16)
    n_pair = aw // LANES

    def head_norm(p, g):
        sq = (p * p).astype(BF16)
        ms = jnp.concatenate([_dot(sq[:, c:c + MXU_TILE], ones_ref[...])
                              for c in range(0, aw, MXU_TILE)], axis=1) * (1.0 / HEAD_DIM)
        return p * lax.rsqrt(ms + EPS) * g

    q = head_norm(_dot(hb, w_ref[:, 0:aw]), gq_ref[...]).astype(BF16)
    for hp in range(n_pair):
        q_ref[0, hp] = q[:, hp * LANES:(hp + 1) * LANES]
    k = head_norm(_dot(hb, w_ref[:, aw:2 * aw]), gk_ref[...]).astype(BF16)
    for hp in range(n_pair):
        k_ref[0, hp] = k[:, hp * LANES:(hp + 1) * LANES]
    v = _dot(hb, w_ref[:, 2 * aw:3 * aw]).astype(BF16)
    for hp in range(n_pair):
        v_ref[0, hp] = v[:, hp * LANES:(hp + 1) * LANES]
    u_ref[...] = _dot(hb, w_ref[:, 3 * aw:]).astype(BF16)


def _layer_block(stacked, li, index_args=2):
    zeros = (0,) * (stacked.ndim - 1)
    if index_args == 1:
        return pl.BlockSpec((None,) + stacked.shape[1:], lambda i: (li,) + zeros)
    return pl.BlockSpec((None,) + stacked.shape[1:], lambda bi, i: (li,) + zeros,
                        pipeline_mode=pl.Buffered(1))


def _in_proj(x, sh, sc, g, w_in_bf, li, gq, gk, ones_blk, aw, sw):
    b, l, d = x.shape
    tm = min(PROJ_TM, l)
    n_pair = aw // LANES
    qkv_shape = jax.ShapeDtypeStruct((b, n_pair, l, LANES), BF16)
    qkv_spec = pl.BlockSpec((1, n_pair, tm, LANES), lambda bi, i: (bi, 0, i, 0))
    vec = lambda n: pl.BlockSpec((1, n), lambda bi, i: (0, 0))
    bvec = pl.BlockSpec((1, 1, d), lambda bi, i: (bi, 0, 0))
    return pl.pallas_call(
        functools.partial(_in_proj_kernel, aw=aw),
        grid=(b, l // tm),
        in_specs=[
            pl.BlockSpec((1, tm, d), lambda bi, i: (bi, i, 0)),
            bvec, bvec, vec(d),
            _layer_block(w_in_bf, li),
            vec(aw), vec(aw),
            pl.BlockSpec(ones_blk.shape, lambda bi, i: (0, 0)),
        ],
        out_specs=[qkv_spec, qkv_spec, qkv_spec,
                   pl.BlockSpec((tm, sw), lambda bi, i: (i, bi))],
        out_shape=[qkv_shape, qkv_shape, qkv_shape,
                   jax.ShapeDtypeStruct((l, b * sw), BF16)],
        compiler_params=_cparams(("parallel", "parallel")),
        name="in_proj",
    )(x, sh, sc, g, w_in_bf, gq, gk, ones_blk)


def _attn_kernel(q_ref, k_ref, v_ref, o_ref, acc_ref, carry_ref, next_ref, bound_ref):
    t = ATT_T
    n_sub = q_ref.shape[2] // t
    base = pl.program_id(2) * n_sub
    lane = lax.broadcasted_iota(jnp.int32, (1, LANES), 1)
    first = lane < HEAD_DIM
    row = lax.broadcasted_iota(jnp.int32, (t, t), 0)
    col = lax.broadcasted_iota(jnp.int32, (t, t), 1)
    later = jnp.where(row > col, 1.0, 0.0).astype(BF16)
    strict = col < row

    def split_heads(ref, j):
        blk = ref[0, 0, pl.ds(pl.multiple_of(j * t, t), t), :]
        zero = jnp.zeros_like(blk)
        return jnp.where(first, blk, zero), jnp.where(first, zero, blk)

    def blocks(items, carry_in=None):
        zz = [lax.dot_general(q2, jnp.concatenate(split_heads(k_ref, j), axis=0),
                              (((1,), (1,)), ((), ())),
                              preferred_element_type=F32) for q2, j, _ in items]
        logb, s_bf, sums = [], [], []
        for (_, _, depth), zz_i in zip(items, zz):
            for h in range(2):
                z = zz_i[:, h * t:(h + 1) * t]
                neg_abs = lax.bitcast_convert_type(
                    lax.bitcast_convert_type(z, jnp.uint32) | jnp.uint32(0x80000000), F32)
                s = jnp.maximum(z, 0.0) + jnp.log2(1.0 + jnp.exp2(neg_abs))
                sm = jnp.where(strict, s, 0.0) if depth == 0 else s
                logb.append(z - s)
                s_bf.append(sm.astype(BF16))
                sums.append(jnp.sum(sm, axis=-1, keepdims=True))
        within = [_dot(x, later) for x in s_bf]
        ws, carries = [], []
        for n, (_, _, depth) in enumerate(items):
            carry = carries[-1] if depth > 0 else carry_in
            w_heads, after = [], []
            for h in range(2):
                arg = logb[2 * n + h] - within[2 * n + h]
                if carry is not None:
                    arg = arg - carry[h]
                w = jnp.exp2(arg)
                if depth == 0:
                    w = jnp.where(strict, w, 0.0)
                w_heads.append(w.astype(BF16))
                after.append(sums[2 * n + h] if carry is None else carry[h] + sums[2 * n + h])
            ws.append(w_heads)
            carries.append(after)
        pvs = []
        for (w0, w1), (_, j, _) in zip(ws, items):
            v0, v1 = split_heads(v_ref, j)
            pvs.append(_dot(w0, v0) + _dot(w1, v1))
        return pvs, carries

    def first_blocks(depths):
        for g in range(0, n_sub, ATT_GROUP):
            first_blocks_of(depths, range(g, min(g + ATT_GROUP, n_sub)))

    def first_blocks_of(depths, group):
        items, owner = [], []
        for k in group:
            q2 = q_ref[0, 0, k * t:(k + 1) * t, :]
            for d in range(depths[k]):
                items.append((q2, base + k - d, d))
                owner.append(k)
        pvs, carries = blocks(items)
        for k in group:
            mine = [n for n, o in enumerate(owner) if o == k]
            acc = pvs[mine[0]]
            for n in mine[1:]:
                acc = acc + pvs[n]
            carry = carries[mine[-1]]
            acc_ref[k] = acc
            for h in range(2):
                carry_ref[k, h] = carry[h]
            next_ref[k] = base + k - depths[k]
            bound_ref[k] = jnp.min(jnp.minimum(carry[0], carry[1]))

    @pl.when(base >= ATT_WINDOW - 1)
    def _():
        first_blocks([ATT_WINDOW] * n_sub)

    @pl.when(base < ATT_WINDOW - 1)
    def _():
        first_blocks([ATT_WINDOW if k >= ATT_WINDOW - 1 else 1 for k in range(n_sub)])

    def finish(k, _):
        rows = pl.ds(pl.multiple_of(k * t, t), t)
        q2 = q_ref[0, 0, rows, :]

        def cond(state):
            j, bound = state
            return jnp.logical_and(j >= 0, bound < UNDERFLOW_BOUND)

        def body(state):
            j, _ = state
            carry = [carry_ref[k, 0], carry_ref[k, 1]]
            (pv,), (carry,) = blocks([(q2, j, -1)], carry)
            acc_ref[k] += pv
            for h in range(2):
                carry_ref[k, h] = carry[h]
            return j - 1, jnp.min(jnp.minimum(carry[0], carry[1]))

        lax.while_loop(cond, body, (next_ref[k], bound_ref[k]))
        o_ref[0, rows, :] = acc_ref[k].astype(o_ref.dtype)
        return 0

    lax.fori_loop(0, n_sub, finish, 0)


def _attn(q, k, v):
    b, n_pair, l, _ = q.shape
    t = ATT_T
    qs = min(ATT_QS, l)
    kv_spec = pl.BlockSpec((1, 1, l, LANES), lambda bi, hp, i: (bi, hp, 0, 0))
    return pl.pallas_call(
        _attn_kernel,
        grid=(b, n_pair, l // qs),
        in_specs=[pl.BlockSpec((1, 1, qs, LANES), lambda bi, hp, i: (bi, hp, i, 0)),
                  kv_spec, kv_spec],
        out_specs=pl.BlockSpec((1, qs, LANES), lambda bi, hp, i: (bi, i, hp)),
        out_shape=jax.ShapeDtypeStruct((b, l, n_pair * LANES), BF16),
        scratch_shapes=[pltpu.VMEM((qs // t, t, LANES), F32),
                        pltpu.VMEM((qs // t, 2, t, 1), F32),
                        pltpu.SMEM((qs // t,), jnp.int32),
                        pltpu.SMEM((qs // t,), F32)],
        compiler_params=_cparams(("parallel", "parallel", "parallel")),
        name="attn",
    )(q, k, v)


def _ssm_kernel(u_ref, wb_ref, are_ref, aim_ref, wcre_ref, wcim_ref, wft_ref, d_ref,
                wglu_ref, bglu_ref, g_ref, o_ref, sre_ref, sim_ref, st_ref, ulast_ref, y_ref,
                *, n_oct, batch):
    n_tok = u_ref.shape[0]
    n_pair = n_tok // 2
    rows = n_pair * batch
    sw = u_ref.shape[1] // batch
    oct_states = sre_ref.shape[2]

    @pl.when(pl.program_id(0) == 0)
    def _():
        st_ref[...] = jnp.zeros_like(st_ref)
        ulast_ref[...] = jnp.zeros_like(ulast_ref)

    def split_tokens(x):
        x3 = x.reshape(n_pair, 2 * batch, LANES)
        return x3[:, :batch].reshape(rows, LANES), x3[:, batch:].reshape(rows, LANES)

    def merge_tokens(even, odd):
        return jnp.concatenate([even.reshape(n_pair, batch, LANES),
                                odd.reshape(n_pair, batch, LANES)], axis=1).reshape(-1, LANES)

    for bi in range(batch):
        for o in range(n_oct):
            c0 = bi * sw + o * LANES
            y_ref[o, pl.ds(bi, n_tok, stride=batch), :] = u_ref[:, c0:c0 + LANES].astype(F32)
    def input_proj(o):
        even, odd = split_tokens(y_ref[o])
        prev_odd = jnp.concatenate([ulast_ref[o], odd[:rows - batch]], axis=0)
        ulast_ref[o] = odd[rows - batch:]
        bu = _dot(jnp.concatenate([prev_odd, even], axis=1).astype(BF16), wb_ref[o])
        sre_ref[o] = bu[:, :oct_states]
        sim_ref[o] = bu[:, oct_states:]

    def recurrence(octs):
        cols = [slice(o * oct_states, (o + 1) * oct_states) for o in octs]
        are = [jnp.broadcast_to(are_ref[:, c], (batch, oct_states)) for c in cols]
        aim = [jnp.broadcast_to(aim_ref[:, c], (batch, oct_states)) for c in cols]
        s_re = [st_ref[0, :, c] for c in cols]
        s_im = [st_ref[1, :, c] for c in cols]
        for t in range(n_pair):
            r = slice(t * batch, (t + 1) * batch)
            for n, o in enumerate(octs):
                s_re[n], s_im[n] = (are[n] * s_re[n] - aim[n] * s_im[n] + sre_ref[o, r, :],
                                    are[n] * s_im[n] + aim[n] * s_re[n] + sim_ref[o, r, :])
                sre_ref[o, r, :] = s_re[n]
                sim_ref[o, r, :] = s_im[n]
        for n, c in enumerate(cols):
            st_ref[0, :, c] = s_re[n]
            st_ref[1, :, c] = s_im[n]

    def output_proj(octs):
        lanes = [slice(o * LANES, (o + 1) * LANES) for o in octs]
        tokens = [split_tokens(y_ref[o]) for o in octs]
        y2 = [_dot(sre_ref[o].astype(BF16), wcre_ref[o])
              + _dot(sim_ref[o].astype(BF16), wcim_ref[o]) for o in octs]
        feed = [_dot(odd.astype(BF16), wft_ref[o]) for o, (_, odd) in zip(octs, tokens)]
        ys = []
        for n, (even, odd) in enumerate(tokens):
            d = d_ref[:, lanes[n]]
            ys.append(jax.nn.gelu(jnp.concatenate(
                [y2[n][:, :LANES] + d * even, y2[n][:, LANES:] + feed[n] + d * odd], axis=1)))
        pre = [_dot(y.astype(BF16), wglu_ref[o]) for o, y in zip(octs, ys)]
        for n, o in enumerate(octs):
            bias = jnp.concatenate([bglu_ref[:, lanes[n]]] * 2, axis=1)
            out = ys[n] * jax.nn.sigmoid(pre[n] + bias)
            y_ref[o] = merge_tokens(out[:, :LANES], out[:, LANES:])

    pairs = [list(range(o, min(o + SSM_STAGE_GROUPS, n_oct)))
             for o in range(0, n_oct, SSM_STAGE_GROUPS)]
    for o in pairs[0]:
        input_proj(o)
    for n, octs in enumerate(pairs):
        if n + 1 < len(pairs):
            for o in pairs[n + 1]:
                input_proj(o)
        recurrence(octs)
        if n >= 1:
            output_proj(pairs[n - 1])
    output_proj(pairs[-1])
    sumsq = 0.0
    for o in range(n_oct):
        out = y_ref[o]
        sumsq = sumsq + jnp.sum(out * out, axis=-1, keepdims=True)
    inv = lax.rsqrt(sumsq * (1.0 / sw) + EPS)
    for o in range(n_oct):
        y_ref[o] = y_ref[o] * inv * g_ref[:, o * LANES:(o + 1) * LANES]
    for bi in range(batch):
        for o in range(n_oct):
            c0 = bi * sw + o * LANES
            o_ref[:, c0:c0 + LANES] = (
                y_ref[o, pl.ds(bi, n_tok, stride=batch), :].astype(o_ref.dtype))


def _ssm(u2, prm, li, batch):
    l, width = u2.shape
    sw = width // batch
    n_oct = sw // LANES
    n_state = prm["are"].shape[-1]
    rows = SSM_T * batch
    pair_rows = rows // 2
    full = lambda a: _layer_block(a, li, index_args=1)
    names = ["wb", "are", "aim", "wcre", "wcim", "wft", "d", "wglu", "bglu", "g"]
    return pl.pallas_call(
        functools.partial(_ssm_kernel, n_oct=n_oct, batch=batch),
        grid=(l // SSM_T,),
        in_specs=[pl.BlockSpec((SSM_T, width), lambda i: (i, 0))]
                 + [full(prm[n]) for n in names],
        out_specs=pl.BlockSpec((SSM_T, width), lambda i: (i, 0)),
        out_shape=jax.ShapeDtypeStruct((l, width), BF16),
        scratch_shapes=[pltpu.VMEM((n_oct, pair_rows, n_state // n_oct), F32),
                        pltpu.VMEM((n_oct, pair_rows, n_state // n_oct), F32),
                        pltpu.VMEM((2, batch, n_state), F32),
                        pltpu.VMEM((n_oct, batch, LANES), F32),
                        pltpu.VMEM((n_oct, rows, LANES), F32)],
        compiler_params=_cparams(("arbitrary",)),
        name="ssm",
    )(u2, *[prm[n] for n in names])


def _ssm_params(a_re, a_im, log_dt, b_re, b_im, c_re, c_im, d_skip, glu_w, glu_b, out_g):
    g, p, h = b_re.shape
    per_oct = LANES // h
    n_oct = g // per_oct
    dt = jnp.exp(log_dt)[:, None]
    mag = jnp.exp(dt * a_re)
    abar_re = mag * jnp.cos(dt * a_im)
    abar_im = mag * jnp.sin(dt * a_im)
    em_re = abar_re - 1.0
    em_im = abar_im
    den = a_re * a_re + a_im * a_im
    f_re = (em_re * a_re + em_im * a_im) / den
    f_im = (em_im * a_re - em_re * a_im) / den
    bb_re = f_re[..., None] * b_re - f_im[..., None] * b_im
    bb_im = f_re[..., None] * b_im + f_im[..., None] * b_re
    eye = jnp.eye(per_oct, dtype=F32)

    def in_blockdiag(bb):
        bb = bb.reshape(n_oct, per_oct, p, h)
        m = jnp.einsum("ogph,gk->oghkp", bb, eye)
        return m.reshape(n_oct, per_oct * h, per_oct * p)

    def out_blockdiag(c):
        c = c.reshape(n_oct, per_oct, h, p)
        m = jnp.einsum("oghp,gk->ogpkh", c, eye)
        return m.reshape(n_oct, per_oct * p, per_oct * h)

    def chan_blockdiag(m):
        m = jnp.einsum("oghk,gf->oghfk", m.reshape(n_oct, per_oct, h, h), eye)
        return m.reshape(n_oct, per_oct * h, per_oct * h)

    are_c, aim_c = abar_re[..., None], abar_im[..., None]
    ab_re, ab_im = are_c * bb_re - aim_c * bb_im, are_c * bb_im + aim_c * bb_re
    are_r, aim_r = abar_re[:, None, :], abar_im[:, None, :]
    ca_re, ca_im = c_re * are_r - c_im * aim_r, c_re * aim_r + c_im * are_r
    feed = (jnp.einsum("gop,gpi->gio", c_re, bb_re) - jnp.einsum("gop,gpi->gio", c_im, bb_im))
    wb = jnp.concatenate(
        [jnp.concatenate([in_blockdiag(ab_re), in_blockdiag(ab_im)], axis=-1),
         jnp.concatenate([in_blockdiag(bb_re), in_blockdiag(bb_im)], axis=-1)], axis=1)
    glu1 = chan_blockdiag(glu_w)
    zero = jnp.zeros_like(glu1)
    wglu = jnp.concatenate([jnp.concatenate([glu1, zero], axis=-1),
                            jnp.concatenate([zero, glu1], axis=-1)], axis=1)
    return {
        "wb": wb.astype(BF16),
        "are": (abar_re * abar_re - abar_im * abar_im).reshape(1, g * p),
        "aim": (2.0 * abar_re * abar_im).reshape(1, g * p),
        "wcre": jnp.concatenate([out_blockdiag(c_re), out_blockdiag(ca_re)], -1).astype(BF16),
        "wcim": jnp.concatenate([out_blockdiag(-c_im), out_blockdiag(-ca_im)], -1).astype(BF16),
        "wft": chan_blockdiag(feed).astype(BF16),
        "d": d_skip.reshape(1, g * h), "wglu": wglu.astype(BF16),
        "bglu": glu_b.reshape(1, g * h), "g": out_g.reshape(1, g * h),
    }


def _mix_ffn_kernel(x_ref, xh_ref, oa_ref, oah_ref, os_ref, osh_ref, ga_ref, wa_ref, ws_ref,
                    g1_ref, sh_ref, sc_ref, g2_ref, g_ref, wup_ref, cw_ref, cb_ref, wdn_ref,
                    o_ref, perm_ref, h_ref, up_ref, a_ref, *, d_ff):
    tm, d = x_ref.shape[1], x_ref.shape[2]
    n_grp = tm // SUBLANES
    n_slab = d // LANES
    n_chunk = d_ff // FFN_CK
    g, sh, sc = g_ref[...], sh_ref[0], sc_ref[0]

    def natural_rows(s):
        return slice(s * n_grp, (s + 1) * n_grp)

    def interleaved_rows(s):
        return pl.ds(s, n_grp, stride=SUBLANES)

    oa = jnp.concatenate([oa_ref[0], oah_ref[0]], axis=0).astype(F32)
    inv = lax.rsqrt(jnp.mean(oa * oa, axis=-1, keepdims=True) + EPS)
    mixed = (_dot((oa * inv * ga_ref[...]).astype(BF16), wa_ref[...])
             + _dot(jnp.concatenate([os_ref[...], osh_ref[...]], axis=0), ws_ref[...]))
    o_ref[0] = x_ref[0] + g1_ref[0] * mixed[:tm]
    x1_halo = xh_ref[0] + g1_ref[0] * mixed[tm:]

    hn = _norm_mod(o_ref[0], g, sh, sc)
    for slab in range(n_slab):
        lanes = slice(slab * LANES, (slab + 1) * LANES)
        for s in range(SUBLANES):
            perm_ref[slab, interleaved_rows(s), :] = hn[natural_rows(s), lanes]
    for slab in range(n_slab):
        h_ref[:tm, slab * LANES:(slab + 1) * LANES] = perm_ref[slab].astype(BF16)
    keep = (pl.program_id(1) > 0).astype(F32)
    h_ref[tm:, :] = (_norm_mod(x1_halo, g, sh, sc) * keep).astype(BF16)
    sublane = lax.broadcasted_iota(jnp.int32, (SUBLANES, FFN_CK), 0)

    def chunk(start):
        if not isinstance(start, int):
            start = pl.multiple_of(start, FFN_CK)
        return pl.ds(start, FFN_CK)

    def cols(c, part):
        return chunk(part * d_ff + c * FFN_CK)

    def wrapped(group, before):
        return jnp.where(sublane == 0, before, pltpu.roll(group, 1, axis=0))

    def up_proj(c, slot):
        for part in range(2):
            r = _dot(h_ref[...], wup_ref[:, cols(c, part)])
            pad = (CONV_WIDTH - 1) * SUBLANES
            up_ref[slot, part, pad:, :] = r[:tm]
            for back in range(1, CONV_WIDTH):
                up_ref[slot, part, pad - back * SUBLANES:pad - (back - 1) * SUBLANES, :] = wrapped(
                    r[tm - back * SUBLANES:tm - (back - 1) * SUBLANES],
                    r[tm + FFN_HALO - back:tm + FFN_HALO - back + 1])

    def conv(c, slot, part):
        out = cb_ref[:, cols(c, part)]
        for i in range(CONV_WIDTH):
            off = i * SUBLANES
            out = out + up_ref[slot, part, off:off + tm, :] * cw_ref[i:i + 1, cols(c, part)]
        return out

    def activate(c, slot):
        a_ref[:, chunk(c * FFN_CK)] = (
            jax.nn.gelu(conv(c, slot, 1)) * conv(c, slot, 0)).astype(BF16)

    for c in range(n_chunk):
        up_proj(c, c % 2)
        activate(c, c % 2)
    y = _dot(a_ref[...], wdn_ref[...])
    for slab in range(n_slab):
        perm_ref[slab] = y[:, slab * LANES:(slab + 1) * LANES]
    for slab in range(n_slab):
        lanes = slice(slab * LANES, (slab + 1) * LANES)
        for s in range(SUBLANES):
            o_ref[0, natural_rows(s), lanes] += (
                g2_ref[0, :, lanes] * perm_ref[slab, interleaved_rows(s), :])


def _mix_ffn(x, o_attn, o_ssm2, ga, w_out, li, g1, sh, sc, g2, g, wup, cw, cb, wdn):
    b, l, d = x.shape
    aw = o_attn.shape[-1]
    sw = o_ssm2.shape[-1] // b
    d_ff = wdn.shape[1]
    assert aw == sw
    w_half = lambda half: pl.BlockSpec((None, aw, d), lambda bi, i: (li, half, 0),
                                       pipeline_mode=pl.Buffered(1))
    tm = min(FFN_TM, l)
    halo_per_tile = tm // FFN_HALO
    halo = lambda i: jnp.maximum(i * halo_per_tile - 1, 0)
    bvec = pl.BlockSpec((1, 1, d), lambda bi, i: (bi, 0, 0))
    const = lambda a: pl.BlockSpec(a.shape, lambda bi, i: (0,) * a.ndim,
                                   pipeline_mode=pl.Buffered(1))
    return pl.pallas_call(
        functools.partial(_mix_ffn_kernel, d_ff=d_ff),
        grid=(b, l // tm),
        in_specs=[
            pl.BlockSpec((1, tm, d), lambda bi, i: (bi, i, 0)),
            pl.BlockSpec((1, FFN_HALO, d), lambda bi, i: (bi, halo(i), 0)),
            pl.BlockSpec((1, tm, aw), lambda bi, i: (bi, i, 0)),
            pl.BlockSpec((1, FFN_HALO, aw), lambda bi, i: (bi, halo(i), 0)),
            pl.BlockSpec((tm, sw), lambda bi, i: (i, bi)),
            pl.BlockSpec((FFN_HALO, sw), lambda bi, i: (halo(i), bi)),
            const(ga), w_half(0), w_half(1), bvec,
            bvec, bvec, bvec, const(g), _layer_block(wup, li), const(cw), const(cb),
            _layer_block(wdn, li),
        ],
        out_specs=pl.BlockSpec((1, tm, d), lambda bi, i: (bi, i, 0)),
        out_shape=jax.ShapeDtypeStruct(x.shape, F32),
        scratch_shapes=[pltpu.VMEM((d // LANES, tm, LANES), F32),
                        pltpu.VMEM((tm + FFN_HALO, d), BF16),
                        pltpu.VMEM((2, 2, tm + (CONV_WIDTH - 1) * SUBLANES, FFN_CK), F32),
                        pltpu.VMEM((tm, d_ff), BF16)],
        compiler_params=_cparams(("parallel", "parallel")),
        name="mix_ffn",
    )(x, x, o_attn, o_attn, o_ssm2, o_ssm2, ga, w_out, w_out, g1, sh, sc, g2, g, wup, cw, cb,
      wdn)


def kernel(x, c, ada_w, ada_b, norm1_g, w_in, q_norm_g, k_norm_g, ssm_a_re, ssm_a_im,
           ssm_log_dt, ssm_b_re, ssm_b_im, ssm_c_re, ssm_c_im, ssm_d, glu_w, glu_b,
           attn_out_g, ssm_out_g, w_out, norm2_g, ffn_w_up, ffn_conv_w, ffn_conv_b,
           ffn_w_down):
    b, l, d = x.shape
    depth = ada_w.shape[0]
    aw = attn_out_g.shape[-1]
    sw = ssm_out_g.shape[-1]
    n_heads = aw // HEAD_DIM
    assert b == SUBLANES and aw % LANES == 0 and sw % LANES == 0
    assert l % ATT_T == 0 and l % SSM_T == 0

    mod = _ada_mod(c, ada_w, ada_b)
    head = jnp.arange(aw) // HEAD_DIM
    ones_blk = (head[:MXU_TILE, None] == head[None, :MXU_TILE]).astype(BF16)
    inv_sqrt = LOG2_E / math.sqrt(HEAD_DIM)

    w_in_bf, w_out_bf = w_in.astype(BF16), w_out.astype(BF16)
    w_up_bf, w_down_bf = ffn_w_up.astype(BF16), ffn_w_down.astype(BF16)
    prm = jax.vmap(_ssm_params)(ssm_a_re, ssm_a_im, ssm_log_dt, ssm_b_re, ssm_b_im, ssm_c_re,
                                ssm_c_im, ssm_d, glu_w, glu_b, ssm_out_g)
    gq_all = jnp.tile(q_norm_g * inv_sqrt, (1, n_heads))
    gk_all = jnp.tile(k_norm_g, (1, n_heads))

    for li in range(depth):
        m = mod[li].reshape(b, N_MOD, 1, d)
        sh1, sc1, g1, sh2, sc2, g2 = (m[:, i] for i in range(N_MOD))
        q, k, v, u2 = _in_proj(x, sh1, sc1, norm1_g[li].reshape(1, d), w_in_bf, li,
                               gq_all[li].reshape(1, aw), gk_all[li].reshape(1, aw),
                               ones_blk, aw, sw)
        o_attn = _attn(q, k, v)
        o_ssm = _ssm(u2, prm, li, b)
        x = _mix_ffn(x, o_attn, o_ssm, attn_out_g[li].reshape(1, aw), w_out_bf, li, g1,
                     sh2, sc2, g2, norm2_g[li].reshape(1, d), w_up_bf,
                     ffn_conv_w[li], ffn_conv_b[li].reshape(1, -1), w_down_bf)
    return x
```
